```python
import jax
import jax.numpy as jnp
from jax import lax
import numpy as np


D_MODEL = 4096
BATCH = 2
SEQ = 8192
DEPTH = 2

CTX_LEN = 256
GRID_W = 64
HEAD_DIM = 128
ROPE_PAIRS = HEAD_DIM // 4
ROPE_THETA = 10000.0
A_HEADS = 8
A_KV_HEADS = 2
A_WINDOW = 128
A_BLOCK = 128
B_HEADS = 8
NA_ROWS = 8
NA_COLS = 16
C_HEADS = 8
C_CHUNK = 64
C_CONV = 5
BRANCH_W = 8 * HEAD_DIM
N_BRANCH = 3
D_FF = 7168
N_MOD = 9
EPS = 1e-6
NEG = -1e30
IN_SPLITS = (A_HEADS * HEAD_DIM, A_KV_HEADS * HEAD_DIM, A_KV_HEADS * HEAD_DIM,
             B_HEADS * HEAD_DIM, B_HEADS * HEAD_DIM, B_HEADS * HEAD_DIM,
             C_HEADS * HEAD_DIM, C_HEADS * HEAD_DIM, C_HEADS * HEAD_DIM, C_HEADS * HEAD_DIM,
             4 * C_HEADS)
IN_WIDTH = sum(IN_SPLITS)

kernel_name = 'hybrid_dit_gqa_natten_mlstm'


def rms_norm(x, g):
    xf = x.astype(jnp.float32)
    y = xf * lax.rsqrt(jnp.mean(xf * xf, axis=-1, keepdims=True) + EPS)
    return (y * g.astype(jnp.float32)).astype(x.dtype)


def modulate(x, g, shift, scale):
    return rms_norm(x, g) * (1 + scale) + shift


def swiglu(u, w1, w3, w2):
    return (jax.nn.silu(u @ w1) * (u @ w3)) @ w2


def split_heads(a, h):
    return a.reshape(a.shape[0], a.shape[1], h, HEAD_DIM)


def rope_tables(n):
    t = jnp.arange(n)
    inv = ROPE_THETA ** (-jnp.arange(ROPE_PAIRS, dtype=jnp.float32) / ROPE_PAIRS)
    row = (t // GRID_W).astype(jnp.float32)[:, None] * inv
    col = (t % GRID_W).astype(jnp.float32)[:, None] * inv
    ang = jnp.stack([row, col], axis=1)
    return jnp.cos(ang), jnp.sin(ang)


def rope_2d(x, cos, sin):
    b, n, h, _ = x.shape
    xr = x.reshape(b, n, h, 2, 2, ROPE_PAIRS)
    x1, x2 = xr[..., 0, :], xr[..., 1, :]
    c = cos[None, :, None].astype(x.dtype)
    s = sin[None, :, None].astype(x.dtype)
    out = jnp.stack([x1 * c - x2 * s, x1 * s + x2 * c], axis=-2)
    return out.reshape(b, n, h, HEAD_DIM)


def context_attention(q, k, v, sink):
    s = jnp.einsum('bqhgd,bkhd->bhgqk', q, k).astype(jnp.float32) * HEAD_DIM ** -0.5
    if sink is not None:
        snk = jnp.broadcast_to(sink.astype(jnp.float32)[None, :, :, None, None], s.shape[:-1] + (1,))
        p = jax.nn.softmax(jnp.concatenate([snk, s], axis=-1), axis=-1)[..., 1:]
    else:
        p = jax.nn.softmax(s, axis=-1)
    o = jnp.einsum('bhgqk,bkhd->bqhgd', p.astype(v.dtype), v)
    return o.reshape(o.shape[0], o.shape[1], -1)


def windowed_gqa(pc, pl, g_q, g_k, sink, cos, sin, last):
    qc, kc, vc = pc
    ql, kl, vl = pl
    bsz, n, _ = ql.shape
    lc = qc.shape[1]
    grp = A_HEADS // A_KV_HEADS
    sink = sink.reshape(A_KV_HEADS, grp)
    kc = rms_norm(split_heads(kc, A_KV_HEADS), g_k)
    vc = split_heads(vc, A_KV_HEADS)
    ql = rope_2d(rms_norm(split_heads(ql, A_HEADS), g_q), cos, sin)
    kl = rope_2d(rms_norm(split_heads(kl, A_KV_HEADS), g_k), cos, sin)
    vl = split_heads(vl, A_KV_HEADS)
    nb = n // A_BLOCK
    qb = ql.reshape(bsz, nb, A_BLOCK, A_KV_HEADS, grp, HEAD_DIM)
    pad = ((0, 0), (A_BLOCK, A_BLOCK), (0, 0), (0, 0))
    kp, vp = jnp.pad(kl, pad), jnp.pad(vl, pad)
    idx = jnp.arange(nb)[:, None] * A_BLOCK + jnp.arange(3 * A_BLOCK)[None, :]
    kw, vw = kp[:, idx], vp[:, idx]
    qpos = jnp.arange(nb)[:, None, None] * A_BLOCK + jnp.arange(A_BLOCK)[None, :, None]
    kpos = idx[:, None, :] - A_BLOCK
    band = (jnp.abs(qpos - kpos) <= A_WINDOW) & (kpos >= 0) & (kpos < n)
    scale = HEAD_DIM ** -0.5
    s_win = jnp.einsum('bnqhgd,bnkhd->bnhgqk', qb, kw).astype(jnp.float32) * scale
    s_win = jnp.where(band[None, :, None, None], s_win, NEG)
    s_ctx = jnp.einsum('bnqhgd,bkhd->bnhgqk', qb, kc).astype(jnp.float32) * scale
    snk = jnp.broadcast_to(sink.astype(jnp.float32)[None, None, :, :, None, None], s_ctx.shape[:-1] + (1,))
    p = jax.nn.softmax(jnp.concatenate([snk, s_ctx, s_win], axis=-1), axis=-1).astype(vl.dtype)
    p_ctx, p_win = p[..., 1:1 + lc], p[..., 1 + lc:]
    o = jnp.einsum('bnhgqk,bnkhd->bnqhgd', p_win, vw) + jnp.einsum('bnhgqk,bkhd->bnqhgd', p_ctx, vc)
    out_l = o.reshape(bsz, n, A_HEADS * HEAD_DIM)
    if last:
        return None, out_l
    qcg = rms_norm(split_heads(qc, A_HEADS), g_q).reshape(bsz, lc, A_KV_HEADS, grp, HEAD_DIM)
    return context_attention(qcg, kc, vc, sink), out_l


def neighbourhood_attn(pc, pl, g_q, g_k, relpos, last):
    qc, kc, vc = pc
    ql, kl, vl = pl
    bsz, n, _ = ql.shape
    lc = qc.shape[1]
    rows = n // GRID_W
    kr_n = min(NA_ROWS, rows)
    kc = rms_norm(split_heads(kc, B_HEADS), g_k)
    vc = split_heads(vc, B_HEADS)
    qg = rms_norm(split_heads(ql, B_HEADS), g_q).reshape(bsz, rows, GRID_W, B_HEADS, HEAD_DIM)
    kg = rms_norm(split_heads(kl, B_HEADS), g_k).reshape(bsz, rows, GRID_W, B_HEADS, HEAD_DIM)
    vg = split_heads(vl, B_HEADS).reshape(bsz, rows, GRID_W, B_HEADS, HEAD_DIM)
    r = jnp.arange(rows)
    row_idx = jnp.clip(r - kr_n // 2, 0, rows - kr_n)[:, None] + jnp.arange(kr_n)[None, :]
    col = jnp.arange(GRID_W)
    col_start = jnp.clip(col - NA_COLS // 2, 0, GRID_W - NA_COLS)
    col_in = (col[None, :] >= col_start[:, None]) & (col[None, :] < col_start[:, None] + NA_COLS)
    kr, vr = kg[:, row_idx], vg[:, row_idx]
    dr = row_idx - r[:, None] + NA_ROWS - 1
    dc = jnp.clip(col[None, :] - col[:, None], -(NA_COLS - 1), NA_COLS - 1) + NA_COLS - 1
    bias = relpos[:, dr[:, None, :, None], dc[None, :, None, :]]
    bias = jnp.moveaxis(bias, 0, 1)[None].astype(jnp.float32)
    scale = HEAD_DIM ** -0.5
    s_nb = jnp.einsum('brqhd,brjkhd->brhqjk', qg, kr).astype(jnp.float32) * scale + bias
    s_nb = jnp.where(col_in[:, None, :], s_nb, NEG).reshape(bsz, rows, B_HEADS, GRID_W, kr_n * GRID_W)
    s_ctx = jnp.einsum('brqhd,bkhd->brhqk', qg, kc).astype(jnp.float32) * scale
    p = jax.nn.softmax(jnp.concatenate([s_ctx, s_nb], axis=-1), axis=-1).astype(vl.dtype)
    p_ctx = p[..., :lc]
    p_nb = p[..., lc:].reshape(bsz, rows, B_HEADS, GRID_W, kr_n, GRID_W)
    o = jnp.einsum('brhqjk,brjkhd->brqhd', p_nb, vr) + jnp.einsum('brhqk,bkhd->brqhd', p_ctx, vc)
    out_l = o.reshape(bsz, n, B_HEADS * HEAD_DIM)
    if last:
        return None, out_l
    qcg = rms_norm(split_heads(qc, B_HEADS), g_q)[:, :, :, None, :]
    return context_attention(qcg, kc, vc, None), out_l


def centred_conv(x, w, b):
    t = x.shape[1]
    pad = C_CONV // 2
    xp = jnp.pad(x, ((0, 0), (pad, pad), (0, 0)))
    out = b
    for j in range(C_CONV):
        out = out + w[j] * xp[:, j:j + t]
    return out


def mlstm_prep(p, conv_w, conv_b, gate_b):
    q, k, v, o, g = p
    bsz, t, _ = q.shape
    qk = jax.nn.silu(centred_conv(jnp.concatenate([q, k], axis=-1), conv_w, conv_b))
    q, k = jnp.split(qk, 2, axis=-1)
    q, k, v = (split_heads(a, C_HEADS).astype(jnp.float32) for a in (q, k, v))
    g = (g.reshape(bsz, t, 4, C_HEADS) + gate_b).astype(jnp.float32)
    return q, k, v, o, g


def mlstm_scan(q, k, v, ig, fg, state0, want_out):
    bsz, t, h, d = q.shape
    nc = t // C_CHUNK

    def chunks(a):
        return jnp.moveaxis(a.reshape(bsz, nc, C_CHUNK, h, -1), 3, 1)

    qc, kc, vc = chunks(q), chunks(k), chunks(v)
    log_i = jnp.moveaxis(ig.reshape(bsz, nc, C_CHUNK, h), 3, 1)
    log_f = jax.nn.log_sigmoid(jnp.moveaxis(fg.reshape(bsz, nc, C_CHUNK, h), 3, 1))
    bcum = jnp.cumsum(log_f, axis=-1)
    b_last = bcum[..., -1]
    a = b_last[..., None] - bcum + log_i
    m_loc = jnp.max(a, axis=-1)
    w = jnp.exp(a - m_loc[..., None])
    c_loc = jnp.einsum('bhcl,bhcld,bhcle->bhcde', w, kc, vc)
    n_loc = jnp.einsum('bhcl,bhcld->bhcd', w, kc)

    def step(carry, inp):
        c_prev, n_prev, m_prev = carry
        c_l, n_l, m_l, b_l = inp
        m_new = jnp.maximum(b_l + m_prev, m_l)
        s_prev = jnp.exp(b_l + m_prev - m_new)
        s_loc = jnp.exp(m_l - m_new)
        c_new = s_prev[..., None, None] * c_prev + s_loc[..., None, None] * c_l
        n_new = s_prev[..., None] * n_prev + s_loc[..., None] * n_l
        return (c_new, n_new, m_new), (c_prev, n_prev, m_prev)

    xs = (jnp.moveaxis(c_loc, 2, 0), jnp.moveaxis(n_loc, 2, 0), jnp.moveaxis(m_loc, 2, 0), jnp.moveaxis(b_last, 2, 0))
    final, starts = lax.scan(step, state0, xs)
    if not want_out:
        return None, final
    c0, n0, m0 = (jnp.moveaxis(s, 0, 2) for s in starts)
    qs = qc * d ** -0.5
    dmat = bcum[..., :, None] - bcum[..., None, :] + log_i[..., None, :]
    tril = jnp.tril(jnp.ones((C_CHUNK, C_CHUNK), dtype=bool))
    dmat = jnp.where(tril, dmat, NEG)
    inter = bcum + m0[..., None]
    m = jnp.maximum(jnp.max(dmat, axis=-1), inter)
    sw = jnp.exp(dmat - m[..., None]) * jnp.einsum('bhcid,bhcjd->bhcij', qs, kc)
    si = jnp.exp(inter - m)
    num = jnp.einsum('bhcij,bhcje->bhcie', sw, vc) + si[..., None] * jnp.einsum('bhcid,bhcde->bhcie', qs, c0)
    den = jnp.sum(sw, axis=-1) + si * jnp.einsum('bhcid,bhcd->bhci', qs, n0)
    hout = num / jnp.maximum(jnp.abs(den), jnp.exp(-m))[..., None]
    return jnp.moveaxis(hout, 1, 3).reshape(bsz, t, h, d), final


def mlstm_out(hsum, o, g):
    hn = rms_norm(hsum, g.reshape(C_HEADS, HEAD_DIM))
    y = hn.reshape(hsum.shape[0], hsum.shape[1], -1) * jax.nn.sigmoid(o.astype(jnp.float32))
    return y.astype(o.dtype)


def bidir_mlstm(pc, pl, conv_w, conv_b, gate_b, norm_g, last):
    qc, kc, vc, oc, gc = mlstm_prep(pc, conv_w, conv_b, gate_b)
    ql, kl, vl, ol, gl = mlstm_prep(pl, conv_w, conv_b, gate_b)
    bsz = ql.shape[0]
    f32 = jnp.float32
    st0 = (jnp.zeros((bsz, C_HEADS, HEAD_DIM, HEAD_DIM), f32), jnp.zeros((bsz, C_HEADS, HEAD_DIM), f32),
           jnp.zeros((bsz, C_HEADS), f32))

    def rev(a):
        return a[:, ::-1]

    hc_f, sc_f = mlstm_scan(qc, kc, vc, gc[:, :, 0], gc[:, :, 1], st0, not last)
    hc_b, sc_b = mlstm_scan(rev(qc), rev(kc), rev(vc), rev(gc[:, :, 2]), rev(gc[:, :, 3]), st0, not last)
    hl_f, _ = mlstm_scan(ql, kl, vl, gl[:, :, 0], gl[:, :, 1], sc_f, True)
    hl_b, _ = mlstm_scan(rev(ql), rev(kl), rev(vl), rev(gl[:, :, 2]), rev(gl[:, :, 3]), sc_b, True)
    out_l = mlstm_out(hl_f + rev(hl_b), ol, norm_g)
    if last:
        return None, out_l
    return mlstm_out(hc_f + rev(hc_b), oc, norm_g), out_l


def gated_merge(u, branches, w_gate, b_gate, w_branch, w_out):
    y = None
    for j, o in enumerate(branches):
        term = jax.nn.sigmoid(u @ w_gate[j] + b_gate[j]) * (o @ w_branch[j])
        y = term if y is None else y + term
    return y @ w_out


def setup_inputs(seed: int = 0) -> dict:
    key = jax.random.key(seed)
    ks = jax.random.split(key, 24)
    f32 = jnp.float32
    d = D_MODEL

    def nrm(k, shape, scale):
        return jax.random.normal(k, shape, f32) * scale

    f_base = jnp.array([0.0, 1.0, 0.0, 1.0], f32)[:, None] * jnp.linspace(3.0, 6.0, C_HEADS, dtype=f32)[None, :]
    return {
        'x': nrm(ks[0], (BATCH, SEQ, d), 1.0),
        'c': nrm(ks[1], (BATCH, d), 1.0),
        'ctx': nrm(ks[2], (BATCH, CTX_LEN, d), 1.0),
        'c_ctx': nrm(ks[3], (d,), 1.0),
        'w_mod': nrm(ks[4], (DEPTH, d, N_MOD * d), 0.5 * d ** -0.5),
        'b_mod': nrm(ks[5], (DEPTH, N_MOD * d), 0.02),
        'norm_g': 1.0 + nrm(ks[6], (DEPTH, 3, d), 0.02),
        'ffn_w1': nrm(ks[7], (DEPTH, 2, d, D_FF), d ** -0.5),
        'ffn_w3': nrm(ks[8], (DEPTH, 2, d, D_FF), d ** -0.5),
        'ffn_w2': nrm(ks[9], (DEPTH, 2, D_FF, d), D_FF ** -0.5),
        'w_in': nrm(ks[10], (DEPTH, d, IN_WIDTH), d ** -0.5),
        'qk_g': 1.0 + nrm(ks[11], (DEPTH, 4, HEAD_DIM), 0.02),
        'attn_sink': nrm(ks[12], (DEPTH, A_HEADS), 0.5),
        'na_relpos': nrm(ks[13], (DEPTH, B_HEADS, 2 * NA_ROWS - 1, 2 * NA_COLS - 1), 0.5),
        'mlstm_conv_w': nrm(ks[14], (DEPTH, C_CONV, 2 * C_HEADS * HEAD_DIM), C_CONV ** -0.5),
        'mlstm_conv_b': nrm(ks[15], (DEPTH, 2 * C_HEADS * HEAD_DIM), 0.02),
        'mlstm_gate_b': f_base[None] + nrm(ks[16], (DEPTH, 4, C_HEADS), 0.1),
        'mlstm_norm_g': 1.0 + nrm(ks[17], (DEPTH, C_HEADS * HEAD_DIM), 0.02),
        'w_gate': nrm(ks[18], (DEPTH, N_BRANCH, d, d), d ** -0.5),
        'b_gate': nrm(ks[19], (DEPTH, N_BRANCH, d), 0.02),
        'w_branch': nrm(ks[20], (DEPTH, N_BRANCH, BRANCH_W, d), BRANCH_W ** -0.5),
        'w_out': nrm(ks[21], (DEPTH, d, d), d ** -0.5),
    }


def reference(x, c, ctx, c_ctx, w_mod, b_mod, norm_g, ffn_w1, ffn_w3, ffn_w2, w_in, qk_g, attn_sink,
              na_relpos, mlstm_conv_w, mlstm_conv_b, mlstm_gate_b, mlstm_norm_g, w_gate, b_gate, w_branch, w_out):
    cos, sin = rope_tables(x.shape[1])
    splits = [int(s) for s in np.cumsum(IN_SPLITS)[:-1]]
    xc, xl = ctx, x
    for i in range(DEPTH):
        last = i == DEPTH - 1
        g = norm_g[i]
        mod_l = (jax.nn.silu(c) @ w_mod[i] + b_mod[i])[:, None, :]
        mod_c = (jax.nn.silu(c_ctx) @ w_mod[i] + b_mod[i])[None, None, :]
        ml = jnp.split(mod_l, N_MOD, axis=-1)
        mc = jnp.split(mod_c, N_MOD, axis=-1)
        xc = xc + 0.5 * mc[2] * swiglu(modulate(xc, g[0], mc[0], mc[1]), ffn_w1[i, 0], ffn_w3[i, 0], ffn_w2[i, 0])
        xl = xl + 0.5 * ml[2] * swiglu(modulate(xl, g[0], ml[0], ml[1]), ffn_w1[i, 0], ffn_w3[i, 0], ffn_w2[i, 0])
        uc = modulate(xc, g[1], mc[3], mc[4])
        ul = modulate(xl, g[1], ml[3], ml[4])
        pc = jnp.split(uc @ w_in[i], splits, axis=-1)
        pl = jnp.split(ul @ w_in[i], splits, axis=-1)
        a_c, a_l = windowed_gqa(pc[0:3], pl[0:3], qk_g[i, 0], qk_g[i, 1], attn_sink[i], cos, sin, last)
        n_c, n_l = neighbourhood_attn(pc[3:6], pl[3:6], qk_g[i, 2], qk_g[i, 3], na_relpos[i], last)
        m_c, m_l = bidir_mlstm(pc[6:11], pl[6:11], mlstm_conv_w[i], mlstm_conv_b[i], mlstm_gate_b[i],
                               mlstm_norm_g[i], last)
        xl = xl + ml[5] * gated_merge(ul, (a_l, n_l, m_l), w_gate[i], b_gate[i], w_branch[i], w_out[i])
        xl = xl + 0.5 * ml[8] * swiglu(modulate(xl, g[2], ml[6], ml[7]), ffn_w1[i, 1], ffn_w3[i, 1], ffn_w2[i, 1])
        if not last:
            xc = xc + mc[5] * gated_merge(uc, (a_c, n_c, m_c), w_gate[i], b_gate[i], w_branch[i], w_out[i])
            xc = xc + 0.5 * mc[8] * swiglu(modulate(xc, g[2], mc[6], mc[7]), ffn_w1[i, 1], ffn_w3[i, 1],
                                           ffn_w2[i, 1])
    return xl
```

```python
import functools

import numpy as np
import jax
import jax.numpy as jnp
from jax import lax
from jax.experimental import pallas as pl
from jax.experimental.pallas import tpu as pltpu

F32 = jnp.float32
BF16 = jnp.bfloat16

HEAD_DIM = 128
GRID_W = 64
ROPE_PAIRS = HEAD_DIM // 4
ROPE_THETA = 10000.0
A_HEADS = 8
A_KV_HEADS = 2
A_GROUP = A_HEADS // A_KV_HEADS
A_WINDOW = 128
A_BLOCK = 128
B_HEADS = 8
NA_ROWS = 8
NA_COLS = 16
C_HEADS = 8
C_CONV = 5
BRANCH_W = 8 * HEAD_DIM
N_MOD = 9
EPS = 1e-6
NEG = -1e30
ATTN_SCALE = HEAD_DIM ** -0.5

M_CHUNK = 128
NA_TILE_ROWS = 8
NA_TILE = NA_TILE_ROWS * GRID_W

V7X_VMEM_BYTES = 64 * 1024 * 1024
VMEM_CAP = V7X_VMEM_BYTES - 6 * 1024 * 1024

COL_AQ, COL_BQ, COL_BK, COL_BV, COL_CQ, COL_CK, COL_CV, COL_CO = range(8)
MAIN_W = 8 * BRANCH_W + 2 * A_KV_HEADS * HEAD_DIM
GATE_W = 128


def _pick(n, cands):
    for c in cands:
        if n % c == 0:
            return c
    raise ValueError(f"no tile in {cands} divides {n}")


def _params(sem, block_bytes, temp_bytes=0):
    limit = 2 * block_bytes + temp_bytes + 4 * 1024 * 1024
    return pltpu.CompilerParams(dimension_semantics=sem,
                                vmem_limit_bytes=int(min(max(limit, 16 * 1024 * 1024), VMEM_CAP)))


def _nbytes(shape, dtype):
    return int(np.prod(shape)) * jnp.dtype(dtype).itemsize


def _dot(a, b):
    return jnp.dot(a, b, preferred_element_type=F32)


def _dot_nt(a, b):
    return lax.dot_general(a, b, (((1,), (1,)), ((), ())), preferred_element_type=F32)


def _dot_tn(a, b):
    return lax.dot_general(a, b, (((0,), (0,)), ((), ())), preferred_element_type=F32)


def _silu(x):
    return x * jax.nn.sigmoid(x)


def _mod_kernel(c_ref, w_ref, b_ref, o_ref):
    a = _silu(c_ref[...]).astype(BF16)
    o_ref[0] = _dot(a, w_ref[0].astype(BF16)) + b_ref[0]


def _mod_call(cs, w_mod, b_mod):
    depth, d, nd = w_mod.shape
    tn = _pick(nd, (512, 256, 128))
    rows = cs.shape[0]
    blocks = _nbytes((d, tn), F32) + _nbytes((rows, d), F32) + _nbytes((rows, tn), F32)
    return pl.pallas_call(
        _mod_kernel,
        out_shape=jax.ShapeDtypeStruct((depth, rows, nd), F32),
        grid=(depth, nd // tn),
        in_specs=[pl.BlockSpec((rows, d), lambda l, j: (0, 0)),
                  pl.BlockSpec((1, d, tn), lambda l, j: (l, 0, j)),
                  pl.BlockSpec((1, 1, tn), lambda l, j: (l, 0, j))],
        out_specs=pl.BlockSpec((1, rows, tn), lambda l, j: (l, 0, j)),
        compiler_params=_params(("parallel", "parallel"), blocks, _nbytes((d, tn), BF16)),
        name="mod_vectors",
    )(cs, w_mod, b_mod.reshape(depth, 1, nd))


def _normmod_kernel(x_ref, g_ref, shift_ref, scale_ref, o_ref):
    x = x_ref[...]
    ms = jnp.mean(x * x, axis=-1, keepdims=True)
    y = x * lax.rsqrt(ms + EPS) * g_ref[...]
    o_ref[...] = (y * (1.0 + scale_ref[0]) + shift_ref[0]).astype(o_ref.dtype)


def _normmod_call(x, g, shift, scale):
    m, d = x.shape
    groups = shift.shape[0]
    tr = _pick(m // groups, (256, 128, 64, 8))
    per = (m // groups) // tr
    blocks = _nbytes((tr, d), F32) + _nbytes((tr, d), BF16) + 3 * _nbytes((1, d), F32)
    return pl.pallas_call(
        _normmod_kernel,
        out_shape=jax.ShapeDtypeStruct((m, d), BF16),
        grid=(m // tr,),
        in_specs=[pl.BlockSpec((tr, d), lambda i: (i, 0)),
                  pl.BlockSpec((1, d), lambda i: (0, 0)),
                  pl.BlockSpec((1, 1, d), lambda i: (i // per, 0, 0)),
                  pl.BlockSpec((1, 1, d), lambda i: (i // per, 0, 0))],
        out_specs=pl.BlockSpec((tr, d), lambda i: (i, 0)),
        compiler_params=_params(("parallel",), blocks, 2 * _nbytes((tr, d), F32)),
        name="norm_modulate",
    )(x, g.reshape(1, d), shift.reshape(groups, 1, d), scale.reshape(groups, 1, d))


def _up_kernel(u_ref, w1_ref, w3_ref, o_ref):
    u = u_ref[...]
    h1 = _dot(u, w1_ref[...])
    h3 = _dot(u, w3_ref[...])
    o_ref[...] = (_silu(h1) * h3).astype(o_ref.dtype)


def _up_call(u, w1, w3):
    m, d = u.shape
    f = w1.shape[1]
    tm = _pick(m, (1024, 512, 256))
    tf = _pick(f, (512, 256, 128))
    blocks = _nbytes((tm, d), BF16) + 2 * _nbytes((d, tf), BF16) + _nbytes((tm, tf), BF16)
    return pl.pallas_call(
        _up_kernel,
        out_shape=jax.ShapeDtypeStruct((m, f), BF16),
        grid=(m // tm, f // tf),
        in_specs=[pl.BlockSpec((tm, d), lambda i, j: (i, 0)),
                  pl.BlockSpec((d, tf), lambda i, j: (0, j)),
                  pl.BlockSpec((d, tf), lambda i, j: (0, j))],
        out_specs=pl.BlockSpec((tm, tf), lambda i, j: (i, j)),
        compiler_params=_params(("parallel", "parallel"), blocks, 4 * _nbytes((tm, tf), F32)),
        name="ffn_up",
    )(u, w1, w3)


def _down_kernel(a_ref, w_ref, x_ref, gate_ref, o_ref, *, coef):
    acc = _dot(a_ref[...], w_ref[...])
    o_ref[...] = x_ref[...] + (coef * gate_ref[0]) * acc


def _down_call(a, w, x, gate, coef):
    m, k = a.shape
    d = w.shape[1]
    groups = gate.shape[0]
    tm = _pick(m // groups, (1024, 512, 256))
    tn = _pick(d, (512, 256, 128) if k <= 4096 else (256, 128))
    per = (m // groups) // tm
    blocks = (_nbytes((tm, k), BF16) + _nbytes((k, tn), BF16) + 2 * _nbytes((tm, tn), F32)
              + _nbytes((1, tn), F32))
    return pl.pallas_call(
        functools.partial(_down_kernel, coef=coef),
        out_shape=jax.ShapeDtypeStruct((m, d), F32),
        grid=(m // tm, d // tn),
        in_specs=[pl.BlockSpec((tm, k), lambda i, j: (i, 0)),
                  pl.BlockSpec((k, tn), lambda i, j: (0, j)),
                  pl.BlockSpec((tm, tn), lambda i, j: (i, j)),
                  pl.BlockSpec((1, 1, tn), lambda i, j: (i // per, 0, j))],
        out_specs=pl.BlockSpec((tm, tn), lambda i, j: (i, j)),
        compiler_params=_params(("parallel", "parallel"), blocks, 2 * _nbytes((tm, tn), F32)),
        name="proj_residual",
    )(a, w, x, gate.reshape(groups, 1, d))


def _matmul_kernel(a_ref, w_ref, o_ref):
    o_ref[...] = _dot(a_ref[...], w_ref[...]).astype(o_ref.dtype)


def _matmul_call(a, w, out_dtype=F32):
    m, k = a.shape
    n = w.shape[1]
    tm = _pick(m, (1024, 512, 256))
    tn = _pick(n, (512, 256, 128))
    blocks = _nbytes((tm, k), BF16) + _nbytes((k, tn), BF16) + _nbytes((tm, tn), out_dtype)
    return pl.pallas_call(
        _matmul_kernel,
        out_shape=jax.ShapeDtypeStruct((m, n), out_dtype),
        grid=(m // tm, n // tn),
        in_specs=[pl.BlockSpec((tm, k), lambda i, j: (i, 0)),
                  pl.BlockSpec((k, tn), lambda i, j: (0, j))],
        out_specs=pl.BlockSpec((tm, tn), lambda i, j: (i, j)),
        compiler_params=_params(("parallel", "parallel"), blocks, _nbytes((tm, tn), F32)),
        name="in_proj",
    )(a, w)


def _merge_kernel(u_ref, oa_ref, ob_ref, oc_ref, wg_ref, bg_ref, wb_ref, y_ref):
    u = u_ref[...]
    acc = None
    for j, br_ref in enumerate((oa_ref, ob_ref, oc_ref)):
        gate = jax.nn.sigmoid(_dot(u, wg_ref[j]) + bg_ref[j])
        term = gate * _dot(br_ref[...], wb_ref[j])
        acc = term if acc is None else acc + term
    y_ref[...] = acc.astype(y_ref.dtype)


def _merge_call(u, oa, ob, oc, wg, bg, wb):
    m, d = u.shape
    bw = oa.shape[1]
    tm = _pick(m, (1024, 512, 256))
    tn = _pick(d, (256, 128))
    blocks = (_nbytes((tm, d), BF16) + 3 * _nbytes((tm, bw), BF16) + 3 * _nbytes((d, tn), BF16)
              + 3 * _nbytes((bw, tn), BF16) + _nbytes((tm, tn), BF16))
    return pl.pallas_call(
        _merge_kernel,
        out_shape=jax.ShapeDtypeStruct((m, d), BF16),
        grid=(m // tm, d // tn),
        in_specs=[pl.BlockSpec((tm, d), lambda i, j: (i, 0)),
                  pl.BlockSpec((tm, bw), lambda i, j: (i, 0)),
                  pl.BlockSpec((tm, bw), lambda i, j: (i, 0)),
                  pl.BlockSpec((tm, bw), lambda i, j: (i, 0)),
                  pl.BlockSpec((3, d, tn), lambda i, j: (0, 0, j)),
                  pl.BlockSpec((3, 1, tn), lambda i, j: (0, 0, j)),
                  pl.BlockSpec((3, bw, tn), lambda i, j: (0, 0, j))],
        out_specs=pl.BlockSpec((tm, tn), lambda i, j: (i, j)),
        compiler_params=_params(("parallel", "parallel"), blocks, 10 * _nbytes((tm, tn), F32)),
        name="gated_merge",
    )(u, oa, ob, oc, wg, bg.reshape(3, 1, d), wb)


def _head_norm(x, g):
    ms = jnp.mean(x * x, axis=-1, keepdims=True)
    return x * lax.rsqrt(ms + EPS) * g


def _rope(y, cos, sin_signed):
    lane = lax.broadcasted_iota(jnp.int32, y.shape, 1)
    partner = jnp.where((lane % 64) < ROPE_PAIRS, pltpu.roll(y, HEAD_DIM - ROPE_PAIRS, axis=1),
                        pltpu.roll(y, ROPE_PAIRS, axis=1))
    return y * cos + partner * sin_signed


def _qkprep_kernel(p4_ref, pa_ref, g_ref, cos_ref, sin_ref,
                   qa_ref, qb_ref, kb_ref, vb_ref, ka_ref, va_ref, *, rope):
    hd = HEAD_DIM
    if rope:
        cos = cos_ref[...]
        sin = sin_ref[...]
    for h in range(A_HEADS):
        y = _head_norm(p4_ref[:, h * hd:(h + 1) * hd], g_ref[0:1, :])
        if rope:
            y = _rope(y, cos, sin)
        qa_ref[:, h * hd:(h + 1) * hd] = y.astype(BF16)
    for h in range(B_HEADS):
        c0 = BRANCH_W + h * hd
        qb_ref[:, h * hd:(h + 1) * hd] = _head_norm(p4_ref[:, c0:c0 + hd], g_ref[2:3, :]).astype(BF16)
        c0 = 2 * BRANCH_W + h * hd
        kb_ref[:, h * hd:(h + 1) * hd] = _head_norm(p4_ref[:, c0:c0 + hd], g_ref[3:4, :]).astype(BF16)
    vb_ref[...] = p4_ref[:, 3 * BRANCH_W:4 * BRANCH_W].astype(BF16)
    for h in range(A_KV_HEADS):
        y = _head_norm(pa_ref[:, h * hd:(h + 1) * hd], g_ref[1:2, :])
        if rope:
            y = _rope(y, cos, sin)
        ka_ref[:, h * hd:(h + 1) * hd] = y.astype(BF16)
    kvw = A_KV_HEADS * hd
    va_ref[...] = pa_ref[:, kvw:2 * kvw].astype(BF16)


def _qkprep_call(p_main, qk_g, cos_t, sin_t, rope, seq):
    m = p_main.shape[0]
    tr = _pick(seq, (256, 128))
    per = seq // tr
    kvw = A_KV_HEADS * HEAD_DIM
    w4 = 4 * BRANCH_W
    blocks = (_nbytes((tr, w4), F32) + _nbytes((tr, 2 * kvw), F32) + 2 * _nbytes((tr, HEAD_DIM), F32)
              + _nbytes((tr, w4), BF16) + _nbytes((tr, 2 * kvw), BF16))
    outs = [jax.ShapeDtypeStruct((m, BRANCH_W), BF16)] * 4 + [jax.ShapeDtypeStruct((m, kvw), BF16)] * 2
    return pl.pallas_call(
        functools.partial(_qkprep_kernel, rope=rope),
        out_shape=outs,
        grid=(m // tr,),
        in_specs=[pl.BlockSpec((tr, w4), lambda i: (i, 0)),
                  pl.BlockSpec((tr, 2 * kvw), lambda i: (i, (8 * BRANCH_W) // (2 * kvw))),
                  pl.BlockSpec((4, HEAD_DIM), lambda i: (0, 0)),
                  pl.BlockSpec((tr, HEAD_DIM), lambda i: (i % per, 0)),
                  pl.BlockSpec((tr, HEAD_DIM), lambda i: (i % per, 0))],
        out_specs=[pl.BlockSpec((tr, BRANCH_W), lambda i: (i, 0))] * 4
                  + [pl.BlockSpec((tr, kvw), lambda i: (i, 0))] * 2,
        compiler_params=_params(("parallel",), blocks, 4 * _nbytes((tr, HEAD_DIM), F32)),
        name="qk_prep",
    )(p_main, p_main, qk_g, cos_t, sin_t)


def _attn_a_kernel(sink_ref, q_ref, *rest, n_tok, lc, window):
    if window:
        kp_ref, kc_ref, kn_ref, vp_ref, vc_ref, vn_ref, kx_ref, vx_ref, o_ref = rest
        kall = jnp.concatenate([kx_ref[...], kp_ref[...], kc_ref[...], kn_ref[...]], axis=0)
        vall = jnp.concatenate([vx_ref[...], vp_ref[...], vc_ref[...], vn_ref[...]], axis=0)
    else:
        kx_ref, vx_ref, o_ref = rest
        kall = kx_ref[...]
        vall = vx_ref[...]
    i = pl.program_id(1)
    hk = pl.program_id(2)
    hd = HEAD_DIM
    blk = A_BLOCK
    q4 = jnp.concatenate([q_ref[:, g * hd:(g + 1) * hd] for g in range(A_GROUP)], axis=0)
    s = _dot_nt(q4, kall) * ATTN_SCALE
    if window:
        row = lax.broadcasted_iota(jnp.int32, s.shape, 0) % blk
        col = lax.broadcasted_iota(jnp.int32, s.shape, 1)
        qpos = i * blk + row
        kpos = (i - 1) * blk + (col - lc)
        band = (jnp.abs(qpos - kpos) <= A_WINDOW) & (kpos >= 0) & (kpos < n_tok)
        s = jnp.where((col < lc) | band, s, NEG)
    snk = jnp.concatenate([jnp.full((blk, 1), sink_ref[hk * A_GROUP + g], F32) for g in range(A_GROUP)],
                          axis=0)
    mx = jnp.maximum(jnp.max(s, axis=-1, keepdims=True), snk)
    e = jnp.exp(s - mx)
    den = jnp.sum(e, axis=-1, keepdims=True) + jnp.exp(snk - mx)
    p = (e * (1.0 / den)).astype(BF16)
    o = _dot(p, vall)
    for g in range(A_GROUP):
        o_ref[:, g * hd:(g + 1) * hd] = o[g * blk:(g + 1) * blk].astype(o_ref.dtype)


def _attn_a_call(qa, ka, va, kx, vx, sink, bsz, window):
    n = qa.shape[0] // bsz
    lc = kx.shape[0] // bsz
    blk = A_BLOCK
    nb = n // blk
    hd = HEAD_DIM
    gw = A_GROUP * hd
    qspec = pl.BlockSpec((blk, gw), lambda b, i, h: (b * nb + i, h))
    xspec = pl.BlockSpec((lc, hd), lambda b, i, h: (b, h))
    sspec = pl.BlockSpec(memory_space=pltpu.SMEM)
    if window:
        prev = pl.BlockSpec((blk, hd), lambda b, i, h: (b * nb + jnp.maximum(i - 1, 0), h))
        cur = pl.BlockSpec((blk, hd), lambda b, i, h: (b * nb + i, h))
        nxt = pl.BlockSpec((blk, hd), lambda b, i, h: (b * nb + jnp.minimum(i + 1, nb - 1), h))
        in_specs = [sspec, qspec, prev, cur, nxt, prev, cur, nxt, xspec, xspec]
        args = (sink, qa, ka, ka, ka, va, va, va, kx, vx)
        keys = lc + 3 * blk
    else:
        in_specs = [sspec, qspec, xspec, xspec]
        args = (sink, qa, kx, vx)
        keys = lc
    blocks = 2 * _nbytes((blk, gw), BF16) + 2 * _nbytes((keys, hd), BF16)
    return pl.pallas_call(
        functools.partial(_attn_a_kernel, n_tok=n, lc=lc, window=window),
        out_shape=jax.ShapeDtypeStruct(qa.shape, BF16),
        grid=(bsz, nb, A_KV_HEADS),
        in_specs=in_specs,
        out_specs=qspec,
        compiler_params=_params(("parallel", "parallel", "parallel"), blocks,
                                6 * _nbytes((A_GROUP * blk, keys), F32)),
        name="windowed_gqa" if window else "context_gqa",
    )(*args)


def _softmax2(s1, s2):
    mx = jnp.maximum(jnp.max(s1, axis=-1, keepdims=True), jnp.max(s2, axis=-1, keepdims=True))
    e1 = jnp.exp(s1 - mx)
    e2 = jnp.exp(s2 - mx)
    inv = 1.0 / (jnp.sum(e1, axis=-1, keepdims=True) + jnp.sum(e2, axis=-1, keepdims=True))
    return (e1 * inv).astype(BF16), (e2 * inv).astype(BF16)


def _attn_b_kernel(q_ref, kp_ref, kc_ref, kn_ref, vp_ref, vc_ref, vn_ref, kx_ref, vx_ref, bias_ref,
                   o_ref, kbuf, vbuf, *, rows):
    t = pl.program_id(1)
    hd = HEAD_DIM
    nt = NA_TILE
    for s, (kr, vr) in enumerate(((kp_ref, vp_ref), (kc_ref, vc_ref), (kn_ref, vn_ref))):
        kbuf[s * nt:(s + 1) * nt, :] = kr[...]
        vbuf[s * nt:(s + 1) * nt, :] = vr[...]

    def row_body(j, carry):
        r = t * NA_TILE_ROWS + j
        rs = jnp.clip(r - NA_ROWS // 2, 0, rows - NA_ROWS)
        off = pl.multiple_of((rs - (t - 1) * NA_TILE_ROWS) * GRID_W, GRID_W)
        d0 = rs - r + NA_ROWS - 1
        qoff = pl.multiple_of(j * GRID_W, GRID_W)
        for h in range(B_HEADS):
            cs = slice(h * hd, (h + 1) * hd)
            q = q_ref[pl.ds(qoff, GRID_W), cs]
            k = kbuf[pl.ds(off, NA_ROWS * GRID_W), cs]
            v = vbuf[pl.ds(off, NA_ROWS * GRID_W), cs]
            s_nb = _dot_nt(q, k) * ATTN_SCALE + bias_ref[h, d0]
            s_cx = _dot_nt(q, kx_ref[:, cs]) * ATTN_SCALE
            p_nb, p_cx = _softmax2(s_nb, s_cx)
            o = _dot(p_nb, v) + _dot(p_cx, vx_ref[:, cs])
            o_ref[pl.ds(qoff, GRID_W), cs] = o.astype(o_ref.dtype)
        return carry

    lax.fori_loop(0, NA_TILE_ROWS, row_body, 0)


def _attn_bx_kernel(q_ref, kx_ref, vx_ref, o_ref):
    hd = HEAD_DIM
    for h in range(B_HEADS):
        cs = slice(h * hd, (h + 1) * hd)
        s = _dot_nt(q_ref[:, cs], kx_ref[:, cs]) * ATTN_SCALE
        mx = jnp.max(s, axis=-1, keepdims=True)
        e = jnp.exp(s - mx)
        p = (e * (1.0 / jnp.sum(e, axis=-1, keepdims=True))).astype(BF16)
        o_ref[:, cs] = _dot(p, vx_ref[:, cs]).astype(o_ref.dtype)


def _na_bias_table(relpos):
    col = np.arange(GRID_W)
    col_start = np.clip(col - NA_COLS // 2, 0, GRID_W - NA_COLS)
    col_in = (col[None, :] >= col_start[:, None]) & (col[None, :] < col_start[:, None] + NA_COLS)
    dc = np.clip(col[None, :] - col[:, None], -(NA_COLS - 1), NA_COLS - 1) + NA_COLS - 1
    dr = np.arange(NA_ROWS)[:, None] + np.arange(NA_ROWS)[None, :]
    tab = relpos[:, dr[:, None, :, None], dc[None, :, None, :]]
    tab = jnp.where(col_in[None, None, :, None, :], tab.astype(F32), NEG)
    return tab.reshape(relpos.shape[0], NA_ROWS, GRID_W, NA_ROWS * GRID_W)


def _attn_b_call(qb, kb, vb, kx, vx, bias, bsz):
    n = qb.shape[0] // bsz
    lc = kx.shape[0] // bsz
    rows = n // GRID_W
    assert rows % NA_TILE_ROWS == 0 and rows >= NA_ROWS
    nt = rows // NA_TILE_ROWS
    w = qb.shape[1]
    cur = pl.BlockSpec((NA_TILE, w), lambda b, t: (b * nt + t, 0))
    prev = pl.BlockSpec((NA_TILE, w), lambda b, t: (b * nt + jnp.maximum(t - 1, 0), 0))
    nxt = pl.BlockSpec((NA_TILE, w), lambda b, t: (b * nt + jnp.minimum(t + 1, nt - 1), 0))
    xspec = pl.BlockSpec((lc, w), lambda b, t: (b, 0))
    bspec = pl.BlockSpec(bias.shape, lambda b, t: (0, 0, 0, 0))
    blocks = 8 * _nbytes((NA_TILE, w), BF16) + 2 * _nbytes((lc, w), BF16) + _nbytes(bias.shape, F32)
    scratch = 2 * _nbytes((3 * NA_TILE, w), BF16)
    return pl.pallas_call(
        functools.partial(_attn_b_kernel, rows=rows),
        out_shape=jax.ShapeDtypeStruct(qb.shape, BF16),
        grid=(bsz, nt),
        in_specs=[cur, prev, cur, nxt, prev, cur, nxt, xspec, xspec, bspec],
        out_specs=cur,
        scratch_shapes=[pltpu.VMEM((3 * NA_TILE, w), BF16), pltpu.VMEM((3 * NA_TILE, w), BF16)],
        compiler_params=_params(("parallel", "parallel"), blocks, scratch + 4 * 1024 * 1024),
        name="neighbourhood_attn",
    )(qb, kb, kb, kb, vb, vb, vb, kx, vx, bias)


def _attn_bx_call(qx, kx, vx, bsz):
    lc = qx.shape[0] // bsz
    w = qx.shape[1]
    spec = pl.BlockSpec((lc, w), lambda b: (b, 0))
    return pl.pallas_call(
        _attn_bx_kernel,
        out_shape=jax.ShapeDtypeStruct(qx.shape, BF16),
        grid=(bsz,),
        in_specs=[spec, spec, spec],
        out_specs=spec,
        compiler_params=_params(("parallel",), 4 * _nbytes((lc, w), BF16), 4 * 1024 * 1024),
        name="context_full_attn",
    )(qx, kx, vx)


def _log_sigmoid(x):
    return -(jnp.maximum(-x, 0.0) + jnp.log1p(jnp.exp(-jnp.abs(x))))


def _mlstm_kernel(q_ref, k_ref, v_ref, qlo_ref, qhi_ref, klo_ref, khi_ref, g_ref, gb_ref, cw_ref, cb_ref,
                  s0_ref, m0_ref, h_ref, sf_ref, mf_ref, cst, mst, ext, *, nc):
    d = pl.program_id(1)
    c = pl.program_id(2)
    ci = jnp.where(d == 0, c, nc - 1 - c)
    L = M_CHUNK
    hd = HEAD_DIM
    hw = C_HEADS * hd
    nh = C_HEADS
    halo = 8
    pad = C_CONV // 2

    @pl.when(c == 0)
    def _():
        cst[...] = s0_ref[...]
        mst[...] = m0_ref[...]

    def conv_silu(x_ref, lo_ref, hi_ref, col0):
        ext[0:halo, :] = jnp.where(ci > 0, lo_ref[...], 0.0)
        ext[halo:halo + L, :] = x_ref[...]
        ext[halo + L:2 * halo + L, :] = jnp.where(ci < nc - 1, hi_ref[...], 0.0)
        out = cb_ref[:, col0:col0 + hw]
        for j in range(C_CONV):
            out = out + cw_ref[j:j + 1, col0:col0 + hw] * ext[halo - pad + j:halo - pad + j + L, :]
        return _silu(out)

    qs = (conv_silu(q_ref, qlo_ref, qhi_ref, 0) * ATTN_SCALE).astype(BF16)
    kc = conv_silu(k_ref, klo_ref, khi_ref, hw)
    kb = kc.astype(BF16)
    vb = v_ref[...].astype(BF16)

    gt = (g_ref[...] + gb_ref[...]).T
    ig = jnp.where(d == 0, gt[0:nh], gt[2 * nh:3 * nh])
    fg = jnp.where(d == 0, gt[nh:2 * nh], gt[3 * nh:4 * nh])
    logf = _log_sigmoid(fg)
    lane = lax.broadcasted_iota(jnp.int32, (nh, L), 1)
    pre = logf
    sft = 1
    while sft < L:
        pre = pre + jnp.where(lane >= sft, pltpu.roll(pre, sft, axis=1), 0.0)
        sft *= 2
    total = pre[:, L - 1:L]
    bcum = jnp.where(d == 0, pre, total - pre + logf)
    a = total - bcum + ig
    m_loc = jnp.max(a, axis=1, keepdims=True)
    w_row = jnp.exp(a - m_loc)
    r_row = ig - bcum
    xt = jnp.concatenate([bcum, w_row, jnp.zeros((L - 2 * nh, L), F32)], axis=0).T

    ii = lax.broadcasted_iota(jnp.int32, (L, L), 0)
    jj = lax.broadcasted_iota(jnp.int32, (L, L), 1)
    causal = jnp.where(d == 0, ii - jj, jj - ii) >= 0
    ones = jnp.ones((L, hd), BF16)

    for h in range(nh):
        cs = slice(h * hd, (h + 1) * hd)
        c_i = xt[:, h:h + 1]
        w_col = xt[:, nh + h:nh + h + 1]
        m0 = mst[h][0:1, 0:1]
        cext = cst[h]
        dmat = jnp.where(causal, c_i + r_row[h:h + 1, :], NEG)
        inter = c_i + m0
        m_i = jnp.maximum(jnp.max(dmat, axis=-1, keepdims=True), inter)
        sw = jnp.exp(dmat - m_i) * _dot_nt(qs[:, cs], kb[:, cs])
        si = jnp.exp(inter - m_i)
        qc0 = _dot(qs[:, cs], cext.astype(BF16))
        num = _dot(sw.astype(BF16), vb[:, cs]) + si * qc0[:, 0:hd]
        den = jnp.sum(sw, axis=-1, keepdims=True) + si * qc0[:, hd:hd + 1]
        h_ref[0, :, cs] = num / jnp.maximum(jnp.abs(den), jnp.exp(-m_i))
        kw = (kc[:, cs] * w_col).astype(BF16)
        vext = jnp.concatenate([vb[:, cs], ones], axis=1)
        c_loc = _dot_tn(kw, vext)
        tot_h = total[h:h + 1, :]
        mloc_h = m_loc[h:h + 1, :]
        m_new = jnp.maximum(tot_h + m0, mloc_h)
        s_prev = jnp.exp(tot_h + m0 - m_new)
        s_loc = jnp.exp(mloc_h - m_new)
        cst[h] = s_prev * cext + s_loc * c_loc
        mst[h] = jnp.broadcast_to(m_new, (8, 128))

    @pl.when(c == nc - 1)
    def _():
        sf_ref[...] = cst[...]
        mf_ref[...] = mst[...]


def _mlstm_call(p_main, gates, gate_b, conv_w, conv_b, s0, m0, bsz):
    t = p_main.shape[0] // bsz
    L = M_CHUNK
    nc = t // L
    nh = C_HEADS
    hw = nh * HEAD_DIM
    halo = 8
    hb = L // halo

    def cidx(d, c):
        return jnp.where(d == 0, c, nc - 1 - c)

    def main(col):
        return pl.BlockSpec((L, hw), lambda b, d, c: (b * nc + cidx(d, c), col))

    def lo(col):
        return pl.BlockSpec((halo, hw), lambda b, d, c: (jnp.maximum((b * nc + cidx(d, c)) * hb - 1, 0), col))

    def hi(col):
        return pl.BlockSpec((halo, hw),
                            lambda b, d, c: (jnp.minimum((b * nc + cidx(d, c) + 1) * hb, bsz * nc * hb - 1), col))

    sspec = pl.BlockSpec((nh, HEAD_DIM, 2 * HEAD_DIM), lambda b, d, c: (b * 2 + d, 0, 0))
    mspec = pl.BlockSpec((nh, 8, 128), lambda b, d, c: (b * 2 + d, 0, 0))
    blocks = (3 * _nbytes((L, hw), F32) + 4 * _nbytes((halo, hw), F32) + _nbytes((L, GATE_W), F32)
              + _nbytes((C_CONV + 1, 2 * hw), F32) + 2 * _nbytes((nh, HEAD_DIM, 2 * HEAD_DIM), F32)
              + 2 * _nbytes((nh, 8, 128), F32) + _nbytes((L, hw), F32))
    scratch = _nbytes((nh, HEAD_DIM, 2 * HEAD_DIM), F32) + _nbytes((L + 2 * halo, hw), F32)
    return pl.pallas_call(
        functools.partial(_mlstm_kernel, nc=nc),
        out_shape=[jax.ShapeDtypeStruct((2, bsz * t, hw), F32),
                   jax.ShapeDtypeStruct(s0.shape, F32), jax.ShapeDtypeStruct(m0.shape, F32)],
        grid=(bsz, 2, nc),
        in_specs=[main(COL_CQ), main(COL_CK), main(COL_CV), lo(COL_CQ), hi(COL_CQ), lo(COL_CK), hi(COL_CK),
                  pl.BlockSpec((L, GATE_W), lambda b, d, c: (b * nc + cidx(d, c), 0)),
                  pl.BlockSpec((1, GATE_W), lambda b, d, c: (0, 0)),
                  pl.BlockSpec((C_CONV, 2 * hw), lambda b, d, c: (0, 0)),
                  pl.BlockSpec((1, 2 * hw), lambda b, d, c: (0, 0)),
                  sspec, mspec],
        out_specs=[pl.BlockSpec((1, L, hw), lambda b, d, c: (d, b * nc + cidx(d, c), 0)), sspec, mspec],
        scratch_shapes=[pltpu.VMEM((nh, HEAD_DIM, 2 * HEAD_DIM), F32), pltpu.VMEM((nh, 8, 128), F32),
                        pltpu.VMEM((L + 2 * halo, hw), F32)],
        compiler_params=_params(("parallel", "parallel", "arbitrary"), blocks, scratch + 8 * 1024 * 1024),
        name="mlstm_scan",
    )(p_main, p_main, p_main, p_main, p_main, p_main, p_main, gates, gate_b, conv_w, conv_b, s0, m0)


def _mlstm_out_kernel(h_ref, o_ref, g_ref, y_ref):
    hd = HEAD_DIM
    for h in range(C_HEADS):
        cs = slice(h * hd, (h + 1) * hd)
        hn = _head_norm(h_ref[0, :, cs] + h_ref[1, :, cs], g_ref[:, cs])
        y_ref[:, cs] = (hn * jax.nn.sigmoid(o_ref[:, cs])).astype(y_ref.dtype)


def _mlstm_out_call(hdir, p_main, norm_g):
    m, hw = hdir.shape[1:]
    tr = _pick(m, (256, 128))
    blocks = 3 * _nbytes((tr, hw), F32) + _nbytes((tr, hw), BF16)
    return pl.pallas_call(
        _mlstm_out_kernel,
        out_shape=jax.ShapeDtypeStruct((m, hw), BF16),
        grid=(m // tr,),
        in_specs=[pl.BlockSpec((2, tr, hw), lambda i: (0, i, 0)),
                  pl.BlockSpec((tr, hw), lambda i: (i, COL_CO)),
                  pl.BlockSpec((1, hw), lambda i: (0, 0))],
        out_specs=pl.BlockSpec((tr, hw), lambda i: (i, 0)),
        compiler_params=_params(("parallel",), blocks, 2 * 1024 * 1024),
        name="mlstm_out",
    )(hdir, p_main, norm_g.reshape(1, hw))


def _rope_lane_tables(n):
    t = jnp.arange(n)
    inv = ROPE_THETA ** (-jnp.arange(ROPE_PAIRS, dtype=F32) / ROPE_PAIRS)
    row = (t // GRID_W).astype(F32)[:, None] * inv
    col = (t % GRID_W).astype(F32)[:, None] * inv
    cos = jnp.concatenate([jnp.cos(row), jnp.cos(row), jnp.cos(col), jnp.cos(col)], axis=1)
    sin = jnp.concatenate([-jnp.sin(row), jnp.sin(row), -jnp.sin(col), jnp.sin(col)], axis=1)
    return cos, sin


def _split_w_in(w_in):
    kv = A_KV_HEADS * HEAD_DIM
    aq_end = BRANCH_W
    akv_end = aq_end + 2 * kv
    co_end = akv_end + 7 * BRANCH_W
    main = jnp.concatenate([w_in[:, :aq_end], w_in[:, akv_end:co_end], w_in[:, aq_end:akv_end]], axis=1)
    gates = jnp.pad(w_in[:, co_end:], ((0, 0), (0, GATE_W - 4 * C_HEADS)))
    return main.astype(BF16), gates.astype(BF16)


def kernel(x, c, ctx, c_ctx, w_mod, b_mod, norm_g, ffn_w1, ffn_w3, ffn_w2, w_in, qk_g, attn_sink, na_relpos,
           mlstm_conv_w, mlstm_conv_b, mlstm_gate_b, mlstm_norm_g, w_gate, b_gate, w_branch, w_out):
    bsz, n, d = x.shape
    lc = ctx.shape[1]
    depth = w_mod.shape[0]
    assert n % GRID_W == 0 and n % M_CHUNK == 0 and lc % M_CHUNK == 0 and n % A_BLOCK == 0

    cs = jnp.zeros((8, d), F32).at[:bsz].set(c).at[bsz].set(c_ctx)
    mods = _mod_call(cs, w_mod, b_mod).reshape(depth, 8, N_MOD, d)
    cos_t, sin_t = _rope_lane_tables(n)

    xl = x.reshape(bsz * n, d)
    xc = ctx.reshape(bsz * lc, d)
    nh = C_HEADS
    s_zero = jnp.zeros((bsz * 2 * nh, HEAD_DIM, 2 * HEAD_DIM), F32)
    m_zero = jnp.zeros((bsz * 2 * nh, 8, 128), F32)

    def ffn(xs, md, k0, g, w1, w3, w2):
        u = _normmod_call(xs, g, md[:, k0], md[:, k0 + 1])
        return _down_call(_up_call(u, w1, w3), w2, xs, md[:, k0 + 2], 0.5)

    for i in range(depth):
        last = i == depth - 1
        ml = mods[i, :bsz]
        mc = mods[i, bsz:bsz + 1]
        w1 = ffn_w1[i].astype(BF16)
        w3 = ffn_w3[i].astype(BF16)
        w2 = ffn_w2[i].astype(BF16)
        w_main, w_gates = _split_w_in(w_in[i])
        wg = w_gate[i].astype(BF16)
        wb = w_branch[i].astype(BF16)
        wo = w_out[i].astype(BF16)
        gate_b = jnp.pad(mlstm_gate_b[i].reshape(1, 4 * nh), ((0, 0), (0, GATE_W - 4 * nh)))
        conv_b = mlstm_conv_b[i].reshape(1, -1)
        bias_tab = _na_bias_table(na_relpos[i])

        xc = ffn(xc, mc, 0, norm_g[i, 0], w1[0], w3[0], w2[0])
        xl = ffn(xl, ml, 0, norm_g[i, 0], w1[0], w3[0], w2[0])

        uc = _normmod_call(xc, norm_g[i, 1], mc[:, 3], mc[:, 4])
        ul = _normmod_call(xl, norm_g[i, 1], ml[:, 3], ml[:, 4])
        pc = _matmul_call(uc, w_main)
        pl_ = _matmul_call(ul, w_main)
        gc = _matmul_call(uc, w_gates)
        gl = _matmul_call(ul, w_gates)

        qa_c, qb_c, kb_c, vb_c, ka_c, va_c = _qkprep_call(pc, qk_g[i], cos_t, sin_t, False, lc)
        qa_l, qb_l, kb_l, vb_l, ka_l, va_l = _qkprep_call(pl_, qk_g[i], cos_t, sin_t, True, n)

        a_l = _attn_a_call(qa_l, ka_l, va_l, ka_c, va_c, attn_sink[i], bsz, True)
        n_l = _attn_b_call(qb_l, kb_l, vb_l, kb_c, vb_c, bias_tab, bsz)
        hc, s_c, m_c = _mlstm_call(pc, gc, gate_b, mlstm_conv_w[i], conv_b, s_zero, m_zero, bsz)
        hl, _, _ = _mlstm_call(pl_, gl, gate_b, mlstm_conv_w[i], conv_b, s_c, m_c, bsz)
        m_l = _mlstm_out_call(hl, pl_, mlstm_norm_g[i])

        yl = _merge_call(ul, a_l, n_l, m_l, wg, b_gate[i], wb)
        xl = _down_call(yl, wo, xl, ml[:, 5], 1.0)
        xl = ffn(xl, ml, 6, norm_g[i, 2], w1[1], w3[1], w2[1])
        if not last:
            a_c = _attn_a_call(qa_c, ka_c, va_c, ka_c, va_c, attn_sink[i], bsz, False)
            n_c = _attn_bx_call(qb_c, kb_c, vb_c, bsz)
            mm_c = _mlstm_out_call(hc, pc, mlstm_norm_g[i])
            yc = _merge_call(uc, a_c, n_c, mm_c, wg, b_gate[i], wb)
            xc = _down_call(yc, wo, xc, mc[:, 5], 1.0)
            xc = ffn(xc, mc, 6, norm_g[i, 2], w1[1], w3[1], w2[1])
    return xl.reshape(bsz, n, d)
```

```python
import functools

import numpy as np
import jax
import jax.numpy as jnp
from jax import lax
from jax.experimental import pallas as pl
from jax.experimental.pallas import tpu as pltpu

F32 = jnp.float32
BF16 = jnp.bfloat16

HEAD_DIM = 128
GRID_W = 64
ROPE_PAIRS = HEAD_DIM // 4
ROPE_THETA = 10000.0
A_HEADS = 8
A_KV_HEADS = 2
A_GROUP = A_HEADS // A_KV_HEADS
A_WINDOW = 128
A_BLOCK = 128
B_HEADS = 8
NA_ROWS = 8
NA_COLS = 16
C_HEADS = 8
C_CONV = 5
BRANCH_W = 8 * HEAD_DIM
N_MOD = 9
EPS = 1e-6
NEG = -1e30
ATTN_SCALE = HEAD_DIM ** -0.5

M_CHUNK = 128
NA_TILE_ROWS = 8
NA_TILE = NA_TILE_ROWS * GRID_W
NA_WIN_ROWS = NA_TILE_ROWS + NA_ROWS

V7X_VMEM_BYTES = 64 * 1024 * 1024
VMEM_CAP = V7X_VMEM_BYTES - 6 * 1024 * 1024

COL_AQ, COL_BQ, COL_BK, COL_BV, COL_CQ, COL_CK, COL_CV, COL_CO = range(8)
MAIN_W = 8 * BRANCH_W + 2 * A_KV_HEADS * HEAD_DIM
GATE_W = 128


def _pick(n, cands):
    for c in cands:
        if n % c == 0:
            return c
    raise ValueError(f"no tile in {cands} divides {n}")


def _params(sem, block_bytes, temp_bytes=0):
    limit = 2 * block_bytes + temp_bytes + 4 * 1024 * 1024
    return pltpu.CompilerParams(dimension_semantics=sem,
                                vmem_limit_bytes=int(min(max(limit, 16 * 1024 * 1024), VMEM_CAP)))


def _nbytes(shape, dtype):
    return int(np.prod(shape)) * jnp.dtype(dtype).itemsize


def _dot(a, b):
    return jnp.dot(a, b, preferred_element_type=F32)


def _dot_nt(a, b):
    return lax.dot_general(a, b, (((1,), (1,)), ((), ())), preferred_element_type=F32)


def _dot_tn(a, b):
    return lax.dot_general(a, b, (((0,), (0,)), ((), ())), preferred_element_type=F32)


def _silu(x):
    return x * jax.nn.sigmoid(x)


def _mod_kernel(c_ref, w_ref, b_ref, o_ref):
    a = _silu(c_ref[...]).astype(BF16)
    o_ref[0] = _dot(a, w_ref[0].astype(BF16)) + b_ref[0]


def _mod_call(cs, w_mod, b_mod):
    depth, d, nd = w_mod.shape
    tn = _pick(nd, (512, 256, 128))
    rows = cs.shape[0]
    blocks = _nbytes((d, tn), F32) + _nbytes((rows, d), F32) + _nbytes((rows, tn), F32)
    return pl.pallas_call(
        _mod_kernel,
        out_shape=jax.ShapeDtypeStruct((depth, rows, nd), F32),
        grid=(depth, nd // tn),
        in_specs=[pl.BlockSpec((rows, d), lambda l, j: (0, 0)),
                  pl.BlockSpec((1, d, tn), lambda l, j: (l, 0, j)),
                  pl.BlockSpec((1, 1, tn), lambda l, j: (l, 0, j))],
        out_specs=pl.BlockSpec((1, rows, tn), lambda l, j: (l, 0, j)),
        compiler_params=_params(("parallel", "parallel"), blocks, _nbytes((d, tn), BF16)),
        name="mod_vectors",
    )(cs, w_mod, b_mod.reshape(depth, 1, nd))


def _normmod_kernel(x_ref, g_ref, shift_ref, scale_ref, o_ref):
    x = x_ref[...]
    ms = jnp.mean(x * x, axis=-1, keepdims=True)
    y = x * lax.rsqrt(ms + EPS) * g_ref[...]
    o_ref[...] = (y * (1.0 + scale_ref[0]) + shift_ref[0]).astype(o_ref.dtype)


def _normmod_call(x, g, shift, scale):
    m, d = x.shape
    groups = shift.shape[0]
    tr = _pick(m // groups, (256, 128, 64, 8))
    per = (m // groups) // tr
    blocks = _nbytes((tr, d), F32) + _nbytes((tr, d), BF16) + 3 * _nbytes((1, d), F32)
    return pl.pallas_call(
        _normmod_kernel,
        out_shape=jax.ShapeDtypeStruct((m, d), BF16),
        grid=(m // tr,),
        in_specs=[pl.BlockSpec((tr, d), lambda i: (i, 0)),
                  pl.BlockSpec((1, d), lambda i: (0, 0)),
                  pl.BlockSpec((1, 1, d), lambda i: (i // per, 0, 0)),
                  pl.BlockSpec((1, 1, d), lambda i: (i // per, 0, 0))],
        out_specs=pl.BlockSpec((tr, d), lambda i: (i, 0)),
        compiler_params=_params(("parallel",), blocks, 2 * _nbytes((tr, d), F32)),
        name="norm_modulate",
    )(x, g.reshape(1, d), shift.reshape(groups, 1, d), scale.reshape(groups, 1, d))


def _up_kernel(u_ref, w1_ref, w3_ref, o_ref):
    u = u_ref[...]
    h1 = _dot(u, w1_ref[...].astype(BF16))
    h3 = _dot(u, w3_ref[...].astype(BF16))
    o_ref[...] = (_silu(h1) * h3).astype(o_ref.dtype)


def _up_call(u, w1, w3, layer, sub):
    m, d = u.shape
    f = w1.shape[-1]
    tm = _pick(m, (1024, 512, 256))
    tf = _pick(f, (256, 128))
    wspec = pl.BlockSpec((None, None, d, tf), lambda i, j: (layer, sub, 0, j))
    blocks = _nbytes((tm, d), BF16) + 2 * _nbytes((d, tf), F32) + _nbytes((tm, tf), BF16)
    return pl.pallas_call(
        _up_kernel,
        out_shape=jax.ShapeDtypeStruct((m, f), BF16),
        grid=(m // tm, f // tf),
        in_specs=[pl.BlockSpec((tm, d), lambda i, j: (i, 0)), wspec, wspec],
        out_specs=pl.BlockSpec((tm, tf), lambda i, j: (i, j)),
        compiler_params=_params(("parallel", "parallel"), blocks,
                                2 * _nbytes((d, tf), BF16) + 4 * _nbytes((tm, tf), F32)),
        name="ffn_up",
    )(u, w1, w3)


def _down_kernel(a_ref, w_ref, x_ref, gate_ref, o_ref, *, coef):
    acc = _dot(a_ref[...], w_ref[...].astype(BF16))
    o_ref[...] = x_ref[...] + (coef * gate_ref[0]) * acc


def _down_call(a, w, x, gate, coef, lead=()):
    m, k = a.shape
    d = w.shape[-1]
    groups = gate.shape[0]
    tm = _pick(m // groups, (1024, 512, 256))
    tn = _pick(d, (512, 256, 128) if k <= 4096 else (256, 128))
    per = (m // groups) // tm
    blocks = (_nbytes((tm, k), BF16) + _nbytes((k, tn), w.dtype) + 2 * _nbytes((tm, tn), F32)
              + _nbytes((1, tn), F32))
    temps = 2 * _nbytes((tm, tn), F32) + (_nbytes((k, tn), BF16) if w.dtype != BF16 else 0)
    wspec = pl.BlockSpec((None,) * len(lead) + (k, tn), lambda i, j: tuple(lead) + (0, j))
    return pl.pallas_call(
        functools.partial(_down_kernel, coef=coef),
        out_shape=jax.ShapeDtypeStruct((m, d), F32),
        grid=(m // tm, d // tn),
        in_specs=[pl.BlockSpec((tm, k), lambda i, j: (i, 0)),
                  wspec,
                  pl.BlockSpec((tm, tn), lambda i, j: (i, j)),
                  pl.BlockSpec((1, 1, tn), lambda i, j: (i // per, 0, j))],
        out_specs=pl.BlockSpec((tm, tn), lambda i, j: (i, j)),
        compiler_params=_params(("parallel", "parallel"), blocks, temps),
        name="proj_residual",
    )(a, w, x, gate.reshape(groups, 1, d))


def _matmul_kernel(a_ref, w_ref, o_ref):
    o_ref[...] = _dot(a_ref[...], w_ref[...]).astype(o_ref.dtype)


def _matmul_call(a, w, out_dtype=F32):
    m, k = a.shape
    n = w.shape[1]
    tm = _pick(m, (1024, 512, 256))
    tn = _pick(n, (512, 256, 128))
    blocks = _nbytes((tm, k), BF16) + _nbytes((k, tn), BF16) + _nbytes((tm, tn), out_dtype)
    return pl.pallas_call(
        _matmul_kernel,
        out_shape=jax.ShapeDtypeStruct((m, n), out_dtype),
        grid=(m // tm, n // tn),
        in_specs=[pl.BlockSpec((tm, k), lambda i, j: (i, 0)),
                  pl.BlockSpec((k, tn), lambda i, j: (0, j))],
        out_specs=pl.BlockSpec((tm, tn), lambda i, j: (i, j)),
        compiler_params=_params(("parallel", "parallel"), blocks, _nbytes((tm, tn), F32)),
        name="in_proj",
    )(a, w)


def _merge_kernel(u_ref, oa_ref, ob_ref, oc_ref, wg_ref, bg_ref, wb_ref, y_ref):
    u = u_ref[...]
    acc = None
    for j, br_ref in enumerate((oa_ref, ob_ref, oc_ref)):
        gate = jax.nn.sigmoid(_dot(u, wg_ref[j]) + bg_ref[j])
        term = gate * _dot(br_ref[...], wb_ref[j])
        acc = term if acc is None else acc + term
    y_ref[...] = acc.astype(y_ref.dtype)


def _merge_call(u, oa, ob, oc, wg, bg, wb):
    m, d = u.shape
    bw = oa.shape[1]
    tm = _pick(m, (1024, 512, 256))
    tn = _pick(d, (256, 128))
    blocks = (_nbytes((tm, d), BF16) + 3 * _nbytes((tm, bw), BF16) + 3 * _nbytes((d, tn), BF16)
              + 3 * _nbytes((bw, tn), BF16) + _nbytes((tm, tn), BF16))
    return pl.pallas_call(
        _merge_kernel,
        out_shape=jax.ShapeDtypeStruct((m, d), BF16),
        grid=(m // tm, d // tn),
        in_specs=[pl.BlockSpec((tm, d), lambda i, j: (i, 0)),
                  pl.BlockSpec((tm, bw), lambda i, j: (i, 0)),
                  pl.BlockSpec((tm, bw), lambda i, j: (i, 0)),
                  pl.BlockSpec((tm, bw), lambda i, j: (i, 0)),
                  pl.BlockSpec((3, d, tn), lambda i, j: (0, 0, j)),
                  pl.BlockSpec((3, 1, tn), lambda i, j: (0, 0, j)),
                  pl.BlockSpec((3, bw, tn), lambda i, j: (0, 0, j))],
        out_specs=pl.BlockSpec((tm, tn), lambda i, j: (i, j)),
        compiler_params=_params(("parallel", "parallel"), blocks, 10 * _nbytes((tm, tn), F32)),
        name="gated_merge",
    )(u, oa, ob, oc, wg, bg.reshape(3, 1, d), wb)


def _head_norm(x, g):
    ms = jnp.mean(x * x, axis=-1, keepdims=True)
    return x * lax.rsqrt(ms + EPS) * g


def _rope(y, cos, sin_signed):
    lane = lax.broadcasted_iota(jnp.int32, y.shape, 1)
    partner = jnp.where((lane % 64) < ROPE_PAIRS, pltpu.roll(y, HEAD_DIM - ROPE_PAIRS, axis=1),
                        pltpu.roll(y, ROPE_PAIRS, axis=1))
    return y * cos + partner * sin_signed


def _qkprep_kernel(p4_ref, pa_ref, g_ref, cos_ref, sin_ref,
                   qa_ref, qb_ref, kb_ref, vb_ref, ka_ref, va_ref, *, rope):
    hd = HEAD_DIM
    if rope:
        cos = cos_ref[...]
        sin = sin_ref[...]
    for h in range(A_HEADS):
        y = _head_norm(p4_ref[:, h * hd:(h + 1) * hd], g_ref[0:1, :])
        if rope:
            y = _rope(y, cos, sin)
        qa_ref[:, h * hd:(h + 1) * hd] = y.astype(BF16)
    for h in range(B_HEADS):
        c0 = BRANCH_W + h * hd
        qb_ref[:, h * hd:(h + 1) * hd] = _head_norm(p4_ref[:, c0:c0 + hd], g_ref[2:3, :]).astype(BF16)
        c0 = 2 * BRANCH_W + h * hd
        kb_ref[:, h * hd:(h + 1) * hd] = _head_norm(p4_ref[:, c0:c0 + hd], g_ref[3:4, :]).astype(BF16)
    vb_ref[...] = p4_ref[:, 3 * BRANCH_W:4 * BRANCH_W].astype(BF16)
    for h in range(A_KV_HEADS):
        y = _head_norm(pa_ref[:, h * hd:(h + 1) * hd], g_ref[1:2, :])
        if rope:
            y = _rope(y, cos, sin)
        ka_ref[:, h * hd:(h + 1) * hd] = y.astype(BF16)
    kvw = A_KV_HEADS * hd
    va_ref[...] = pa_ref[:, kvw:2 * kvw].astype(BF16)


def _qkprep_call(p_main, qk_g, cos_t, sin_t, rope, seq):
    m = p_main.shape[0]
    tr = _pick(seq, (256, 128))
    per = seq // tr
    kvw = A_KV_HEADS * HEAD_DIM
    w4 = 4 * BRANCH_W
    blocks = (_nbytes((tr, w4), F32) + _nbytes((tr, 2 * kvw), F32) + 2 * _nbytes((tr, HEAD_DIM), F32)
              + _nbytes((tr, w4), BF16) + _nbytes((tr, 2 * kvw), BF16))
    outs = [jax.ShapeDtypeStruct((m, BRANCH_W), BF16)] * 4 + [jax.ShapeDtypeStruct((m, kvw), BF16)] * 2
    return pl.pallas_call(
        functools.partial(_qkprep_kernel, rope=rope),
        out_shape=outs,
        grid=(m // tr,),
        in_specs=[pl.BlockSpec((tr, w4), lambda i: (i, 0)),
                  pl.BlockSpec((tr, 2 * kvw), lambda i: (i, (8 * BRANCH_W) // (2 * kvw))),
                  pl.BlockSpec((4, HEAD_DIM), lambda i: (0, 0)),
                  pl.BlockSpec((tr, HEAD_DIM), lambda i: (i % per, 0)),
                  pl.BlockSpec((tr, HEAD_DIM), lambda i: (i % per, 0))],
        out_specs=[pl.BlockSpec((tr, BRANCH_W), lambda i: (i, 0))] * 4
                  + [pl.BlockSpec((tr, kvw), lambda i: (i, 0))] * 2,
        compiler_params=_params(("parallel",), blocks, 4 * _nbytes((tr, HEAD_DIM), F32)),
        name="qk_prep",
    )(p_main, p_main, qk_g, cos_t, sin_t)


def _attn_a_kernel(sink_ref, q_ref, *rest, n_tok, lc, window):
    if window:
        kp_ref, kc_ref, kn_ref, vp_ref, vc_ref, vn_ref, kx_ref, vx_ref, o_ref = rest
        kall = jnp.concatenate([kx_ref[...], kp_ref[...], kc_ref[...], kn_ref[...]], axis=0)
        vall = jnp.concatenate([vx_ref[...], vp_ref[...], vc_ref[...], vn_ref[...]], axis=0)
    else:
        kx_ref, vx_ref, o_ref = rest
        kall = kx_ref[...]
        vall = vx_ref[...]
    i = pl.program_id(1)
    hk = pl.program_id(2)
    hd = HEAD_DIM
    blk = A_BLOCK
    q4 = jnp.concatenate([q_ref[:, g * hd:(g + 1) * hd] for g in range(A_GROUP)], axis=0)
    s = _dot_nt(q4, kall) * ATTN_SCALE
    if window:
        row = lax.broadcasted_iota(jnp.int32, s.shape, 0) % blk
        col = lax.broadcasted_iota(jnp.int32, s.shape, 1)
        qpos = i * blk + row
        kpos = (i - 1) * blk + (col - lc)
        band = (jnp.abs(qpos - kpos) <= A_WINDOW) & (kpos >= 0) & (kpos < n_tok)
        s = jnp.where((col < lc) | band, s, NEG)
    snk = jnp.concatenate([jnp.full((blk, 1), sink_ref[hk * A_GROUP + g], F32) for g in range(A_GROUP)],
                          axis=0)
    mx = jnp.maximum(jnp.max(s, axis=-1, keepdims=True), snk)
    e = jnp.exp(s - mx)
    den = jnp.sum(e, axis=-1, keepdims=True) + jnp.exp(snk - mx)
    p = (e * (1.0 / den)).astype(BF16)
    o = _dot(p, vall)
    for g in range(A_GROUP):
        o_ref[:, g * hd:(g + 1) * hd] = o[g * blk:(g + 1) * blk].astype(o_ref.dtype)


def _attn_a_call(qa, ka, va, kx, vx, sink, bsz, window):
    n = qa.shape[0] // bsz
    lc = kx.shape[0] // bsz
    blk = A_BLOCK
    nb = n // blk
    hd = HEAD_DIM
    gw = A_GROUP * hd
    qspec = pl.BlockSpec((blk, gw), lambda b, i, h: (b * nb + i, h))
    xspec = pl.BlockSpec((lc, hd), lambda b, i, h: (b, h))
    sspec = pl.BlockSpec(memory_space=pltpu.SMEM)
    if window:
        prev = pl.BlockSpec((blk, hd), lambda b, i, h: (b * nb + jnp.maximum(i - 1, 0), h))
        cur = pl.BlockSpec((blk, hd), lambda b, i, h: (b * nb + i, h))
        nxt = pl.BlockSpec((blk, hd), lambda b, i, h: (b * nb + jnp.minimum(i + 1, nb - 1), h))
        in_specs = [sspec, qspec, prev, cur, nxt, prev, cur, nxt, xspec, xspec]
        args = (sink, qa, ka, ka, ka, va, va, va, kx, vx)
        keys = lc + 3 * blk
    else:
        in_specs = [sspec, qspec, xspec, xspec]
        args = (sink, qa, kx, vx)
        keys = lc
    blocks = 2 * _nbytes((blk, gw), BF16) + 2 * _nbytes((keys, hd), BF16)
    return pl.pallas_call(
        functools.partial(_attn_a_kernel, n_tok=n, lc=lc, window=window),
        out_shape=jax.ShapeDtypeStruct(qa.shape, BF16),
        grid=(bsz, nb, A_KV_HEADS),
        in_specs=in_specs,
        out_specs=qspec,
        compiler_params=_params(("parallel", "parallel", "parallel"), blocks,
                                6 * _nbytes((A_GROUP * blk, keys), F32)),
        name="windowed_gqa" if window else "context_gqa",
    )(*args)


def _softmax2(s1, s2):
    mx = jnp.maximum(jnp.max(s1, axis=-1, keepdims=True), jnp.max(s2, axis=-1, keepdims=True))
    e1 = jnp.exp(s1 - mx)
    e2 = jnp.exp(s2 - mx)
    inv = 1.0 / (jnp.sum(e1, axis=-1, keepdims=True) + jnp.sum(e2, axis=-1, keepdims=True))
    return (e1 * inv).astype(BF16), (e2 * inv).astype(BF16)


def _attn_b_kernel(q_ref, k0_ref, k1_ref, k2_ref, k3_ref, v0_ref, v1_ref, v2_ref, v3_ref, kx_ref, vx_ref,
                   bias_ref, o_ref):
    kwin = jnp.concatenate([k0_ref[...], k1_ref[...], k2_ref[...], k3_ref[...]], axis=0)
    vwin = jnp.concatenate([v0_ref[...], v1_ref[...], v2_ref[...], v3_ref[...]], axis=0)
    q = q_ref[...]
    s_nb = _dot_nt(q, kwin) * ATTN_SCALE + bias_ref[0, 0]
    s_cx = _dot_nt(q, kx_ref[...]) * ATTN_SCALE
    p_nb, p_cx = _softmax2(s_nb, s_cx)
    o_ref[...] = (_dot(p_nb, vwin) + _dot(p_cx, vx_ref[...])).astype(o_ref.dtype)


def _attn_bx_kernel(q_ref, kx_ref, vx_ref, o_ref):
    hd = HEAD_DIM
    for h in range(B_HEADS):
        cs = slice(h * hd, (h + 1) * hd)
        s = _dot_nt(q_ref[:, cs], kx_ref[:, cs]) * ATTN_SCALE
        mx = jnp.max(s, axis=-1, keepdims=True)
        e = jnp.exp(s - mx)
        p = (e * (1.0 / jnp.sum(e, axis=-1, keepdims=True))).astype(BF16)
        o_ref[:, cs] = _dot(p, vx_ref[:, cs]).astype(o_ref.dtype)


def _na_bias_kernel(rel_ref, o_ref):
    h = pl.program_id(0)
    pos = pl.program_id(1)
    nrel_r = 2 * NA_ROWS - 1
    nrel_c = 2 * NA_COLS - 1
    qi = lax.broadcasted_iota(jnp.int32, (GRID_W, GRID_W), 0)
    ki = lax.broadcasted_iota(jnp.int32, (GRID_W, GRID_W), 1)
    dc = jnp.clip(ki - qi, -(NA_COLS - 1), NA_COLS - 1) + NA_COLS - 1
    start = jnp.clip(qi - NA_COLS // 2, 0, GRID_W - NA_COLS)
    col_in = (ki >= start) & (ki < start + NA_COLS)
    planes = []
    for dr in range(nrel_r):
        acc = jnp.zeros((GRID_W, GRID_W), F32)
        for dcv in range(nrel_c):
            acc = jnp.where(dc == dcv, rel_ref[(h * nrel_r + dr) * nrel_c + dcv], acc)
        planes.append(jnp.where(col_in, acc, NEG))
    masked = jnp.full((GRID_W, GRID_W), NEG, F32)
    half = NA_ROWS // 2
    first_key = {0: lambda j: max(j, half), 1: lambda j: j, 2: lambda j: min(j, half)}
    for p, lo_of in first_key.items():
        @pl.when(pos == p)
        def _(lo_of=lo_of):
            for j in range(NA_TILE_ROWS):
                lo = lo_of(j)
                for jj in range(NA_WIN_ROWS):
                    blk = planes[jj - j + half - 1] if lo <= jj < lo + NA_ROWS else masked
                    o_ref[0, 0, j * GRID_W:(j + 1) * GRID_W, jj * GRID_W:(jj + 1) * GRID_W] = blk


def _na_bias_table(relpos):
    nh = relpos.shape[0]
    shape = (nh, 3, NA_TILE, NA_WIN_ROWS * GRID_W)
    return pl.pallas_call(
        _na_bias_kernel,
        out_shape=jax.ShapeDtypeStruct(shape, F32),
        grid=(nh, 3),
        in_specs=[pl.BlockSpec(memory_space=pltpu.SMEM)],
        out_specs=pl.BlockSpec((1, 1) + shape[2:], lambda h, p: (h, p, 0, 0)),
        compiler_params=_params(("parallel", "parallel"), _nbytes(shape[2:], F32)),
        name="na_bias_table",
    )(relpos.reshape(-1))


def _attn_b_call(qb, kb, vb, kx, vx, bias, bsz):
    n = qb.shape[0] // bsz
    lc = kx.shape[0] // bsz
    rows = n // GRID_W
    assert rows % NA_TILE_ROWS == 0 and rows >= 2 * NA_TILE_ROWS
    nt = rows // NA_TILE_ROWS
    hd = HEAD_DIM
    sub = NA_TILE // 2
    qspec = pl.BlockSpec((NA_TILE, hd), lambda b, h, t: (b * nt + t, h))

    def kspec(s):
        return pl.BlockSpec((sub, hd),
                            lambda b, h, t: (jnp.clip(2 * t + s, 0, 2 * nt - 1) + 2 * b * nt, h))

    kspecs = [kspec(s) for s in (-1, 0, 1, 2)]
    xspec = pl.BlockSpec((lc, hd), lambda b, h, t: (b, h))
    bspec = pl.BlockSpec((1, 1) + bias.shape[2:],
                         lambda b, h, t: (h, jnp.where(t == 0, 0, jnp.where(t == nt - 1, 2, 1)), 0, 0))
    blocks = (2 * _nbytes((NA_TILE, hd), BF16) + 8 * _nbytes((sub, hd), BF16) + 2 * _nbytes((lc, hd), BF16)
              + _nbytes(bias.shape[2:], F32))
    return pl.pallas_call(
        _attn_b_kernel,
        out_shape=jax.ShapeDtypeStruct(qb.shape, BF16),
        grid=(bsz, B_HEADS, nt),
        in_specs=[qspec] + kspecs + kspecs + [xspec, xspec, bspec],
        out_specs=qspec,
        compiler_params=_params(("parallel", "parallel", "parallel"), blocks,
                                5 * _nbytes((NA_TILE, NA_WIN_ROWS * GRID_W + lc), F32)),
        name="neighbourhood_attn",
    )(qb, kb, kb, kb, kb, vb, vb, vb, vb, kx, vx, bias)


def _attn_bx_call(qx, kx, vx, bsz):
    lc = qx.shape[0] // bsz
    w = qx.shape[1]
    spec = pl.BlockSpec((lc, w), lambda b: (b, 0))
    return pl.pallas_call(
        _attn_bx_kernel,
        out_shape=jax.ShapeDtypeStruct(qx.shape, BF16),
        grid=(bsz,),
        in_specs=[spec, spec, spec],
        out_specs=spec,
        compiler_params=_params(("parallel",), 4 * _nbytes((lc, w), BF16), 4 * 1024 * 1024),
        name="context_full_attn",
    )(qx, kx, vx)


def _log_sigmoid(x):
    return -(jnp.maximum(-x, 0.0) + jnp.log1p(jnp.exp(-jnp.abs(x))))


CONV_HALO = 8


def _cprep_kernel(q_ref, k_ref, v_ref, qlo_ref, qhi_ref, klo_ref, khi_ref, cw_ref, cb_ref,
                  qs_ref, kb_ref, vb_ref, ext, *, per):
    pos = pl.program_id(0) % per
    tr = q_ref.shape[0]
    hw = q_ref.shape[1]
    halo = CONV_HALO
    pad = C_CONV // 2

    def conv_silu(x_ref, lo_ref, hi_ref, col0):
        ext[0:halo, :] = jnp.where(pos > 0, lo_ref[...], 0.0)
        ext[halo:halo + tr, :] = x_ref[...]
        ext[halo + tr:2 * halo + tr, :] = jnp.where(pos < per - 1, hi_ref[...], 0.0)
        out = cb_ref[:, col0:col0 + hw]
        for j in range(C_CONV):
            out = out + cw_ref[j:j + 1, col0:col0 + hw] * ext[halo - pad + j:halo - pad + j + tr, :]
        return _silu(out)

    qs_ref[...] = (conv_silu(q_ref, qlo_ref, qhi_ref, 0) * ATTN_SCALE).astype(BF16)
    kb_ref[...] = conv_silu(k_ref, klo_ref, khi_ref, hw).astype(BF16)
    vb_ref[...] = v_ref[...].astype(BF16)


def _cprep_call(p_main, conv_w, conv_b, seq):
    m = p_main.shape[0]
    hw = C_HEADS * HEAD_DIM
    tr = _pick(seq, (256, 128))
    per = seq // tr
    halo = CONV_HALO
    hb = tr // halo
    nblk = m // halo

    def main(col):
        return pl.BlockSpec((tr, hw), lambda i: (i, col))

    def lo(col):
        return pl.BlockSpec((halo, hw), lambda i: (jnp.maximum(i * hb - 1, 0), col))

    def hi(col):
        return pl.BlockSpec((halo, hw), lambda i: (jnp.minimum((i + 1) * hb, nblk - 1), col))

    blocks = (3 * _nbytes((tr, hw), F32) + 4 * _nbytes((halo, hw), F32) + _nbytes((C_CONV + 1, 2 * hw), F32)
              + 3 * _nbytes((tr, hw), BF16))
    scratch = _nbytes((tr + 2 * halo, hw), F32)
    out = jax.ShapeDtypeStruct((m, hw), BF16)
    ospec = pl.BlockSpec((tr, hw), lambda i: (i, 0))
    return pl.pallas_call(
        functools.partial(_cprep_kernel, per=per),
        out_shape=[out, out, out],
        grid=(m // tr,),
        in_specs=[main(COL_CQ), main(COL_CK), main(COL_CV), lo(COL_CQ), hi(COL_CQ), lo(COL_CK), hi(COL_CK),
                  pl.BlockSpec((C_CONV, 2 * hw), lambda i: (0, 0)),
                  pl.BlockSpec((1, 2 * hw), lambda i: (0, 0))],
        out_specs=[ospec, ospec, ospec],
        scratch_shapes=[pltpu.VMEM((tr + 2 * halo, hw), F32)],
        compiler_params=_params(("parallel",), blocks, scratch + 4 * _nbytes((tr, hw), F32)),
        name="mlstm_prep",
    )(p_main, p_main, p_main, p_main, p_main, p_main, p_main, conv_w, conv_b)


def _mlstm_kernel(q_ref, k_ref, v_ref, g_ref, gb_ref, s0_ref, m0_ref, h_ref, sf_ref, mf_ref, cst, mst, *, nc):
    d = pl.program_id(1)
    c = pl.program_id(2)
    L = M_CHUNK
    hd = HEAD_DIM
    nh = C_HEADS

    @pl.when(c == 0)
    def _():
        cst[...] = s0_ref[...]
        mst[...] = m0_ref[...]

    gt = (g_ref[...] + gb_ref[...]).T
    ig = jnp.where(d == 0, gt[0:nh], gt[2 * nh:3 * nh])
    fg = jnp.where(d == 0, gt[nh:2 * nh], gt[3 * nh:4 * nh])
    logf = _log_sigmoid(fg)
    lane = lax.broadcasted_iota(jnp.int32, (nh, L), 1)
    pre = logf
    sft = 1
    while sft < L:
        pre = pre + jnp.where(lane >= sft, pltpu.roll(pre, sft, axis=1), 0.0)
        sft *= 2
    total = pre[:, L - 1:L]
    bcum = jnp.where(d == 0, pre, total - pre + logf)
    a = total - bcum + ig
    m_loc = jnp.max(a, axis=1, keepdims=True)
    w_row = jnp.exp(a - m_loc)
    r_row = ig - bcum
    xt = jnp.concatenate([bcum, w_row, jnp.zeros((L - 2 * nh, L), F32)], axis=0).T

    ii = lax.broadcasted_iota(jnp.int32, (L, L), 0)
    jj = lax.broadcasted_iota(jnp.int32, (L, L), 1)
    causal = jnp.where(d == 0, ii - jj, jj - ii) >= 0
    ones = jnp.ones((L, hd), BF16)

    heads = range(nh)
    cols = [slice(h * hd, (h + 1) * hd) for h in heads]
    qs = [q_ref[:, cs] for cs in cols]
    kb = [k_ref[:, cs] for cs in cols]
    vb = [v_ref[:, cs] for cs in cols]
    cext = [cst[h] for h in heads]
    m0 = [mst[h][0:1, 0:1] for h in heads]
    qk = [_dot_nt(qs[h], kb[h]) for h in heads]
    qc0 = [_dot(qs[h], cext[h].astype(BF16)) for h in heads]
    c_i = [xt[:, h:h + 1] for h in heads]
    dmat = [jnp.where(causal, c_i[h] + r_row[h:h + 1, :], NEG) for h in heads]
    inter = [c_i[h] + m0[h] for h in heads]
    m_i = [jnp.maximum(jnp.max(dmat[h], axis=-1, keepdims=True), inter[h]) for h in heads]
    sw = [jnp.exp(dmat[h] - m_i[h]) * qk[h] for h in heads]
    si = [jnp.exp(inter[h] - m_i[h]) for h in heads]
    num = [_dot(sw[h].astype(BF16), vb[h]) + si[h] * qc0[h][:, 0:hd] for h in heads]
    den = [jnp.sum(sw[h], axis=-1, keepdims=True) + si[h] * qc0[h][:, hd:hd + 1] for h in heads]
    for h in heads:
        h_ref[0, :, cols[h]] = num[h] / jnp.maximum(jnp.abs(den[h]), jnp.exp(-m_i[h]))
    kw = [(kb[h].astype(F32) * xt[:, nh + h:nh + h + 1]).astype(BF16) for h in heads]
    c_loc = [_dot_tn(kw[h], jnp.concatenate([vb[h], ones], axis=1)) for h in heads]
    for h in heads:
        tot_h = total[h:h + 1, :]
        mloc_h = m_loc[h:h + 1, :]
        m_new = jnp.maximum(tot_h + m0[h], mloc_h)
        s_prev = jnp.exp(tot_h + m0[h] - m_new)
        s_loc = jnp.exp(mloc_h - m_new)
        cst[h] = s_prev * cext[h] + s_loc * c_loc[h]
        mst[h] = jnp.broadcast_to(m_new, (8, 128))

    @pl.when(c == nc - 1)
    def _():
        sf_ref[...] = cst[...]
        mf_ref[...] = mst[...]


def _mlstm_call(qs, kb, vb, gates, gate_b, s0, m0, bsz):
    t = qs.shape[0] // bsz
    L = M_CHUNK
    nc = t // L
    nh = C_HEADS
    hw = nh * HEAD_DIM

    def cidx(d, c):
        return jnp.where(d == 0, c, nc - 1 - c)

    main = pl.BlockSpec((L, hw), lambda b, d, c: (b * nc + cidx(d, c), 0))
    sspec = pl.BlockSpec((nh, HEAD_DIM, 2 * HEAD_DIM), lambda b, d, c: (b * 2 + d, 0, 0))
    mspec = pl.BlockSpec((nh, 8, 128), lambda b, d, c: (b * 2 + d, 0, 0))
    blocks = (3 * _nbytes((L, hw), BF16) + _nbytes((L, GATE_W), F32)
              + 2 * _nbytes((nh, HEAD_DIM, 2 * HEAD_DIM), F32)
              + 2 * _nbytes((nh, 8, 128), F32) + _nbytes((L, hw), F32))
    scratch = _nbytes((nh, HEAD_DIM, 2 * HEAD_DIM), F32)
    return pl.pallas_call(
        functools.partial(_mlstm_kernel, nc=nc),
        out_shape=[jax.ShapeDtypeStruct((2, bsz * t, hw), F32),
                   jax.ShapeDtypeStruct(s0.shape, F32), jax.ShapeDtypeStruct(m0.shape, F32)],
        grid=(bsz, 2, nc),
        in_specs=[main, main, main,
                  pl.BlockSpec((L, GATE_W), lambda b, d, c: (b * nc + cidx(d, c), 0)),
                  pl.BlockSpec((1, GATE_W), lambda b, d, c: (0, 0)),
                  sspec, mspec],
        out_specs=[pl.BlockSpec((1, L, hw), lambda b, d, c: (d, b * nc + cidx(d, c), 0)), sspec, mspec],
        scratch_shapes=[pltpu.VMEM((nh, HEAD_DIM, 2 * HEAD_DIM), F32), pltpu.VMEM((nh, 8, 128), F32)],
        compiler_params=_params(("parallel", "parallel", "arbitrary"), blocks, scratch + 12 * 1024 * 1024),
        name="mlstm_scan",
    )(qs, kb, vb, gates, gate_b, s0, m0)


def _mlstm_out_kernel(h_ref, o_ref, g_ref, y_ref):
    hd = HEAD_DIM
    for h in range(C_HEADS):
        cs = slice(h * hd, (h + 1) * hd)
        hn = _head_norm(h_ref[0, :, cs] + h_ref[1, :, cs], g_ref[:, cs])
        y_ref[:, cs] = (hn * jax.nn.sigmoid(o_ref[:, cs])).astype(y_ref.dtype)


def _mlstm_out_call(hdir, p_main, norm_g):
    m, hw = hdir.shape[1:]
    tr = _pick(m, (256, 128))
    blocks = 3 * _nbytes((tr, hw), F32) + _nbytes((tr, hw), BF16)
    return pl.pallas_call(
        _mlstm_out_kernel,
        out_shape=jax.ShapeDtypeStruct((m, hw), BF16),
        grid=(m // tr,),
        in_specs=[pl.BlockSpec((2, tr, hw), lambda i: (0, i, 0)),
                  pl.BlockSpec((tr, hw), lambda i: (i, COL_CO)),
                  pl.BlockSpec((1, hw), lambda i: (0, 0))],
        out_specs=pl.BlockSpec((tr, hw), lambda i: (i, 0)),
        compiler_params=_params(("parallel",), blocks, 2 * 1024 * 1024),
        name="mlstm_out",
    )(hdir, p_main, norm_g.reshape(1, hw))


def _rope_lane_tables(n):
    t = jnp.arange(n)
    inv = ROPE_THETA ** (-jnp.arange(ROPE_PAIRS, dtype=F32) / ROPE_PAIRS)
    row = (t // GRID_W).astype(F32)[:, None] * inv
    col = (t % GRID_W).astype(F32)[:, None] * inv
    cos = jnp.concatenate([jnp.cos(row), jnp.cos(row), jnp.cos(col), jnp.cos(col)], axis=1)
    sin = jnp.concatenate([-jnp.sin(row), jnp.sin(row), -jnp.sin(col), jnp.sin(col)], axis=1)
    return cos, sin


def _split_w_in(w_in):
    kv = A_KV_HEADS * HEAD_DIM
    aq_end = BRANCH_W
    akv_end = aq_end + 2 * kv
    co_end = akv_end + 7 * BRANCH_W
    main = jnp.concatenate([w_in[:, :aq_end], w_in[:, akv_end:co_end], w_in[:, aq_end:akv_end]], axis=1)
    gates = jnp.pad(w_in[:, co_end:], ((0, 0), (0, GATE_W - 4 * C_HEADS)))
    return main.astype(BF16), gates.astype(BF16)


def kernel(x, c, ctx, c_ctx, w_mod, b_mod, norm_g, ffn_w1, ffn_w3, ffn_w2, w_in, qk_g, attn_sink, na_relpos,
           mlstm_conv_w, mlstm_conv_b, mlstm_gate_b, mlstm_norm_g, w_gate, b_gate, w_branch, w_out):
    bsz, n, d = x.shape
    lc = ctx.shape[1]
    depth = w_mod.shape[0]
    assert n % GRID_W == 0 and n % M_CHUNK == 0 and lc % M_CHUNK == 0 and n % A_BLOCK == 0

    cs = jnp.zeros((8, d), F32).at[:bsz].set(c).at[bsz].set(c_ctx)
    mods = _mod_call(cs, w_mod, b_mod).reshape(depth, 8, N_MOD, d)
    cos_t, sin_t = _rope_lane_tables(n)

    xl = x.reshape(bsz * n, d)
    xc = ctx.reshape(bsz * lc, d)
    nh = C_HEADS
    s_zero = jnp.zeros((bsz * 2 * nh, HEAD_DIM, 2 * HEAD_DIM), F32)
    m_zero = jnp.zeros((bsz * 2 * nh, 8, 128), F32)

    def ffn(xs, md, k0, g, layer, sub):
        u = _normmod_call(xs, g, md[:, k0], md[:, k0 + 1])
        h = _up_call(u, ffn_w1, ffn_w3, layer, sub)
        return _down_call(h, ffn_w2, xs, md[:, k0 + 2], 0.5, lead=(layer, sub))

    for i in range(depth):
        last = i == depth - 1
        ml = mods[i, :bsz]
        mc = mods[i, bsz:bsz + 1]
        w_main, w_gates = _split_w_in(w_in[i])
        wg = w_gate[i].astype(BF16)
        wb = w_branch[i].astype(BF16)
        wo = w_out[i].astype(BF16)
        gate_b = jnp.pad(mlstm_gate_b[i].reshape(1, 4 * nh), ((0, 0), (0, GATE_W - 4 * nh)))
        conv_b = mlstm_conv_b[i].reshape(1, -1)
        bias_tab = _na_bias_table(na_relpos[i])

        xc = ffn(xc, mc, 0, norm_g[i, 0], i, 0)
        xl = ffn(xl, ml, 0, norm_g[i, 0], i, 0)

        uc = _normmod_call(xc, norm_g[i, 1], mc[:, 3], mc[:, 4])
        ul = _normmod_call(xl, norm_g[i, 1], ml[:, 3], ml[:, 4])
        pc = _matmul_call(uc, w_main)
        pl_ = _matmul_call(ul, w_main)
        gc = _matmul_call(uc, w_gates)
        gl = _matmul_call(ul, w_gates)

        qa_c, qb_c, kb_c, vb_c, ka_c, va_c = _qkprep_call(pc, qk_g[i], cos_t, sin_t, False, lc)
        qa_l, qb_l, kb_l, vb_l, ka_l, va_l = _qkprep_call(pl_, qk_g[i], cos_t, sin_t, True, n)

        a_l = _attn_a_call(qa_l, ka_l, va_l, ka_c, va_c, attn_sink[i], bsz, True)
        n_l = _attn_b_call(qb_l, kb_l, vb_l, kb_c, vb_c, bias_tab, bsz)
        qc_c, kc_c, vc_c = _cprep_call(pc, mlstm_conv_w[i], conv_b, lc)
        qc_l, kc_l, vc_l = _cprep_call(pl_, mlstm_conv_w[i], conv_b, n)
        hc, s_c, m_c = _mlstm_call(qc_c, kc_c, vc_c, gc, gate_b, s_zero, m_zero, bsz)
        hl, _, _ = _mlstm_call(qc_l, kc_l, vc_l, gl, gate_b, s_c, m_c, bsz)
        m_l = _mlstm_out_call(hl, pl_, mlstm_norm_g[i])

        yl = _merge_call(ul, a_l, n_l, m_l, wg, b_gate[i], wb)
        xl = _down_call(yl, wo, xl, ml[:, 5], 1.0)
        xl = ffn(xl, ml, 6, norm_g[i, 2], i, 1)
        if not last:
            a_c = _attn_a_call(qa_c, ka_c, va_c, ka_c, va_c, attn_sink[i], bsz, False)
            n_c = _attn_bx_call(qb_c, kb_c, vb_c, bsz)
            mm_c = _mlstm_out_call(hc, pc, mlstm_norm_g[i])
            yc = _merge_call(uc, a_c, n_c, mm_c, wg, b_gate[i], wb)
            xc = _down_call(yc, wo, xc, mc[:, 5], 1.0)
            xc = ffn(xc, mc, 6, norm_g[i, 2], i, 1)
    return xl.reshape(bsz, n, d)
```

```python
import functools

import numpy as np
import jax
import jax.numpy as jnp
from jax import lax
from jax.experimental import pallas as pl
from jax.experimental.pallas import tpu as pltpu

F32 = jnp.float32
BF16 = jnp.bfloat16

HEAD_DIM = 128
GRID_W = 64
ROPE_PAIRS = HEAD_DIM // 4
ROPE_THETA = 10000.0
A_HEADS = 8
A_KV_HEADS = 2
A_GROUP = A_HEADS // A_KV_HEADS
A_WINDOW = 128
A_BLOCK = 128
B_HEADS = 8
NA_ROWS = 8
NA_COLS = 16
C_HEADS = 8
C_CONV = 5
BRANCH_W = 8 * HEAD_DIM
N_MOD = 9
EPS = 1e-6
NEG = -1e30
ATTN_SCALE = HEAD_DIM ** -0.5

M_CHUNK = 128
NA_TILE_ROWS = 8
NA_TILE = NA_TILE_ROWS * GRID_W
NA_WIN_ROWS = NA_TILE_ROWS + NA_ROWS
NA_HEADS_PER_STEP = 2

V7X_VMEM_BYTES = 64 * 1024 * 1024
VMEM_CAP = V7X_VMEM_BYTES - 6 * 1024 * 1024

COL_AQ, COL_BQ, COL_BK, COL_BV, COL_CQ, COL_CK, COL_CV, COL_CO = range(8)
MAIN_W = 8 * BRANCH_W + 2 * A_KV_HEADS * HEAD_DIM
GATE_W = 128


def _pick(n, cands):
    for c in cands:
        if n % c == 0:
            return c
    raise ValueError(f"no tile in {cands} divides {n}")


def _params(sem, block_bytes, temp_bytes=0):
    limit = 2 * block_bytes + temp_bytes + 4 * 1024 * 1024
    return pltpu.CompilerParams(dimension_semantics=sem,
                                vmem_limit_bytes=int(min(max(limit, 16 * 1024 * 1024), VMEM_CAP)))


def _nbytes(shape, dtype):
    return int(np.prod(shape)) * jnp.dtype(dtype).itemsize


def _dot(a, b):
    return jnp.dot(a, b, preferred_element_type=F32)


def _dot_nt(a, b):
    return lax.dot_general(a, b, (((1,), (1,)), ((), ())), preferred_element_type=F32)


def _dot_tn(a, b):
    return lax.dot_general(a, b, (((0,), (0,)), ((), ())), preferred_element_type=F32)


def _silu(x):
    return x * jax.nn.sigmoid(x)


def _mod_kernel(c_ref, w_ref, b_ref, o_ref):
    a = _silu(c_ref[...]).astype(BF16)
    o_ref[0] = _dot(a, w_ref[0].astype(BF16)) + b_ref[0]


def _mod_call(cs, w_mod, b_mod):
    depth, d, nd = w_mod.shape
    tn = _pick(nd, (512, 256, 128))
    rows = cs.shape[0]
    blocks = _nbytes((d, tn), F32) + _nbytes((rows, d), F32) + _nbytes((rows, tn), F32)
    return pl.pallas_call(
        _mod_kernel,
        out_shape=jax.ShapeDtypeStruct((depth, rows, nd), F32),
        grid=(depth, nd // tn),
        in_specs=[pl.BlockSpec((rows, d), lambda l, j: (0, 0)),
                  pl.BlockSpec((1, d, tn), lambda l, j: (l, 0, j)),
                  pl.BlockSpec((1, 1, tn), lambda l, j: (l, 0, j))],
        out_specs=pl.BlockSpec((1, rows, tn), lambda l, j: (l, 0, j)),
        compiler_params=_params(("parallel", "parallel"), blocks, _nbytes((d, tn), BF16)),
        name="mod_vectors",
    )(cs, w_mod, b_mod.reshape(depth, 1, nd))


def _normmod_kernel(x_ref, g_ref, shift_ref, scale_ref, o_ref):
    x = x_ref[...]
    ms = jnp.mean(x * x, axis=-1, keepdims=True)
    y = x * lax.rsqrt(ms + EPS) * g_ref[...]
    o_ref[...] = (y * (1.0 + scale_ref[0]) + shift_ref[0]).astype(o_ref.dtype)


def _normmod_call(x, g, shift, scale):
    m, d = x.shape
    groups = shift.shape[0]
    tr = _pick(m // groups, (256, 128, 64, 8))
    per = (m // groups) // tr
    blocks = _nbytes((tr, d), F32) + _nbytes((tr, d), BF16) + 3 * _nbytes((1, d), F32)
    return pl.pallas_call(
        _normmod_kernel,
        out_shape=jax.ShapeDtypeStruct((m, d), BF16),
        grid=(m // tr,),
        in_specs=[pl.BlockSpec((tr, d), lambda i: (i, 0)),
                  pl.BlockSpec((1, d), lambda i: (0, 0)),
                  pl.BlockSpec((1, 1, d), lambda i: (i // per, 0, 0)),
                  pl.BlockSpec((1, 1, d), lambda i: (i // per, 0, 0))],
        out_specs=pl.BlockSpec((tr, d), lambda i: (i, 0)),
        compiler_params=_params(("parallel",), blocks, 2 * _nbytes((tr, d), F32)),
        name="norm_modulate",
    )(x, g.reshape(1, d), shift.reshape(groups, 1, d), scale.reshape(groups, 1, d))


def _up_kernel(u_ref, w1_ref, w3_ref, o_ref):
    u = u_ref[...]
    h1 = _dot(u, w1_ref[...].astype(BF16))
    h3 = _dot(u, w3_ref[...].astype(BF16))
    o_ref[...] = (_silu(h1) * h3).astype(o_ref.dtype)


def _up_call(u, w1, w3, layer, sub):
    m, d = u.shape
    f = w1.shape[-1]
    tm = _pick(m, (1024, 512, 256))
    tf = _pick(f, (256, 128))
    wspec = pl.BlockSpec((None, None, d, tf), lambda i, j: (layer, sub, 0, j))
    blocks = _nbytes((tm, d), BF16) + 2 * _nbytes((d, tf), F32) + _nbytes((tm, tf), BF16)
    return pl.pallas_call(
        _up_kernel,
        out_shape=jax.ShapeDtypeStruct((m, f), BF16),
        grid=(m // tm, f // tf),
        in_specs=[pl.BlockSpec((tm, d), lambda i, j: (i, 0)), wspec, wspec],
        out_specs=pl.BlockSpec((tm, tf), lambda i, j: (i, j)),
        compiler_params=_params(("parallel", "parallel"), blocks,
                                2 * _nbytes((d, tf), BF16) + 4 * _nbytes((tm, tf), F32)),
        name="ffn_up",
    )(u, w1, w3)


def _down_kernel(a_ref, w_ref, x_ref, gate_ref, o_ref, *, coef):
    acc = _dot(a_ref[...], w_ref[...].astype(BF16))
    o_ref[...] = x_ref[...] + (coef * gate_ref[0]) * acc


def _down_call(a, w, x, gate, coef, lead=()):
    m, k = a.shape
    d = w.shape[-1]
    groups = gate.shape[0]
    tm = _pick(m // groups, (1024, 512, 256))
    tn = _pick(d, (512, 256, 128) if k <= 4096 else (256, 128))
    per = (m // groups) // tm
    blocks = (_nbytes((tm, k), BF16) + _nbytes((k, tn), w.dtype) + 2 * _nbytes((tm, tn), F32)
              + _nbytes((1, tn), F32))
    temps = 2 * _nbytes((tm, tn), F32) + (_nbytes((k, tn), BF16) if w.dtype != BF16 else 0)
    wspec = pl.BlockSpec((None,) * len(lead) + (k, tn), lambda i, j: tuple(lead) + (0, j))
    return pl.pallas_call(
        functools.partial(_down_kernel, coef=coef),
        out_shape=jax.ShapeDtypeStruct((m, d), F32),
        grid=(m // tm, d // tn),
        in_specs=[pl.BlockSpec((tm, k), lambda i, j: (i, 0)),
                  wspec,
                  pl.BlockSpec((tm, tn), lambda i, j: (i, j)),
                  pl.BlockSpec((1, 1, tn), lambda i, j: (i // per, 0, j))],
        out_specs=pl.BlockSpec((tm, tn), lambda i, j: (i, j)),
        compiler_params=_params(("parallel", "parallel"), blocks, temps),
        name="proj_residual",
    )(a, w, x, gate.reshape(groups, 1, d))


def _matmul_kernel(a_ref, w_ref, o_ref):
    o_ref[...] = _dot(a_ref[...], w_ref[...]).astype(o_ref.dtype)


def _matmul_call(a, w, out_dtype=F32):
    m, k = a.shape
    n = w.shape[1]
    tm = _pick(m, (1024, 512, 256))
    tn = _pick(n, (512, 256, 128))
    blocks = _nbytes((tm, k), BF16) + _nbytes((k, tn), BF16) + _nbytes((tm, tn), out_dtype)
    return pl.pallas_call(
        _matmul_kernel,
        out_shape=jax.ShapeDtypeStruct((m, n), out_dtype),
        grid=(m // tm, n // tn),
        in_specs=[pl.BlockSpec((tm, k), lambda i, j: (i, 0)),
                  pl.BlockSpec((k, tn), lambda i, j: (0, j))],
        out_specs=pl.BlockSpec((tm, tn), lambda i, j: (i, j)),
        compiler_params=_params(("parallel", "parallel"), blocks, _nbytes((tm, tn), F32)),
        name="in_proj",
    )(a, w)


def _merge_kernel(u_ref, oa_ref, ob_ref, oc_ref, wg_ref, bg_ref, wb_ref, y_ref):
    u = u_ref[...]
    acc = None
    for j, br_ref in enumerate((oa_ref, ob_ref, oc_ref)):
        gate = jax.nn.sigmoid(_dot(u, wg_ref[j]) + bg_ref[j])
        term = gate * _dot(br_ref[...], wb_ref[j])
        acc = term if acc is None else acc + term
    y_ref[...] = acc.astype(y_ref.dtype)


def _merge_call(u, oa, ob, oc, wg, bg, wb):
    m, d = u.shape
    bw = oa.shape[1]
    tm = _pick(m, (1024, 512, 256))
    tn = _pick(d, (256, 128))
    blocks = (_nbytes((tm, d), BF16) + 3 * _nbytes((tm, bw), BF16) + 3 * _nbytes((d, tn), BF16)
              + 3 * _nbytes((bw, tn), BF16) + _nbytes((tm, tn), BF16))
    return pl.pallas_call(
        _merge_kernel,
        out_shape=jax.ShapeDtypeStruct((m, d), BF16),
        grid=(m // tm, d // tn),
        in_specs=[pl.BlockSpec((tm, d), lambda i, j: (i, 0)),
                  pl.BlockSpec((tm, bw), lambda i, j: (i, 0)),
                  pl.BlockSpec((tm, bw), lambda i, j: (i, 0)),
                  pl.BlockSpec((tm, bw), lambda i, j: (i, 0)),
                  pl.BlockSpec((3, d, tn), lambda i, j: (0, 0, j)),
                  pl.BlockSpec((3, 1, tn), lambda i, j: (0, 0, j)),
                  pl.BlockSpec((3, bw, tn), lambda i, j: (0, 0, j))],
        out_specs=pl.BlockSpec((tm, tn), lambda i, j: (i, j)),
        compiler_params=_params(("parallel", "parallel"), blocks, 10 * _nbytes((tm, tn), F32)),
        name="gated_merge",
    )(u, oa, ob, oc, wg, bg.reshape(3, 1, d), wb)


def _head_norm(x, g):
    ms = jnp.mean(x * x, axis=-1, keepdims=True)
    return x * lax.rsqrt(ms + EPS) * g


def _rope(y, cos, sin_signed):
    lane = lax.broadcasted_iota(jnp.int32, y.shape, 1)
    partner = jnp.where((lane % 64) < ROPE_PAIRS, pltpu.roll(y, HEAD_DIM - ROPE_PAIRS, axis=1),
                        pltpu.roll(y, ROPE_PAIRS, axis=1))
    return y * cos + partner * sin_signed


def _qkprep_kernel(p4_ref, pa_ref, g_ref, cos_ref, sin_ref,
                   qa_ref, qb_ref, kb_ref, vb_ref, ka_ref, va_ref, *, rope):
    hd = HEAD_DIM
    if rope:
        cos = cos_ref[...]
        sin = sin_ref[...]
    for h in range(A_HEADS):
        y = _head_norm(p4_ref[:, h * hd:(h + 1) * hd], g_ref[0:1, :])
        if rope:
            y = _rope(y, cos, sin)
        qa_ref[:, h * hd:(h + 1) * hd] = y.astype(BF16)
    for h in range(B_HEADS):
        c0 = BRANCH_W + h * hd
        qb_ref[:, h * hd:(h + 1) * hd] = _head_norm(p4_ref[:, c0:c0 + hd], g_ref[2:3, :]).astype(BF16)
        c0 = 2 * BRANCH_W + h * hd
        kb_ref[:, h * hd:(h + 1) * hd] = _head_norm(p4_ref[:, c0:c0 + hd], g_ref[3:4, :]).astype(BF16)
    vb_ref[...] = p4_ref[:, 3 * BRANCH_W:4 * BRANCH_W].astype(BF16)
    for h in range(A_KV_HEADS):
        y = _head_norm(pa_ref[:, h * hd:(h + 1) * hd], g_ref[1:2, :])
        if rope:
            y = _rope(y, cos, sin)
        ka_ref[:, h * hd:(h + 1) * hd] = y.astype(BF16)
    kvw = A_KV_HEADS * hd
    va_ref[...] = pa_ref[:, kvw:2 * kvw].astype(BF16)


def _qkprep_call(p_main, qk_g, cos_t, sin_t, rope, seq):
    m = p_main.shape[0]
    tr = _pick(seq, (256, 128))
    per = seq // tr
    kvw = A_KV_HEADS * HEAD_DIM
    w4 = 4 * BRANCH_W
    blocks = (_nbytes((tr, w4), F32) + _nbytes((tr, 2 * kvw), F32) + 2 * _nbytes((tr, HEAD_DIM), F32)
              + _nbytes((tr, w4), BF16) + _nbytes((tr, 2 * kvw), BF16))
    outs = [jax.ShapeDtypeStruct((m, BRANCH_W), BF16)] * 4 + [jax.ShapeDtypeStruct((m, kvw), BF16)] * 2
    return pl.pallas_call(
        functools.partial(_qkprep_kernel, rope=rope),
        out_shape=outs,
        grid=(m // tr,),
        in_specs=[pl.BlockSpec((tr, w4), lambda i: (i, 0)),
                  pl.BlockSpec((tr, 2 * kvw), lambda i: (i, (8 * BRANCH_W) // (2 * kvw))),
                  pl.BlockSpec((4, HEAD_DIM), lambda i: (0, 0)),
                  pl.BlockSpec((tr, HEAD_DIM), lambda i: (i % per, 0)),
                  pl.BlockSpec((tr, HEAD_DIM), lambda i: (i % per, 0))],
        out_specs=[pl.BlockSpec((tr, BRANCH_W), lambda i: (i, 0))] * 4
                  + [pl.BlockSpec((tr, kvw), lambda i: (i, 0))] * 2,
        compiler_params=_params(("parallel",), blocks, 4 * _nbytes((tr, HEAD_DIM), F32)),
        name="qk_prep",
    )(p_main, p_main, qk_g, cos_t, sin_t)


def _attn_a_kernel(sink_ref, q_ref, *rest, n_tok, lc, window):
    if window:
        kp_ref, kc_ref, kn_ref, vp_ref, vc_ref, vn_ref, kx_ref, vx_ref, o_ref = rest
    else:
        kx_ref, vx_ref, o_ref = rest
    i = pl.program_id(1)
    hd = HEAD_DIM
    blk = A_BLOCK
    gw = A_GROUP * hd
    keys = lc + 3 * blk if window else lc
    if window:
        row = lax.broadcasted_iota(jnp.int32, (A_GROUP * blk, keys), 0) % blk
        col = lax.broadcasted_iota(jnp.int32, (A_GROUP * blk, keys), 1)
        qpos = i * blk + row
        kpos = (i - 1) * blk + (col - lc)
        band = (jnp.abs(qpos - kpos) <= A_WINDOW) & (kpos >= 0) & (kpos < n_tok)
        visible = (col < lc) | band
    for hk in range(A_KV_HEADS):
        cs = slice(hk * hd, (hk + 1) * hd)
        if window:
            kall = jnp.concatenate([kx_ref[:, cs], kp_ref[:, cs], kc_ref[:, cs], kn_ref[:, cs]], axis=0)
            vall = jnp.concatenate([vx_ref[:, cs], vp_ref[:, cs], vc_ref[:, cs], vn_ref[:, cs]], axis=0)
        else:
            kall = kx_ref[:, cs]
            vall = vx_ref[:, cs]
        q4 = jnp.concatenate([q_ref[:, hk * gw + g * hd:hk * gw + (g + 1) * hd] for g in range(A_GROUP)],
                             axis=0)
        s = _dot_nt(q4, kall) * ATTN_SCALE
        if window:
            s = jnp.where(visible, s, NEG)
        snk = jnp.concatenate([jnp.full((blk, 1), sink_ref[hk * A_GROUP + g], F32) for g in range(A_GROUP)],
                              axis=0)
        mx = jnp.maximum(jnp.max(s, axis=-1, keepdims=True), snk)
        e = jnp.exp(s - mx)
        den = jnp.sum(e, axis=-1, keepdims=True) + jnp.exp(snk - mx)
        p = (e * (1.0 / den)).astype(BF16)
        o = _dot(p, vall)
        for g in range(A_GROUP):
            o_ref[:, hk * gw + g * hd:hk * gw + (g + 1) * hd] = o[g * blk:(g + 1) * blk].astype(o_ref.dtype)


def _attn_a_call(qa, ka, va, kx, vx, sink, bsz, window):
    n = qa.shape[0] // bsz
    lc = kx.shape[0] // bsz
    blk = A_BLOCK
    nb = n // blk
    kvw = A_KV_HEADS * HEAD_DIM
    qw = A_HEADS * HEAD_DIM
    qspec = pl.BlockSpec((blk, qw), lambda b, i: (b * nb + i, 0))
    xspec = pl.BlockSpec((lc, kvw), lambda b, i: (b, 0))
    sspec = pl.BlockSpec(memory_space=pltpu.SMEM)
    if window:
        prev = pl.BlockSpec((blk, kvw), lambda b, i: (b * nb + jnp.maximum(i - 1, 0), 0))
        cur = pl.BlockSpec((blk, kvw), lambda b, i: (b * nb + i, 0))
        nxt = pl.BlockSpec((blk, kvw), lambda b, i: (b * nb + jnp.minimum(i + 1, nb - 1), 0))
        in_specs = [sspec, qspec, prev, cur, nxt, prev, cur, nxt, xspec, xspec]
        args = (sink, qa, ka, ka, ka, va, va, va, kx, vx)
        keys = lc + 3 * blk
    else:
        in_specs = [sspec, qspec, xspec, xspec]
        args = (sink, qa, kx, vx)
        keys = lc
    blocks = 2 * _nbytes((blk, qw), BF16) + 2 * _nbytes((keys, kvw), BF16)
    return pl.pallas_call(
        functools.partial(_attn_a_kernel, n_tok=n, lc=lc, window=window),
        out_shape=jax.ShapeDtypeStruct(qa.shape, BF16),
        grid=(bsz, nb),
        in_specs=in_specs,
        out_specs=qspec,
        compiler_params=_params(("parallel", "parallel"), blocks,
                                10 * _nbytes((A_GROUP * blk, keys), F32)),
        name="windowed_gqa" if window else "context_gqa",
    )(*args)


def _softmax2(s1, s2):
    mx = jnp.maximum(jnp.max(s1, axis=-1, keepdims=True), jnp.max(s2, axis=-1, keepdims=True))
    e1 = jnp.exp(s1 - mx)
    e2 = jnp.exp(s2 - mx)
    inv = 1.0 / (jnp.sum(e1, axis=-1, keepdims=True) + jnp.sum(e2, axis=-1, keepdims=True))
    return (e1 * inv).astype(BF16), (e2 * inv).astype(BF16)


def _attn_b_kernel(q_ref, k0_ref, k1_ref, k2_ref, k3_ref, v0_ref, v1_ref, v2_ref, v3_ref, kx_ref, vx_ref,
                   bias_ref, o_ref):
    hd = HEAD_DIM
    for h in range(NA_HEADS_PER_STEP):
        cs = slice(h * hd, (h + 1) * hd)
        kwin = jnp.concatenate([k0_ref[:, cs], k1_ref[:, cs], k2_ref[:, cs], k3_ref[:, cs]], axis=0)
        vwin = jnp.concatenate([v0_ref[:, cs], v1_ref[:, cs], v2_ref[:, cs], v3_ref[:, cs]], axis=0)
        q = q_ref[:, cs]
        s_nb = _dot_nt(q, kwin) * ATTN_SCALE + bias_ref[h, 0]
        s_cx = _dot_nt(q, kx_ref[:, cs]) * ATTN_SCALE
        p_nb, p_cx = _softmax2(s_nb, s_cx)
        o_ref[:, cs] = (_dot(p_nb, vwin) + _dot(p_cx, vx_ref[:, cs])).astype(o_ref.dtype)


def _attn_bx_kernel(q_ref, kx_ref, vx_ref, o_ref):
    hd = HEAD_DIM
    for h in range(B_HEADS):
        cs = slice(h * hd, (h + 1) * hd)
        s = _dot_nt(q_ref[:, cs], kx_ref[:, cs]) * ATTN_SCALE
        mx = jnp.max(s, axis=-1, keepdims=True)
        e = jnp.exp(s - mx)
        p = (e * (1.0 / jnp.sum(e, axis=-1, keepdims=True))).astype(BF16)
        o_ref[:, cs] = _dot(p, vx_ref[:, cs]).astype(o_ref.dtype)


def _na_bias_kernel(rel_ref, o_ref):
    h = pl.program_id(0)
    pos = pl.program_id(1)
    nrel_r = 2 * NA_ROWS - 1
    nrel_c = 2 * NA_COLS - 1
    qi = lax.broadcasted_iota(jnp.int32, (GRID_W, GRID_W), 0)
    ki = lax.broadcasted_iota(jnp.int32, (GRID_W, GRID_W), 1)
    dc = jnp.clip(ki - qi, -(NA_COLS - 1), NA_COLS - 1) + NA_COLS - 1
    start = jnp.clip(qi - NA_COLS // 2, 0, GRID_W - NA_COLS)
    col_in = (ki >= start) & (ki < start + NA_COLS)
    planes = []
    for dr in range(nrel_r):
        acc = jnp.zeros((GRID_W, GRID_W), F32)
        for dcv in range(nrel_c):
            acc = jnp.where(dc == dcv, rel_ref[(h * nrel_r + dr) * nrel_c + dcv], acc)
        planes.append(jnp.where(col_in, acc, NEG))
    masked = jnp.full((GRID_W, GRID_W), NEG, F32)
    half = NA_ROWS // 2
    first_key = {0: lambda j: max(j, half), 1: lambda j: j, 2: lambda j: min(j, half)}
    for p, lo_of in first_key.items():
        @pl.when(pos == p)
        def _(lo_of=lo_of):
            for j in range(NA_TILE_ROWS):
                lo = lo_of(j)
                for jj in range(NA_WIN_ROWS):
                    blk = planes[jj - j + half - 1] if lo <= jj < lo + NA_ROWS else masked
                    o_ref[0, 0, j * GRID_W:(j + 1) * GRID_W, jj * GRID_W:(jj + 1) * GRID_W] = blk


def _na_bias_table(relpos):
    nh = relpos.shape[0]
    shape = (nh, 3, NA_TILE, NA_WIN_ROWS * GRID_W)
    return pl.pallas_call(
        _na_bias_kernel,
        out_shape=jax.ShapeDtypeStruct(shape, F32),
        grid=(nh, 3),
        in_specs=[pl.BlockSpec(memory_space=pltpu.SMEM)],
        out_specs=pl.BlockSpec((1, 1) + shape[2:], lambda h, p: (h, p, 0, 0)),
        compiler_params=_params(("parallel", "parallel"), _nbytes(shape[2:], F32)),
        name="na_bias_table",
    )(relpos.reshape(-1))


def _attn_b_call(qb, kb, vb, kx, vx, bias, bsz):
    n = qb.shape[0] // bsz
    lc = kx.shape[0] // bsz
    rows = n // GRID_W
    assert rows % NA_TILE_ROWS == 0 and rows >= 2 * NA_TILE_ROWS
    nt = rows // NA_TILE_ROWS
    hp = NA_HEADS_PER_STEP
    hd = hp * HEAD_DIM
    sub = NA_TILE // 2
    qspec = pl.BlockSpec((NA_TILE, hd), lambda b, h, t: (b * nt + t, h))

    def kspec(s):
        return pl.BlockSpec((sub, hd),
                            lambda b, h, t: (jnp.clip(2 * t + s, 0, 2 * nt - 1) + 2 * b * nt, h))

    kspecs = [kspec(s) for s in (-1, 0, 1, 2)]
    xspec = pl.BlockSpec((lc, hd), lambda b, h, t: (b, h))
    bspec = pl.BlockSpec((hp, 1) + bias.shape[2:],
                         lambda b, h, t: (h, jnp.where(t == 0, 0, jnp.where(t == nt - 1, 2, 1)), 0, 0))
    blocks = (2 * _nbytes((NA_TILE, hd), BF16) + 8 * _nbytes((sub, hd), BF16) + 2 * _nbytes((lc, hd), BF16)
              + hp * _nbytes(bias.shape[2:], F32))
    return pl.pallas_call(
        _attn_b_kernel,
        out_shape=jax.ShapeDtypeStruct(qb.shape, BF16),
        grid=(bsz, B_HEADS // hp, nt),
        in_specs=[qspec] + kspecs + kspecs + [xspec, xspec, bspec],
        out_specs=qspec,
        compiler_params=_params(("parallel", "parallel", "parallel"), blocks,
                                5 * hp * _nbytes((NA_TILE, NA_WIN_ROWS * GRID_W + lc), F32)),
        name="neighbourhood_attn",
    )(qb, kb, kb, kb, kb, vb, vb, vb, vb, kx, vx, bias)


def _attn_bx_call(qx, kx, vx, bsz):
    lc = qx.shape[0] // bsz
    w = qx.shape[1]
    spec = pl.BlockSpec((lc, w), lambda b: (b, 0))
    return pl.pallas_call(
        _attn_bx_kernel,
        out_shape=jax.ShapeDtypeStruct(qx.shape, BF16),
        grid=(bsz,),
        in_specs=[spec, spec, spec],
        out_specs=spec,
        compiler_params=_params(("parallel",), 4 * _nbytes((lc, w), BF16), 4 * 1024 * 1024),
        name="context_full_attn",
    )(qx, kx, vx)


def _log_sigmoid(x):
    return -(jnp.maximum(-x, 0.0) + jnp.log1p(jnp.exp(-jnp.abs(x))))


CONV_HALO = 8


def _cprep_kernel(q_ref, k_ref, v_ref, qlo_ref, qhi_ref, klo_ref, khi_ref, cw_ref, cb_ref,
                  qs_ref, kt_ref, vb_ref, ext, *, per):
    pos = pl.program_id(0) % per
    tr = q_ref.shape[0]
    hw = q_ref.shape[1]
    halo = CONV_HALO
    pad = C_CONV // 2

    def conv_silu(x_ref, lo_ref, hi_ref, col0):
        ext[0:halo, :] = jnp.where(pos > 0, lo_ref[...], 0.0)
        ext[halo:halo + tr, :] = x_ref[...]
        ext[halo + tr:2 * halo + tr, :] = jnp.where(pos < per - 1, hi_ref[...], 0.0)
        out = cb_ref[:, col0:col0 + hw]
        for j in range(C_CONV):
            out = out + cw_ref[j:j + 1, col0:col0 + hw] * ext[halo - pad + j:halo - pad + j + tr, :]
        return _silu(out)

    qs_ref[...] = (conv_silu(q_ref, qlo_ref, qhi_ref, 0) * ATTN_SCALE).astype(BF16)
    kc = conv_silu(k_ref, klo_ref, khi_ref, hw)
    for ck in range(tr // M_CHUNK):
        for h in range(C_HEADS):
            blk = kc[ck * M_CHUNK:(ck + 1) * M_CHUNK, h * HEAD_DIM:(h + 1) * HEAD_DIM]
            kt_ref[ck, h * HEAD_DIM:(h + 1) * HEAD_DIM, :] = blk.T.astype(BF16)
    vb_ref[...] = v_ref[...].astype(BF16)


def _cprep_call(p_main, conv_w, conv_b, seq):
    m = p_main.shape[0]
    hw = C_HEADS * HEAD_DIM
    assert M_CHUNK == HEAD_DIM
    tr = _pick(seq, (256, 128))
    per = seq // tr
    halo = CONV_HALO
    hb = tr // halo
    nblk = m // halo
    cpb = tr // M_CHUNK

    def main(col):
        return pl.BlockSpec((tr, hw), lambda i: (i, col))

    def lo(col):
        return pl.BlockSpec((halo, hw), lambda i: (jnp.maximum(i * hb - 1, 0), col))

    def hi(col):
        return pl.BlockSpec((halo, hw), lambda i: (jnp.minimum((i + 1) * hb, nblk - 1), col))

    blocks = (3 * _nbytes((tr, hw), F32) + 4 * _nbytes((halo, hw), F32) + _nbytes((C_CONV + 1, 2 * hw), F32)
              + 3 * _nbytes((tr, hw), BF16))
    scratch = _nbytes((tr + 2 * halo, hw), F32)
    out = jax.ShapeDtypeStruct((m, hw), BF16)
    out_t = jax.ShapeDtypeStruct((m // M_CHUNK, hw, M_CHUNK), BF16)
    ospec = pl.BlockSpec((tr, hw), lambda i: (i, 0))
    tspec = pl.BlockSpec((cpb, hw, M_CHUNK), lambda i: (i, 0, 0))
    return pl.pallas_call(
        functools.partial(_cprep_kernel, per=per),
        out_shape=[out, out_t, out],
        grid=(m // tr,),
        in_specs=[main(COL_CQ), main(COL_CK), main(COL_CV), lo(COL_CQ), hi(COL_CQ), lo(COL_CK), hi(COL_CK),
                  pl.BlockSpec((C_CONV, 2 * hw), lambda i: (0, 0)),
                  pl.BlockSpec((1, 2 * hw), lambda i: (0, 0))],
        out_specs=[ospec, tspec, ospec],
        scratch_shapes=[pltpu.VMEM((tr + 2 * halo, hw), F32)],
        compiler_params=_params(("parallel",), blocks, scratch + 4 * _nbytes((tr, hw), F32)),
        name="mlstm_prep",
    )(p_main, p_main, p_main, p_main, p_main, p_main, p_main, conv_w, conv_b)


GP_PLANES = 6


def _gateprep_kernel(g_ref, gb_ref, o_ref):
    nh = C_HEADS
    L = M_CHUNK
    lane = lax.broadcasted_iota(jnp.int32, (nh, L), 1)
    for ck in range(g_ref.shape[0] // L):
        gt = (g_ref[ck * L:(ck + 1) * L, :] + gb_ref[...]).T
        for d in range(2):
            ig = gt[2 * d * nh:(2 * d + 1) * nh]
            logf = _log_sigmoid(gt[(2 * d + 1) * nh:(2 * d + 2) * nh])
            pre = logf
            sft = 1
            while sft < L:
                pre = pre + jnp.where(lane >= sft, pltpu.roll(pre, sft, axis=1), 0.0)
                sft *= 2
            total = jnp.broadcast_to(pre[:, L - 1:L], (nh, L))
            bcum = pre if d == 0 else total - pre + logf
            r_row = ig - bcum
            rmax = r_row
            sft = 1
            while sft < L:
                if d == 0:
                    rmax = jnp.maximum(rmax, jnp.where(lane >= sft, pltpu.roll(rmax, sft, axis=1), NEG))
                else:
                    rmax = jnp.maximum(rmax, jnp.where(lane < L - sft, pltpu.roll(rmax, L - sft, axis=1), NEG))
                sft *= 2
            r_top = jnp.broadcast_to(rmax[:, L - 1:L] if d == 0 else rmax[:, 0:1], (nh, L))
            w_row = jnp.exp(r_row - r_top)
            for k, plane in enumerate((r_row, rmax, bcum, total, w_row, r_top)):
                o_ref[ck, d, k * nh:(k + 1) * nh, :] = plane


def _gateprep_call(gates, gate_b, seq):
    m = gates.shape[0]
    L = M_CHUNK
    tr = _pick(seq, (1024, 512, 256, 128))
    shape = (m // L, 2, GP_PLANES * C_HEADS, L)
    blocks = _nbytes((tr, GATE_W), F32) + _nbytes((tr // L,) + shape[1:], F32)
    return pl.pallas_call(
        _gateprep_kernel,
        out_shape=jax.ShapeDtypeStruct(shape, F32),
        grid=(m // tr,),
        in_specs=[pl.BlockSpec((tr, GATE_W), lambda i: (i, 0)),
                  pl.BlockSpec((1, GATE_W), lambda i: (0, 0))],
        out_specs=pl.BlockSpec((tr // L,) + shape[1:], lambda i: (i, 0, 0, 0)),
        compiler_params=_params(("parallel",), blocks, 2 * 1024 * 1024),
        name="mlstm_gate_prep",
    )(gates, gate_b)


def _mlstm_kernel(q_ref, kt_ref, v_ref, gp_ref, s0_ref, m0_ref, h_ref, sf_ref, mf_ref, cst, mst, *, nc):
    d = pl.program_id(1)
    c = pl.program_id(2)
    L = M_CHUNK
    hd = HEAD_DIM
    nh = C_HEADS

    @pl.when(c == 0)
    def _():
        cst[...] = s0_ref[...]
        mst[...] = m0_ref[...]

    r_row, rmax, bcum, total, w_row, r_top = (gp_ref[0, 0, k * nh:(k + 1) * nh, :] for k in range(GP_PLANES))
    m0 = mst[...]
    big_m = jnp.maximum(rmax, m0)
    si_row = jnp.exp(m0 - big_m)
    nrm_row = jnp.exp(-(bcum + big_m))
    m_loc = total + r_top
    m_new = jnp.maximum(total + m0, m_loc)
    s_prev = jnp.exp(total + m0 - m_new)
    s_loc = jnp.exp(m_loc - m_new)
    mst[...] = m_new
    xt = jnp.concatenate([big_m, si_row, nrm_row, jnp.zeros((L - 3 * nh, L), F32)], axis=0).T

    ii = lax.broadcasted_iota(jnp.int32, (L, L), 0)
    jj = lax.broadcasted_iota(jnp.int32, (L, L), 1)
    causal = jnp.where(d == 0, ii - jj, jj - ii) >= 0
    ones = jnp.ones((L, hd), BF16)

    for h in range(nh):
        cs = slice(h * hd, (h + 1) * hd)
        qs = q_ref[:, cs]
        kt = kt_ref[0, cs, :]
        vext = jnp.concatenate([v_ref[:, cs], ones], axis=1)
        cext = cst[h]
        expo = jnp.where(causal, r_row[h:h + 1, :] - xt[:, h:h + 1], NEG)
        sw = jnp.exp(expo) * _dot(qs, kt)
        si = jnp.broadcast_to(xt[:, nh + h:nh + h + 1], (L, 2 * hd))
        both = _dot(sw.astype(BF16), vext) + si * _dot(qs, cext.astype(BF16))
        nrm = jnp.broadcast_to(xt[:, 2 * nh + h:2 * nh + h + 1], (L, hd))
        h_ref[0, :, cs] = both[:, 0:hd] / jnp.maximum(jnp.abs(both[:, hd:2 * hd]), nrm)
        kwt = (kt.astype(F32) * w_row[h:h + 1, :]).astype(BF16)
        sp = jnp.concatenate([s_prev[h:h + 1, :], s_prev[h:h + 1, :]], axis=1)
        sl = jnp.concatenate([s_loc[h:h + 1, :], s_loc[h:h + 1, :]], axis=1)
        cst[h] = sp * cext + sl * _dot(kwt, vext)

    @pl.when(c == nc - 1)
    def _():
        sf_ref[...] = cst[...]
        mf_ref[...] = mst[...]


def _mlstm_call(qs, kt, vb, gplanes, s0, m0, bsz):
    t = qs.shape[0] // bsz
    L = M_CHUNK
    nc = t // L
    nh = C_HEADS
    hw = nh * HEAD_DIM

    def cidx(d, c):
        return jnp.where(d == 0, c, nc - 1 - c)

    main = pl.BlockSpec((L, hw), lambda b, d, c: (b * nc + cidx(d, c), 0))
    tspec = pl.BlockSpec((1, hw, L), lambda b, d, c: (b * nc + cidx(d, c), 0, 0))
    sspec = pl.BlockSpec((nh, HEAD_DIM, 2 * HEAD_DIM), lambda b, d, c: (b * 2 + d, 0, 0))
    mspec = pl.BlockSpec((None, nh, 128), lambda b, d, c: (b * 2 + d, 0, 0))
    gspec = pl.BlockSpec((1, 1, GP_PLANES * nh, L), lambda b, d, c: (b * nc + cidx(d, c), d, 0, 0))
    blocks = (3 * _nbytes((L, hw), BF16) + _nbytes((GP_PLANES * nh, L), F32)
              + 2 * _nbytes((nh, HEAD_DIM, 2 * HEAD_DIM), F32)
              + 2 * _nbytes((nh, 128), F32) + _nbytes((L, hw), F32))
    scratch = _nbytes((nh, HEAD_DIM, 2 * HEAD_DIM), F32)
    return pl.pallas_call(
        functools.partial(_mlstm_kernel, nc=nc),
        out_shape=[jax.ShapeDtypeStruct((2, bsz * t, hw), F32),
                   jax.ShapeDtypeStruct(s0.shape, F32), jax.ShapeDtypeStruct(m0.shape, F32)],
        grid=(bsz, 2, nc),
        in_specs=[main, tspec, main, gspec, sspec, mspec],
        out_specs=[pl.BlockSpec((1, L, hw), lambda b, d, c: (d, b * nc + cidx(d, c), 0)), sspec, mspec],
        scratch_shapes=[pltpu.VMEM((nh, HEAD_DIM, 2 * HEAD_DIM), F32), pltpu.VMEM((nh, 128), F32)],
        compiler_params=_params(("parallel", "parallel", "arbitrary"), blocks, scratch + 12 * 1024 * 1024),
        name="mlstm_scan",
    )(qs, kt, vb, gplanes, s0, m0)


def _mlstm_out_kernel(h_ref, o_ref, g_ref, y_ref):
    hd = HEAD_DIM
    for h in range(C_HEADS):
        cs = slice(h * hd, (h + 1) * hd)
        hn = _head_norm(h_ref[0, :, cs] + h_ref[1, :, cs], g_ref[:, cs])
        y_ref[:, cs] = (hn * jax.nn.sigmoid(o_ref[:, cs])).astype(y_ref.dtype)


def _mlstm_out_call(hdir, p_main, norm_g):
    m, hw = hdir.shape[1:]
    tr = _pick(m, (256, 128))
    blocks = 3 * _nbytes((tr, hw), F32) + _nbytes((tr, hw), BF16)
    return pl.pallas_call(
        _mlstm_out_kernel,
        out_shape=jax.ShapeDtypeStruct((m, hw), BF16),
        grid=(m // tr,),
        in_specs=[pl.BlockSpec((2, tr, hw), lambda i: (0, i, 0)),
                  pl.BlockSpec((tr, hw), lambda i: (i, COL_CO)),
                  pl.BlockSpec((1, hw), lambda i: (0, 0))],
        out_specs=pl.BlockSpec((tr, hw), lambda i: (i, 0)),
        compiler_params=_params(("parallel",), blocks, 2 * 1024 * 1024),
        name="mlstm_out",
    )(hdir, p_main, norm_g.reshape(1, hw))


def _rope_lane_tables(n):
    t = jnp.arange(n)
    inv = ROPE_THETA ** (-jnp.arange(ROPE_PAIRS, dtype=F32) / ROPE_PAIRS)
    row = (t // GRID_W).astype(F32)[:, None] * inv
    col = (t % GRID_W).astype(F32)[:, None] * inv
    cos = jnp.concatenate([jnp.cos(row), jnp.cos(row), jnp.cos(col), jnp.cos(col)], axis=1)
    sin = jnp.concatenate([-jnp.sin(row), jnp.sin(row), -jnp.sin(col), jnp.sin(col)], axis=1)
    return cos, sin


def _split_w_in(w_in):
    kv = A_KV_HEADS * HEAD_DIM
    aq_end = BRANCH_W
    akv_end = aq_end + 2 * kv
    co_end = akv_end + 7 * BRANCH_W
    main = jnp.concatenate([w_in[:, :aq_end], w_in[:, akv_end:co_end], w_in[:, aq_end:akv_end]], axis=1)
    gates = jnp.pad(w_in[:, co_end:], ((0, 0), (0, GATE_W - 4 * C_HEADS)))
    return main.astype(BF16), gates.astype(BF16)


def kernel(x, c, ctx, c_ctx, w_mod, b_mod, norm_g, ffn_w1, ffn_w3, ffn_w2, w_in, qk_g, attn_sink, na_relpos,
           mlstm_conv_w, mlstm_conv_b, mlstm_gate_b, mlstm_norm_g, w_gate, b_gate, w_branch, w_out):
    bsz, n, d = x.shape
    lc = ctx.shape[1]
    depth = w_mod.shape[0]
    assert n % GRID_W == 0 and n % M_CHUNK == 0 and lc % M_CHUNK == 0 and n % A_BLOCK == 0

    cs = jnp.zeros((8, d), F32).at[:bsz].set(c).at[bsz].set(c_ctx)
    mods = _mod_call(cs, w_mod, b_mod).reshape(depth, 8, N_MOD, d)
    cos_t, sin_t = _rope_lane_tables(n)

    xl = x.reshape(bsz * n, d)
    xc = ctx.reshape(bsz * lc, d)
    nh = C_HEADS
    s_zero = jnp.zeros((bsz * 2 * nh, HEAD_DIM, 2 * HEAD_DIM), F32)
    m_zero = jnp.zeros((bsz * 2, nh, 128), F32)

    def ffn(xs, md, k0, g, layer, sub):
        u = _normmod_call(xs, g, md[:, k0], md[:, k0 + 1])
        h = _up_call(u, ffn_w1, ffn_w3, layer, sub)
        return _down_call(h, ffn_w2, xs, md[:, k0 + 2], 0.5, lead=(layer, sub))

    for i in range(depth):
        last = i == depth - 1
        ml = mods[i, :bsz]
        mc = mods[i, bsz:bsz + 1]
        w_main, w_gates = _split_w_in(w_in[i])
        wg = w_gate[i].astype(BF16)
        wb = w_branch[i].astype(BF16)
        gate_b = jnp.pad(mlstm_gate_b[i].reshape(1, 4 * nh), ((0, 0), (0, GATE_W - 4 * nh)))
        conv_b = mlstm_conv_b[i].reshape(1, -1)
        bias_tab = _na_bias_table(na_relpos[i])

        xc = ffn(xc, mc, 0, norm_g[i, 0], i, 0)
        xl = ffn(xl, ml, 0, norm_g[i, 0], i, 0)

        uc = _normmod_call(xc, norm_g[i, 1], mc[:, 3], mc[:, 4])
        ul = _normmod_call(xl, norm_g[i, 1], ml[:, 3], ml[:, 4])
        pc = _matmul_call(uc, w_main)
        pl_ = _matmul_call(ul, w_main)
        gc = _matmul_call(uc, w_gates)
        gl = _matmul_call(ul, w_gates)

        qa_c, qb_c, kb_c, vb_c, ka_c, va_c = _qkprep_call(pc, qk_g[i], cos_t, sin_t, False, lc)
        qa_l, qb_l, kb_l, vb_l, ka_l, va_l = _qkprep_call(pl_, qk_g[i], cos_t, sin_t, True, n)

        a_l = _attn_a_call(qa_l, ka_l, va_l, ka_c, va_c, attn_sink[i], bsz, True)
        n_l = _attn_b_call(qb_l, kb_l, vb_l, kb_c, vb_c, bias_tab, bsz)
        qc_c, kc_c, vc_c = _cprep_call(pc, mlstm_conv_w[i], conv_b, lc)
        qc_l, kc_l, vc_l = _cprep_call(pl_, mlstm_conv_w[i], conv_b, n)
        hc, s_c, m_c = _mlstm_call(qc_c, kc_c, vc_c, _gateprep_call(gc, gate_b, lc), s_zero, m_zero, bsz)
        hl, _, _ = _mlstm_call(qc_l, kc_l, vc_l, _gateprep_call(gl, gate_b, n), s_c, m_c, bsz)
        m_l = _mlstm_out_call(hl, pl_, mlstm_norm_g[i])

        yl = _merge_call(ul, a_l, n_l, m_l, wg, b_gate[i], wb)
        xl = _down_call(yl, w_out, xl, ml[:, 5], 1.0, lead=(i,))
        xl = ffn(xl, ml, 6, norm_g[i, 2], i, 1)
        if not last:
            a_c = _attn_a_call(qa_c, ka_c, va_c, ka_c, va_c, attn_sink[i], bsz, False)
            n_c = _attn_bx_call(qb_c, kb_c, vb_c, bsz)
            mm_c = _mlstm_out_call(hc, pc, mlstm_norm_g[i])
            yc = _merge_call(uc, a_c, n_c, mm_c, wg, b_gate[i], wb)
            xc = _down_call(yc, w_out, xc, mc[:, 5], 1.0, lead=(i,))
            xc = ffn(xc, mc, 6, norm_g[i, 2], i, 1)
    return xl.reshape(bsz, n, d)
```

```python
import functools

import numpy as np
import jax
import jax.numpy as jnp
from jax import lax
from jax.experimental import pallas as pl
from jax.experimental.pallas import tpu as pltpu

F32 = jnp.float32
BF16 = jnp.bfloat16

HEAD_DIM = 128
GRID_W = 64
ROPE_PAIRS = HEAD_DIM // 4
ROPE_THETA = 10000.0
A_HEADS = 8
A_KV_HEADS = 2
A_GROUP = A_HEADS // A_KV_HEADS
A_WINDOW = 128
A_BLOCK = 128
B_HEADS = 8
NA_ROWS = 8
NA_COLS = 16
C_HEADS = 8
C_CONV = 5
BRANCH_W = 8 * HEAD_DIM
N_MOD = 9
EPS = 1e-6
NEG = -1e30
ATTN_SCALE = HEAD_DIM ** -0.5

M_CHUNK = 128
NA_TILE_ROWS = 8
NA_TILE = NA_TILE_ROWS * GRID_W
NA_WIN_ROWS = NA_TILE_ROWS + NA_ROWS
NA_HEADS_PER_STEP = 8
A_QBLOCKS = 2

V7X_VMEM_BYTES = 64 * 1024 * 1024
VMEM_CAP = V7X_VMEM_BYTES - 6 * 1024 * 1024

COL_AQ, COL_BQ, COL_BK, COL_BV, COL_CQ, COL_CK, COL_CV, COL_CO = range(8)
MAIN_W = 8 * BRANCH_W + 2 * A_KV_HEADS * HEAD_DIM
GATE_W = 128


def _pick(n, cands):
    for c in cands:
        if n % c == 0:
            return c
    raise ValueError(f"no tile in {cands} divides {n}")


def _params(sem, block_bytes, temp_bytes=0):
    limit = 2 * block_bytes + temp_bytes + 4 * 1024 * 1024
    return pltpu.CompilerParams(dimension_semantics=sem,
                                vmem_limit_bytes=int(min(max(limit, 16 * 1024 * 1024), VMEM_CAP)))


def _nbytes(shape, dtype):
    return int(np.prod(shape)) * jnp.dtype(dtype).itemsize


def _dot(a, b):
    return jnp.dot(a, b, preferred_element_type=F32)


def _dot_nt(a, b):
    return lax.dot_general(a, b, (((1,), (1,)), ((), ())), preferred_element_type=F32)


def _dot_tn(a, b):
    return lax.dot_general(a, b, (((0,), (0,)), ((), ())), preferred_element_type=F32)


def _silu(x):
    return x * jax.nn.sigmoid(x)


def _mod_kernel(c_ref, w_ref, b_ref, o_ref):
    a = _silu(c_ref[...]).astype(BF16)
    o_ref[0] = _dot(a, w_ref[0].astype(BF16)) + b_ref[0]


def _mod_call(cs, w_mod, b_mod):
    depth, d, nd = w_mod.shape
    tn = _pick(nd, (1024, 512, 256, 128))
    rows = cs.shape[0]
    blocks = _nbytes((d, tn), F32) + _nbytes((rows, d), F32) + _nbytes((rows, tn), F32)
    return pl.pallas_call(
        _mod_kernel,
        out_shape=jax.ShapeDtypeStruct((depth, rows, nd), F32),
        grid=(depth, nd // tn),
        in_specs=[pl.BlockSpec((rows, d), lambda l, j: (0, 0)),
                  pl.BlockSpec((1, d, tn), lambda l, j: (l, 0, j)),
                  pl.BlockSpec((1, 1, tn), lambda l, j: (l, 0, j))],
        out_specs=pl.BlockSpec((1, rows, tn), lambda l, j: (l, 0, j)),
        compiler_params=_params(("parallel", "parallel"), blocks, _nbytes((d, tn), BF16)),
        name="mod_vectors",
    )(cs, w_mod, b_mod.reshape(depth, 1, nd))


def _normmod_kernel(x_ref, g_ref, shift_ref, scale_ref, o_ref):
    x = x_ref[...]
    ms = jnp.mean(x * x, axis=-1, keepdims=True)
    y = x * lax.rsqrt(ms + EPS) * g_ref[...]
    o_ref[...] = (y * (1.0 + scale_ref[0]) + shift_ref[0]).astype(o_ref.dtype)


def _normmod_call(x, g, shift, scale):
    m, d = x.shape
    groups = shift.shape[0]
    tr = _pick(m // groups, (512, 256, 128, 64, 8))
    per = (m // groups) // tr
    blocks = _nbytes((tr, d), F32) + _nbytes((tr, d), BF16) + 3 * _nbytes((1, d), F32)
    return pl.pallas_call(
        _normmod_kernel,
        out_shape=jax.ShapeDtypeStruct((m, d), BF16),
        grid=(m // tr,),
        in_specs=[pl.BlockSpec((tr, d), lambda i: (i, 0)),
                  pl.BlockSpec((1, d), lambda i: (0, 0)),
                  pl.BlockSpec((1, 1, d), lambda i: (i // per, 0, 0)),
                  pl.BlockSpec((1, 1, d), lambda i: (i // per, 0, 0))],
        out_specs=pl.BlockSpec((tr, d), lambda i: (i, 0)),
        compiler_params=_params(("parallel",), blocks, 2 * _nbytes((tr, d), F32)),
        name="norm_modulate",
    )(x, g.reshape(1, d), shift.reshape(groups, 1, d), scale.reshape(groups, 1, d))


def _up_kernel(u_ref, w1_ref, w3_ref, o_ref):
    u = u_ref[...]
    h1 = _dot(u, w1_ref[...].astype(BF16))
    h3 = _dot(u, w3_ref[...].astype(BF16))
    o_ref[...] = (_silu(h1) * h3).astype(o_ref.dtype)


def _up_call(u, w1, w3, layer, sub):
    m, d = u.shape
    f = w1.shape[-1]
    tm = _pick(m, (1024, 512, 256))
    tf = _pick(f, (256, 128))
    wspec = pl.BlockSpec((None, None, d, tf), lambda i, j: (layer, sub, 0, j))
    blocks = _nbytes((tm, d), BF16) + 2 * _nbytes((d, tf), F32) + _nbytes((tm, tf), BF16)
    return pl.pallas_call(
        _up_kernel,
        out_shape=jax.ShapeDtypeStruct((m, f), BF16),
        grid=(m // tm, f // tf),
        in_specs=[pl.BlockSpec((tm, d), lambda i, j: (i, 0)), wspec, wspec],
        out_specs=pl.BlockSpec((tm, tf), lambda i, j: (i, j)),
        compiler_params=_params(("parallel", "parallel"), blocks,
                                2 * _nbytes((d, tf), BF16) + 4 * _nbytes((tm, tf), F32)),
        name="ffn_up",
    )(u, w1, w3)


def _down_kernel(a_ref, w_ref, x_ref, gate_ref, o_ref, *, coef):
    acc = _dot(a_ref[...], w_ref[...].astype(BF16))
    o_ref[...] = x_ref[...] + (coef * gate_ref[0]) * acc


def _down_call(a, w, x, gate, coef, lead=()):
    m, k = a.shape
    d = w.shape[-1]
    groups = gate.shape[0]
    tm = _pick(m // groups, (1024, 512, 256))
    tn = _pick(d, (512, 256, 128) if k <= 4096 else (256, 128))
    per = (m // groups) // tm
    blocks = (_nbytes((tm, k), BF16) + _nbytes((k, tn), w.dtype) + 2 * _nbytes((tm, tn), F32)
              + _nbytes((1, tn), F32))
    temps = 2 * _nbytes((tm, tn), F32) + (_nbytes((k, tn), BF16) if w.dtype != BF16 else 0)
    wspec = pl.BlockSpec((None,) * len(lead) + (k, tn), lambda i, j: tuple(lead) + (0, j))
    return pl.pallas_call(
        functools.partial(_down_kernel, coef=coef),
        out_shape=jax.ShapeDtypeStruct((m, d), F32),
        grid=(m // tm, d // tn),
        in_specs=[pl.BlockSpec((tm, k), lambda i, j: (i, 0)),
                  wspec,
                  pl.BlockSpec((tm, tn), lambda i, j: (i, j)),
                  pl.BlockSpec((1, 1, tn), lambda i, j: (i // per, 0, j))],
        out_specs=pl.BlockSpec((tm, tn), lambda i, j: (i, j)),
        compiler_params=_params(("parallel", "parallel"), blocks, temps),
        name="proj_residual",
    )(a, w, x, gate.reshape(groups, 1, d))


def _matmul_kernel(a_ref, w_ref, o_ref):
    o_ref[...] = _dot(a_ref[...], w_ref[...].astype(BF16)).astype(o_ref.dtype)


IN_TN = 2 * A_KV_HEADS * HEAD_DIM


def _inproj_call(u, w_in, layer):
    m, d = u.shape
    tm = _pick(m, (1024, 512, 256))
    tn = IN_TN
    nblk = MAIN_W // tn
    a_kv = BRANCH_W // tn

    def w_tile(j):
        return jnp.where(j < a_kv, j, jnp.where(j < nblk - 1, j + 1, a_kv))

    blocks = _nbytes((tm, d), BF16) + _nbytes((d, tn), F32) + _nbytes((tm, tn), F32)
    return pl.pallas_call(
        _matmul_kernel,
        out_shape=jax.ShapeDtypeStruct((m, MAIN_W), F32),
        grid=(m // tm, nblk),
        in_specs=[pl.BlockSpec((tm, d), lambda i, j: (i, 0)),
                  pl.BlockSpec((None, d, tn), lambda i, j: (layer, 0, w_tile(j)))],
        out_specs=pl.BlockSpec((tm, tn), lambda i, j: (i, j)),
        compiler_params=_params(("parallel", "parallel"), blocks,
                                _nbytes((d, tn), BF16) + 2 * _nbytes((tm, tn), F32)),
        name="in_proj",
    )(u, w_in)


def _matmul_call(a, w, out_dtype=F32):
    m, k = a.shape
    n = w.shape[1]
    tm = _pick(m, (1024, 512, 256))
    tn = _pick(n, (512, 256, 128))
    blocks = _nbytes((tm, k), BF16) + _nbytes((k, tn), BF16) + _nbytes((tm, tn), out_dtype)
    return pl.pallas_call(
        _matmul_kernel,
        out_shape=jax.ShapeDtypeStruct((m, n), out_dtype),
        grid=(m // tm, n // tn),
        in_specs=[pl.BlockSpec((tm, k), lambda i, j: (i, 0)),
                  pl.BlockSpec((k, tn), lambda i, j: (0, j))],
        out_specs=pl.BlockSpec((tm, tn), lambda i, j: (i, j)),
        compiler_params=_params(("parallel", "parallel"), blocks, _nbytes((tm, tn), F32)),
        name="in_proj",
    )(a, w)


def _merge_kernel(u_ref, oa_ref, ob_ref, oc_ref, wg_ref, bg_ref, wb_ref, y_ref):
    u = u_ref[...]
    acc = None
    for j, br_ref in enumerate((oa_ref, ob_ref, oc_ref)):
        gate = jax.nn.sigmoid(_dot(u, wg_ref[j]) + bg_ref[j])
        term = gate * _dot(br_ref[...], wb_ref[j])
        acc = term if acc is None else acc + term
    y_ref[...] = acc.astype(y_ref.dtype)


def _merge_call(u, oa, ob, oc, wg, bg, wb):
    m, d = u.shape
    bw = oa.shape[1]
    tm = _pick(m, (1024, 512, 256))
    tn = _pick(d, (256, 128))
    blocks = (_nbytes((tm, d), BF16) + 3 * _nbytes((tm, bw), BF16) + 3 * _nbytes((d, tn), BF16)
              + 3 * _nbytes((bw, tn), BF16) + _nbytes((tm, tn), BF16))
    return pl.pallas_call(
        _merge_kernel,
        out_shape=jax.ShapeDtypeStruct((m, d), BF16),
        grid=(m // tm, d // tn),
        in_specs=[pl.BlockSpec((tm, d), lambda i, j: (i, 0)),
                  pl.BlockSpec((tm, bw), lambda i, j: (i, 0)),
                  pl.BlockSpec((tm, bw), lambda i, j: (i, 0)),
                  pl.BlockSpec((tm, bw), lambda i, j: (i, 0)),
                  pl.BlockSpec((3, d, tn), lambda i, j: (0, 0, j)),
                  pl.BlockSpec((3, 1, tn), lambda i, j: (0, 0, j)),
                  pl.BlockSpec((3, bw, tn), lambda i, j: (0, 0, j))],
        out_specs=pl.BlockSpec((tm, tn), lambda i, j: (i, j)),
        compiler_params=_params(("parallel", "parallel"), blocks, 10 * _nbytes((tm, tn), F32)),
        name="gated_merge",
    )(u, oa, ob, oc, wg, bg.reshape(3, 1, d), wb)


def _head_norm(x, g):
    ms = jnp.mean(x * x, axis=-1, keepdims=True)
    return x * lax.rsqrt(ms + EPS) * g


def _rope(y, cos, sin_signed):
    lane = lax.broadcasted_iota(jnp.int32, y.shape, 1)
    partner = jnp.where((lane % 64) < ROPE_PAIRS, pltpu.roll(y, HEAD_DIM - ROPE_PAIRS, axis=1),
                        pltpu.roll(y, ROPE_PAIRS, axis=1))
    return y * cos + partner * sin_signed


def _qkprep_kernel(p4_ref, pa_ref, g_ref, cos_ref, sin_ref,
                   qa_ref, qb_ref, kb_ref, vb_ref, ka_ref, va_ref, *, rope):
    hd = HEAD_DIM
    if rope:
        cos = cos_ref[...]
        sin = sin_ref[...]
    for h in range(A_HEADS):
        y = _head_norm(p4_ref[:, h * hd:(h + 1) * hd], g_ref[0:1, :])
        if rope:
            y = _rope(y, cos, sin)
        qa_ref[:, h * hd:(h + 1) * hd] = y.astype(BF16)
    for h in range(B_HEADS):
        c0 = BRANCH_W + h * hd
        qb_ref[:, h * hd:(h + 1) * hd] = _head_norm(p4_ref[:, c0:c0 + hd], g_ref[2:3, :]).astype(BF16)
        c0 = 2 * BRANCH_W + h * hd
        kb_ref[:, h * hd:(h + 1) * hd] = _head_norm(p4_ref[:, c0:c0 + hd], g_ref[3:4, :]).astype(BF16)
    vb_ref[...] = p4_ref[:, 3 * BRANCH_W:4 * BRANCH_W].astype(BF16)
    for h in range(A_KV_HEADS):
        y = _head_norm(pa_ref[:, h * hd:(h + 1) * hd], g_ref[1:2, :])
        if rope:
            y = _rope(y, cos, sin)
        ka_ref[:, h * hd:(h + 1) * hd] = y.astype(BF16)
    kvw = A_KV_HEADS * hd
    va_ref[...] = pa_ref[:, kvw:2 * kvw].astype(BF16)


def _qkprep_call(p_main, qk_g, cos_t, sin_t, rope, seq):
    m = p_main.shape[0]
    tr = _pick(seq, (256, 128))
    per = seq // tr
    kvw = A_KV_HEADS * HEAD_DIM
    w4 = 4 * BRANCH_W
    blocks = (_nbytes((tr, w4), F32) + _nbytes((tr, 2 * kvw), F32) + 2 * _nbytes((tr, HEAD_DIM), F32)
              + _nbytes((tr, w4), BF16) + _nbytes((tr, 2 * kvw), BF16))
    outs = [jax.ShapeDtypeStruct((m, BRANCH_W), BF16)] * 4 + [jax.ShapeDtypeStruct((m, kvw), BF16)] * 2
    return pl.pallas_call(
        functools.partial(_qkprep_kernel, rope=rope),
        out_shape=outs,
        grid=(m // tr,),
        in_specs=[pl.BlockSpec((tr, w4), lambda i: (i, 0)),
                  pl.BlockSpec((tr, 2 * kvw), lambda i: (i, (8 * BRANCH_W) // (2 * kvw))),
                  pl.BlockSpec((4, HEAD_DIM), lambda i: (0, 0)),
                  pl.BlockSpec((tr, HEAD_DIM), lambda i: (i % per, 0)),
                  pl.BlockSpec((tr, HEAD_DIM), lambda i: (i % per, 0))],
        out_specs=[pl.BlockSpec((tr, BRANCH_W), lambda i: (i, 0))] * 4
                  + [pl.BlockSpec((tr, kvw), lambda i: (i, 0))] * 2,
        compiler_params=_params(("parallel",), blocks, 4 * _nbytes((tr, HEAD_DIM), F32)),
        name="qk_prep",
    )(p_main, p_main, qk_g, cos_t, sin_t)


def _attn_a_kernel(sink_ref, q_ref, *rest, n_tok, lc, window):
    nwin = A_QBLOCKS + 2 if window else 0
    k_refs = rest[:nwin]
    v_refs = rest[nwin:2 * nwin]
    kx_ref, vx_ref, o_ref = rest[2 * nwin:]
    t = pl.program_id(1)
    hd = HEAD_DIM
    blk = A_BLOCK
    gw = A_GROUP * hd
    keys = lc + 3 * blk if window else lc
    for sb in range(A_QBLOCKS):
        rows = slice(sb * blk, (sb + 1) * blk)
        if window:
            i = t * A_QBLOCKS + sb
            row = lax.broadcasted_iota(jnp.int32, (A_GROUP * blk, keys), 0) % blk
            col = lax.broadcasted_iota(jnp.int32, (A_GROUP * blk, keys), 1)
            qpos = i * blk + row
            kpos = (i - 1) * blk + (col - lc)
            band = (jnp.abs(qpos - kpos) <= A_WINDOW) & (kpos >= 0) & (kpos < n_tok)
            visible = (col < lc) | band
        for hk in range(A_KV_HEADS):
            cs = slice(hk * hd, (hk + 1) * hd)
            if window:
                kall = jnp.concatenate([kx_ref[:, cs]] + [r[:, cs] for r in k_refs[sb:sb + 3]], axis=0)
                vall = jnp.concatenate([vx_ref[:, cs]] + [r[:, cs] for r in v_refs[sb:sb + 3]], axis=0)
            else:
                kall = kx_ref[:, cs]
                vall = vx_ref[:, cs]
            q4 = jnp.concatenate([q_ref[rows, hk * gw + g * hd:hk * gw + (g + 1) * hd] for g in range(A_GROUP)],
                                 axis=0)
            s = _dot_nt(q4, kall) * ATTN_SCALE
            if window:
                s = jnp.where(visible, s, NEG)
            snk = jnp.concatenate(
                [jnp.full((blk, 1), sink_ref[hk * A_GROUP + g], F32) for g in range(A_GROUP)], axis=0)
            mx = jnp.maximum(jnp.max(s, axis=-1, keepdims=True), snk)
            e = jnp.exp(s - mx)
            den = jnp.sum(e, axis=-1, keepdims=True) + jnp.exp(snk - mx)
            p = (e * (1.0 / den)).astype(BF16)
            o = _dot(p, vall)
            for g in range(A_GROUP):
                o_ref[rows, hk * gw + g * hd:hk * gw + (g + 1) * hd] = (
                    o[g * blk:(g + 1) * blk].astype(o_ref.dtype))


def _attn_a_call(qa, ka, va, kx, vx, sink, bsz, window):
    n = qa.shape[0] // bsz
    lc = kx.shape[0] // bsz
    blk = A_BLOCK
    nb = n // blk
    qb = A_QBLOCKS
    assert nb % qb == 0
    nt = nb // qb
    kvw = A_KV_HEADS * HEAD_DIM
    qw = A_HEADS * HEAD_DIM
    qspec = pl.BlockSpec((qb * blk, qw), lambda b, t: (b * nt + t, 0))
    xspec = pl.BlockSpec((lc, kvw), lambda b, t: (b, 0))
    sspec = pl.BlockSpec(memory_space=pltpu.SMEM)
    if window:
        def kspec(off):
            return pl.BlockSpec((blk, kvw), lambda b, t: (b * nb + jnp.clip(t * qb + off, 0, nb - 1), 0))

        kspecs = [kspec(off) for off in range(-1, qb + 1)]
        in_specs = [sspec, qspec] + kspecs + kspecs + [xspec, xspec]
        args = (sink, qa) + (ka,) * len(kspecs) + (va,) * len(kspecs) + (kx, vx)
        keys = lc + 3 * blk
    else:
        in_specs = [sspec, qspec, xspec, xspec]
        args = (sink, qa, kx, vx)
        keys = lc
    blocks = 2 * _nbytes((qb * blk, qw), BF16) + 2 * _nbytes((keys + qb * blk, kvw), BF16)
    return pl.pallas_call(
        functools.partial(_attn_a_kernel, n_tok=n, lc=lc, window=window),
        out_shape=jax.ShapeDtypeStruct(qa.shape, BF16),
        grid=(bsz, nt),
        in_specs=in_specs,
        out_specs=qspec,
        compiler_params=_params(("parallel", "parallel"), blocks,
                                10 * qb * _nbytes((A_GROUP * blk, keys), F32)),
        name="windowed_gqa" if window else "context_gqa",
    )(*args)


def _softmax2(s1, s2):
    mx = jnp.maximum(jnp.max(s1, axis=-1, keepdims=True), jnp.max(s2, axis=-1, keepdims=True))
    e1 = jnp.exp(s1 - mx)
    e2 = jnp.exp(s2 - mx)
    inv = 1.0 / (jnp.sum(e1, axis=-1, keepdims=True) + jnp.sum(e2, axis=-1, keepdims=True))
    return (e1 * inv).astype(BF16), (e2 * inv).astype(BF16)


def _attn_b_kernel(q_ref, k0_ref, k1_ref, k2_ref, k3_ref, v0_ref, v1_ref, v2_ref, v3_ref, kx_ref, vx_ref,
                   bias_ref, o_ref):
    hd = HEAD_DIM
    for h in range(NA_HEADS_PER_STEP):
        cs = slice(h * hd, (h + 1) * hd)
        kwin = jnp.concatenate([k0_ref[:, cs], k1_ref[:, cs], k2_ref[:, cs], k3_ref[:, cs]], axis=0)
        vwin = jnp.concatenate([v0_ref[:, cs], v1_ref[:, cs], v2_ref[:, cs], v3_ref[:, cs]], axis=0)
        q = q_ref[:, cs]
        s_nb = _dot_nt(q, kwin) * ATTN_SCALE + bias_ref[h, 0]
        s_cx = _dot_nt(q, kx_ref[:, cs]) * ATTN_SCALE
        p_nb, p_cx = _softmax2(s_nb, s_cx)
        o_ref[:, cs] = (_dot(p_nb, vwin) + _dot(p_cx, vx_ref[:, cs])).astype(o_ref.dtype)


def _attn_bx_kernel(q_ref, kx_ref, vx_ref, o_ref):
    hd = HEAD_DIM
    for h in range(B_HEADS):
        cs = slice(h * hd, (h + 1) * hd)
        s = _dot_nt(q_ref[:, cs], kx_ref[:, cs]) * ATTN_SCALE
        mx = jnp.max(s, axis=-1, keepdims=True)
        e = jnp.exp(s - mx)
        p = (e * (1.0 / jnp.sum(e, axis=-1, keepdims=True))).astype(BF16)
        o_ref[:, cs] = _dot(p, vx_ref[:, cs]).astype(o_ref.dtype)


def _na_bias_kernel(rel_ref, o_ref):
    h = pl.program_id(0)
    pos = pl.program_id(1)
    nrel_r = 2 * NA_ROWS - 1
    nrel_c = 2 * NA_COLS - 1
    qi = lax.broadcasted_iota(jnp.int32, (GRID_W, GRID_W), 0)
    ki = lax.broadcasted_iota(jnp.int32, (GRID_W, GRID_W), 1)
    dc = jnp.clip(ki - qi, -(NA_COLS - 1), NA_COLS - 1) + NA_COLS - 1
    start = jnp.clip(qi - NA_COLS // 2, 0, GRID_W - NA_COLS)
    col_in = (ki >= start) & (ki < start + NA_COLS)
    planes = []
    for dr in range(nrel_r):
        acc = jnp.zeros((GRID_W, GRID_W), F32)
        for dcv in range(nrel_c):
            acc = jnp.where(dc == dcv, rel_ref[(h * nrel_r + dr) * nrel_c + dcv], acc)
        planes.append(jnp.where(col_in, acc, NEG))
    masked = jnp.full((GRID_W, GRID_W), NEG, F32)
    half = NA_ROWS // 2
    first_key = {0: lambda j: max(j, half), 1: lambda j: j, 2: lambda j: min(j, half)}
    for p, lo_of in first_key.items():
        @pl.when(pos == p)
        def _(lo_of=lo_of):
            for j in range(NA_TILE_ROWS):
                lo = lo_of(j)
                for jj in range(NA_WIN_ROWS):
                    blk = planes[jj - j + half - 1] if lo <= jj < lo + NA_ROWS else masked
                    o_ref[0, 0, j * GRID_W:(j + 1) * GRID_W, jj * GRID_W:(jj + 1) * GRID_W] = blk


def _na_bias_table(relpos):
    nh = relpos.shape[0]
    shape = (nh, 3, NA_TILE, NA_WIN_ROWS * GRID_W)
    return pl.pallas_call(
        _na_bias_kernel,
        out_shape=jax.ShapeDtypeStruct(shape, F32),
        grid=(nh, 3),
        in_specs=[pl.BlockSpec(memory_space=pltpu.SMEM)],
        out_specs=pl.BlockSpec((1, 1) + shape[2:], lambda h, p: (h, p, 0, 0)),
        compiler_params=_params(("parallel", "parallel"), _nbytes(shape[2:], F32)),
        name="na_bias_table",
    )(relpos.reshape(-1))


def _attn_b_call(qb, kb, vb, kx, vx, bias, bsz):
    n = qb.shape[0] // bsz
    lc = kx.shape[0] // bsz
    rows = n // GRID_W
    assert rows % NA_TILE_ROWS == 0 and rows >= 2 * NA_TILE_ROWS
    nt = rows // NA_TILE_ROWS
    hp = NA_HEADS_PER_STEP
    hd = hp * HEAD_DIM
    sub = NA_TILE // 2
    qspec = pl.BlockSpec((NA_TILE, hd), lambda b, h, t: (b * nt + t, h))

    def kspec(s):
        return pl.BlockSpec((sub, hd),
                            lambda b, h, t: (jnp.clip(2 * t + s, 0, 2 * nt - 1) + 2 * b * nt, h))

    kspecs = [kspec(s) for s in (-1, 0, 1, 2)]
    xspec = pl.BlockSpec((lc, hd), lambda b, h, t: (b, h))
    bspec = pl.BlockSpec((hp, 1) + bias.shape[2:],
                         lambda b, h, t: (h, jnp.where(t == 0, 0, jnp.where(t == nt - 1, 2, 1)), 0, 0))
    blocks = (2 * _nbytes((NA_TILE, hd), BF16) + 8 * _nbytes((sub, hd), BF16) + 2 * _nbytes((lc, hd), BF16)
              + hp * _nbytes(bias.shape[2:], F32))
    return pl.pallas_call(
        _attn_b_kernel,
        out_shape=jax.ShapeDtypeStruct(qb.shape, BF16),
        grid=(bsz, B_HEADS // hp, nt),
        in_specs=[qspec] + kspecs + kspecs + [xspec, xspec, bspec],
        out_specs=qspec,
        compiler_params=_params(("parallel", "parallel", "parallel"), blocks,
                                5 * hp * _nbytes((NA_TILE, NA_WIN_ROWS * GRID_W + lc), F32)),
        name="neighbourhood_attn",
    )(qb, kb, kb, kb, kb, vb, vb, vb, vb, kx, vx, bias)


def _attn_bx_call(qx, kx, vx, bsz):
    lc = qx.shape[0] // bsz
    w = qx.shape[1]
    spec = pl.BlockSpec((lc, w), lambda b: (b, 0))
    return pl.pallas_call(
        _attn_bx_kernel,
        out_shape=jax.ShapeDtypeStruct(qx.shape, BF16),
        grid=(bsz,),
        in_specs=[spec, spec, spec],
        out_specs=spec,
        compiler_params=_params(("parallel",), 4 * _nbytes((lc, w), BF16), 4 * 1024 * 1024),
        name="context_full_attn",
    )(qx, kx, vx)


def _log_sigmoid(x):
    return -(jnp.maximum(-x, 0.0) + jnp.log1p(jnp.exp(-jnp.abs(x))))


CONV_HALO = 8


def _cprep_kernel(q_ref, k_ref, v_ref, qlo_ref, qhi_ref, klo_ref, khi_ref, cw_ref, cb_ref,
                  qs_ref, kt_ref, vb_ref, ext, *, per):
    pos = pl.program_id(0) % per
    tr = q_ref.shape[0]
    hw = q_ref.shape[1]
    halo = CONV_HALO
    pad = C_CONV // 2

    def conv_silu(x_ref, lo_ref, hi_ref, col0):
        ext[0:halo, :] = jnp.where(pos > 0, lo_ref[...], 0.0)
        ext[halo:halo + tr, :] = x_ref[...]
        ext[halo + tr:2 * halo + tr, :] = jnp.where(pos < per - 1, hi_ref[...], 0.0)
        out = cb_ref[:, col0:col0 + hw]
        for j in range(C_CONV):
            out = out + cw_ref[j:j + 1, col0:col0 + hw] * ext[halo - pad + j:halo - pad + j + tr, :]
        return _silu(out)

    qs_ref[...] = (conv_silu(q_ref, qlo_ref, qhi_ref, 0) * ATTN_SCALE).astype(BF16)
    kc = conv_silu(k_ref, klo_ref, khi_ref, hw)
    for ck in range(tr // M_CHUNK):
        for h in range(C_HEADS):
            blk = kc[ck * M_CHUNK:(ck + 1) * M_CHUNK, h * HEAD_DIM:(h + 1) * HEAD_DIM]
            kt_ref[ck, h * HEAD_DIM:(h + 1) * HEAD_DIM, :] = blk.T.astype(BF16)
    vb_ref[...] = v_ref[...].astype(BF16)


def _cprep_call(p_main, conv_w, conv_b, seq):
    m = p_main.shape[0]
    hw = C_HEADS * HEAD_DIM
    assert M_CHUNK == HEAD_DIM
    tr = _pick(seq, (256, 128))
    per = seq // tr
    halo = CONV_HALO
    hb = tr // halo
    nblk = m // halo
    cpb = tr // M_CHUNK

    def main(col):
        return pl.BlockSpec((tr, hw), lambda i: (i, col))

    def lo(col):
        return pl.BlockSpec((halo, hw), lambda i: (jnp.maximum(i * hb - 1, 0), col))

    def hi(col):
        return pl.BlockSpec((halo, hw), lambda i: (jnp.minimum((i + 1) * hb, nblk - 1), col))

    blocks = (3 * _nbytes((tr, hw), F32) + 4 * _nbytes((halo, hw), F32) + _nbytes((C_CONV + 1, 2 * hw), F32)
              + 3 * _nbytes((tr, hw), BF16))
    scratch = _nbytes((tr + 2 * halo, hw), F32)
    out = jax.ShapeDtypeStruct((m, hw), BF16)
    out_t = jax.ShapeDtypeStruct((m // M_CHUNK, hw, M_CHUNK), BF16)
    ospec = pl.BlockSpec((tr, hw), lambda i: (i, 0))
    tspec = pl.BlockSpec((cpb, hw, M_CHUNK), lambda i: (i, 0, 0))
    return pl.pallas_call(
        functools.partial(_cprep_kernel, per=per),
        out_shape=[out, out_t, out],
        grid=(m // tr,),
        in_specs=[main(COL_CQ), main(COL_CK), main(COL_CV), lo(COL_CQ), hi(COL_CQ), lo(COL_CK), hi(COL_CK),
                  pl.BlockSpec((C_CONV, 2 * hw), lambda i: (0, 0)),
                  pl.BlockSpec((1, 2 * hw), lambda i: (0, 0))],
        out_specs=[ospec, tspec, ospec],
        scratch_shapes=[pltpu.VMEM((tr + 2 * halo, hw), F32)],
        compiler_params=_params(("parallel",), blocks, scratch + 4 * _nbytes((tr, hw), F32)),
        name="mlstm_prep",
    )(p_main, p_main, p_main, p_main, p_main, p_main, p_main, conv_w, conv_b)


GP_PLANES = 6


def _gateprep_kernel(g_ref, gb_ref, o_ref):
    nh = C_HEADS
    L = M_CHUNK
    lane = lax.broadcasted_iota(jnp.int32, (nh, L), 1)
    for ck in range(g_ref.shape[0] // L):
        gt = (g_ref[ck * L:(ck + 1) * L, :] + gb_ref[...]).T
        for d in range(2):
            ig = gt[2 * d * nh:(2 * d + 1) * nh]
            logf = _log_sigmoid(gt[(2 * d + 1) * nh:(2 * d + 2) * nh])
            pre = logf
            sft = 1
            while sft < L:
                pre = pre + jnp.where(lane >= sft, pltpu.roll(pre, sft, axis=1), 0.0)
                sft *= 2
            total = jnp.broadcast_to(pre[:, L - 1:L], (nh, L))
            bcum = pre if d == 0 else total - pre + logf
            r_row = ig - bcum
            rmax = r_row
            sft = 1
            while sft < L:
                if d == 0:
                    rmax = jnp.maximum(rmax, jnp.where(lane >= sft, pltpu.roll(rmax, sft, axis=1), NEG))
                else:
                    rmax = jnp.maximum(rmax, jnp.where(lane < L - sft, pltpu.roll(rmax, L - sft, axis=1), NEG))
                sft *= 2
            r_top = jnp.broadcast_to(rmax[:, L - 1:L] if d == 0 else rmax[:, 0:1], (nh, L))
            w_row = jnp.exp(r_row - r_top)
            for k, plane in enumerate((r_row, rmax, bcum, total, w_row, r_top)):
                o_ref[ck, d, k * nh:(k + 1) * nh, :] = plane


def _gateprep_call(gates, gate_b, seq):
    m = gates.shape[0]
    L = M_CHUNK
    tr = _pick(seq, (1024, 512, 256, 128))
    shape = (m // L, 2, GP_PLANES * C_HEADS, L)
    blocks = _nbytes((tr, GATE_W), F32) + _nbytes((tr // L,) + shape[1:], F32)
    return pl.pallas_call(
        _gateprep_kernel,
        out_shape=jax.ShapeDtypeStruct(shape, F32),
        grid=(m // tr,),
        in_specs=[pl.BlockSpec((tr, GATE_W), lambda i: (i, 0)),
                  pl.BlockSpec((1, GATE_W), lambda i: (0, 0))],
        out_specs=pl.BlockSpec((tr // L,) + shape[1:], lambda i: (i, 0, 0, 0)),
        compiler_params=_params(("parallel",), blocks, 2 * 1024 * 1024),
        name="mlstm_gate_prep",
    )(gates, gate_b)


def _mlstm_kernel(qf_ref, ktf_ref, vf_ref, gpf_ref, qb_ref, ktb_ref, vb_ref, gpb_ref, s0_ref, m0_ref,
                  hf_ref, hb_ref, sf_ref, mf_ref, cst, mst, *, nc):
    c = pl.program_id(1)
    L = M_CHUNK
    hd = HEAD_DIM
    nh = C_HEADS

    @pl.when(c == 0)
    def _():
        cst[...] = s0_ref[...]
        mst[...] = m0_ref[...]

    ii = lax.broadcasted_iota(jnp.int32, (L, L), 0)
    jj = lax.broadcasted_iota(jnp.int32, (L, L), 1)
    ones = jnp.ones((L, hd), BF16)
    dirs = ((qf_ref, ktf_ref, vf_ref, gpf_ref, hf_ref, ii >= jj), (qb_ref, ktb_ref, vb_ref, gpb_ref, hb_ref, ii <= jj))
    for d, (q_ref, kt_ref, v_ref, gp_ref, h_ref, causal) in enumerate(dirs):
        r_row, rmax, bcum, total, w_row, r_top = (gp_ref[0, 0, k * nh:(k + 1) * nh, :] for k in range(GP_PLANES))
        m0 = mst[d]
        big_m = jnp.maximum(rmax, m0)
        si_row = jnp.exp(m0 - big_m)
        nrm_row = jnp.exp(-(bcum + big_m))
        m_loc = total + r_top
        m_new = jnp.maximum(total + m0, m_loc)
        s_prev = jnp.exp(total + m0 - m_new)
        s_loc = jnp.exp(m_loc - m_new)
        mst[d] = m_new
        xt = jnp.concatenate([big_m, si_row, nrm_row, jnp.zeros((L - 3 * nh, L), F32)], axis=0).T
        for h in range(nh):
            cs = slice(h * hd, (h + 1) * hd)
            qs = q_ref[:, cs]
            kt = kt_ref[0, cs, :]
            vext = jnp.concatenate([v_ref[:, cs], ones], axis=1)
            cext = cst[d * nh + h]
            expo = jnp.where(causal, r_row[h:h + 1, :] - xt[:, h:h + 1], NEG)
            sw = jnp.exp(expo) * _dot(qs, kt)
            si = jnp.broadcast_to(xt[:, nh + h:nh + h + 1], (L, 2 * hd))
            both = _dot(sw.astype(BF16), vext) + si * _dot(qs, cext.astype(BF16))
            nrm = jnp.broadcast_to(xt[:, 2 * nh + h:2 * nh + h + 1], (L, hd))
            h_ref[:, cs] = both[:, 0:hd] / jnp.maximum(jnp.abs(both[:, hd:2 * hd]), nrm)
            kwt = (kt.astype(F32) * w_row[h:h + 1, :]).astype(BF16)
            sp = jnp.concatenate([s_prev[h:h + 1, :], s_prev[h:h + 1, :]], axis=1)
            sl = jnp.concatenate([s_loc[h:h + 1, :], s_loc[h:h + 1, :]], axis=1)
            cst[d * nh + h] = sp * cext + sl * _dot(kwt, vext)

    @pl.when(c == nc - 1)
    def _():
        sf_ref[...] = cst[...]
        mf_ref[...] = mst[...]


def _mlstm_call(qs, kt, vb, gplanes, s0, m0, bsz):
    t = qs.shape[0] // bsz
    L = M_CHUNK
    nc = t // L
    nh = C_HEADS
    hw = nh * HEAD_DIM

    def cidx(d, b, c):
        return b * nc + (c if d == 0 else nc - 1 - c)

    def dir_specs(d):
        return [pl.BlockSpec((L, hw), lambda b, c: (cidx(d, b, c), 0)),
                pl.BlockSpec((1, hw, L), lambda b, c: (cidx(d, b, c), 0, 0)),
                pl.BlockSpec((L, hw), lambda b, c: (cidx(d, b, c), 0)),
                pl.BlockSpec((1, 1, GP_PLANES * nh, L), lambda b, c: (cidx(d, b, c), d, 0, 0))]

    sspec = pl.BlockSpec((2 * nh, HEAD_DIM, 2 * HEAD_DIM), lambda b, c: (b, 0, 0))
    mspec = pl.BlockSpec((2, nh, 128), lambda b, c: (b, 0, 0))
    blocks = (6 * _nbytes((L, hw), BF16) + 2 * _nbytes((GP_PLANES * nh, L), F32)
              + 4 * _nbytes((nh, HEAD_DIM, 2 * HEAD_DIM), F32) + 2 * _nbytes((L, hw), F32))
    scratch = 2 * _nbytes((nh, HEAD_DIM, 2 * HEAD_DIM), F32)
    hshape = jax.ShapeDtypeStruct((bsz * t, hw), F32)
    return pl.pallas_call(
        functools.partial(_mlstm_kernel, nc=nc),
        out_shape=[hshape, hshape, jax.ShapeDtypeStruct(s0.shape, F32), jax.ShapeDtypeStruct(m0.shape, F32)],
        grid=(bsz, nc),
        in_specs=dir_specs(0) + dir_specs(1) + [sspec, mspec],
        out_specs=[pl.BlockSpec((L, hw), lambda b, c: (cidx(0, b, c), 0)),
                   pl.BlockSpec((L, hw), lambda b, c: (cidx(1, b, c), 0)), sspec, mspec],
        scratch_shapes=[pltpu.VMEM((2 * nh, HEAD_DIM, 2 * HEAD_DIM), F32), pltpu.VMEM((2, nh, 128), F32)],
        compiler_params=_params(("parallel", "arbitrary"), blocks, scratch + 16 * 1024 * 1024),
        name="mlstm_scan",
    )(qs, kt, vb, gplanes, qs, kt, vb, gplanes, s0, m0)


def _mlstm_out_kernel(hf_ref, hb_ref, o_ref, g_ref, y_ref):
    hd = HEAD_DIM
    for h in range(C_HEADS):
        cs = slice(h * hd, (h + 1) * hd)
        hn = _head_norm(hf_ref[:, cs] + hb_ref[:, cs], g_ref[:, cs])
        y_ref[:, cs] = (hn * jax.nn.sigmoid(o_ref[:, cs])).astype(y_ref.dtype)


def _mlstm_out_call(hf, hb, p_main, norm_g):
    m, hw = hf.shape
    tr = _pick(m, (256, 128))
    blocks = 3 * _nbytes((tr, hw), F32) + _nbytes((tr, hw), BF16)
    hspec = pl.BlockSpec((tr, hw), lambda i: (i, 0))
    return pl.pallas_call(
        _mlstm_out_kernel,
        out_shape=jax.ShapeDtypeStruct((m, hw), BF16),
        grid=(m // tr,),
        in_specs=[hspec, hspec,
                  pl.BlockSpec((tr, hw), lambda i: (i, COL_CO)),
                  pl.BlockSpec((1, hw), lambda i: (0, 0))],
        out_specs=hspec,
        compiler_params=_params(("parallel",), blocks, 2 * 1024 * 1024),
        name="mlstm_out",
    )(hf, hb, p_main, norm_g.reshape(1, hw))


def _rope_lane_tables(n):
    t = jnp.arange(n)
    inv = ROPE_THETA ** (-jnp.arange(ROPE_PAIRS, dtype=F32) / ROPE_PAIRS)
    row = (t // GRID_W).astype(F32)[:, None] * inv
    col = (t % GRID_W).astype(F32)[:, None] * inv
    cos = jnp.concatenate([jnp.cos(row), jnp.cos(row), jnp.cos(col), jnp.cos(col)], axis=1)
    sin = jnp.concatenate([-jnp.sin(row), jnp.sin(row), -jnp.sin(col), jnp.sin(col)], axis=1)
    return cos, sin


def _gate_w(w_in):
    return jnp.pad(w_in[:, MAIN_W:], ((0, 0), (0, GATE_W - 4 * C_HEADS))).astype(BF16)


def kernel(x, c, ctx, c_ctx, w_mod, b_mod, norm_g, ffn_w1, ffn_w3, ffn_w2, w_in, qk_g, attn_sink, na_relpos,
           mlstm_conv_w, mlstm_conv_b, mlstm_gate_b, mlstm_norm_g, w_gate, b_gate, w_branch, w_out):
    bsz, n, d = x.shape
    lc = ctx.shape[1]
    depth = w_mod.shape[0]
    assert n % GRID_W == 0 and n % M_CHUNK == 0 and lc % M_CHUNK == 0 and n % A_BLOCK == 0

    cs = jnp.zeros((8, d), F32).at[:bsz].set(c).at[bsz].set(c_ctx)
    mods = _mod_call(cs, w_mod, b_mod).reshape(depth, 8, N_MOD, d)
    cos_t, sin_t = _rope_lane_tables(n)

    xl = x.reshape(bsz * n, d)
    xc = ctx.reshape(bsz * lc, d)
    nh = C_HEADS
    s_zero = jnp.zeros((bsz * 2 * nh, HEAD_DIM, 2 * HEAD_DIM), F32)
    m_zero = jnp.zeros((bsz * 2, nh, 128), F32)

    def ffn(xs, md, k0, g, layer, sub):
        u = _normmod_call(xs, g, md[:, k0], md[:, k0 + 1])
        h = _up_call(u, ffn_w1, ffn_w3, layer, sub)
        return _down_call(h, ffn_w2, xs, md[:, k0 + 2], 0.5, lead=(layer, sub))

    for i in range(depth):
        last = i == depth - 1
        ml = mods[i, :bsz]
        mc = mods[i, bsz:bsz + 1]
        w_gates = _gate_w(w_in[i])
        wg = w_gate[i].astype(BF16)
        wb = w_branch[i].astype(BF16)
        gate_b = jnp.pad(mlstm_gate_b[i].reshape(1, 4 * nh), ((0, 0), (0, GATE_W - 4 * nh)))
        conv_b = mlstm_conv_b[i].reshape(1, -1)
        bias_tab = _na_bias_table(na_relpos[i])

        xc = ffn(xc, mc, 0, norm_g[i, 0], i, 0)
        xl = ffn(xl, ml, 0, norm_g[i, 0], i, 0)

        uc = _normmod_call(xc, norm_g[i, 1], mc[:, 3], mc[:, 4])
        ul = _normmod_call(xl, norm_g[i, 1], ml[:, 3], ml[:, 4])
        pc = _inproj_call(uc, w_in, i)
        pl_ = _inproj_call(ul, w_in, i)
        gc = _matmul_call(uc, w_gates)
        gl = _matmul_call(ul, w_gates)

        qa_c, qb_c, kb_c, vb_c, ka_c, va_c = _qkprep_call(pc, qk_g[i], cos_t, sin_t, False, lc)
        qa_l, qb_l, kb_l, vb_l, ka_l, va_l = _qkprep_call(pl_, qk_g[i], cos_t, sin_t, True, n)

        a_l = _attn_a_call(qa_l, ka_l, va_l, ka_c, va_c, attn_sink[i], bsz, True)
        n_l = _attn_b_call(qb_l, kb_l, vb_l, kb_c, vb_c, bias_tab, bsz)
        qc_c, kc_c, vc_c = _cprep_call(pc, mlstm_conv_w[i], conv_b, lc)
        qc_l, kc_l, vc_l = _cprep_call(pl_, mlstm_conv_w[i], conv_b, n)
        hc_f, hc_b, s_c, m_c = _mlstm_call(qc_c, kc_c, vc_c, _gateprep_call(gc, gate_b, lc), s_zero, m_zero, bsz)
        hl_f, hl_b, _, _ = _mlstm_call(qc_l, kc_l, vc_l, _gateprep_call(gl, gate_b, n), s_c, m_c, bsz)
        m_l = _mlstm_out_call(hl_f, hl_b, pl_, mlstm_norm_g[i])

        yl = _merge_call(ul, a_l, n_l, m_l, wg, b_gate[i], wb)
        xl = _down_call(yl, w_out, xl, ml[:, 5], 1.0, lead=(i,))
        xl = ffn(xl, ml, 6, norm_g[i, 2], i, 1)
        if not last:
            a_c = _attn_a_call(qa_c, ka_c, va_c, ka_c, va_c, attn_sink[i], bsz, False)
            n_c = _attn_bx_call(qb_c, kb_c, vb_c, bsz)
            mm_c = _mlstm_out_call(hc_f, hc_b, pc, mlstm_norm_g[i])
            yc = _merge_call(uc, a_c, n_c, mm_c, wg, b_gate[i], wb)
            xc = _down_call(yc, w_out, xc, mc[:, 5], 1.0, lead=(i,))
            xc = ffn(xc, mc, 6, norm_g[i, 2], i, 1)
    return xl.reshape(bsz, n, d)
```

```python
import functools

import numpy as np
import jax
import jax.numpy as jnp
from jax import lax
from jax.experimental import pallas as pl
from jax.experimental.pallas import tpu as pltpu

F32 = jnp.float32
BF16 = jnp.bfloat16

HEAD_DIM = 128
GRID_W = 64
ROPE_PAIRS = HEAD_DIM // 4
ROPE_THETA = 10000.0
A_HEADS = 8
A_KV_HEADS = 2
A_GROUP = A_HEADS // A_KV_HEADS
A_WINDOW = 128
A_BLOCK = 128
B_HEADS = 8
NA_ROWS = 8
NA_COLS = 16
C_HEADS = 8
C_CONV = 5
BRANCH_W = 8 * HEAD_DIM
N_MOD = 9
EPS = 1e-6
NEG = -1e30
ATTN_SCALE = HEAD_DIM ** -0.5
LOG2E = 1.4426950408889634
QK_SCALE_LOG2 = ATTN_SCALE * LOG2E

M_CHUNK = 128
NA_TILE_ROWS = 8
NA_TILE = NA_TILE_ROWS * GRID_W
NA_WIN_ROWS = NA_TILE_ROWS + NA_ROWS
NA_HEADS_PER_STEP = 4
A_QBLOCKS = 2

V7X_VMEM_BYTES = 64 * 1024 * 1024
VMEM_CAP = V7X_VMEM_BYTES - 6 * 1024 * 1024

COL_AQ, COL_BQ, COL_BK, COL_BV, COL_CQ, COL_CK, COL_CV, COL_CO = range(8)
MAIN_W = 8 * BRANCH_W + 2 * A_KV_HEADS * HEAD_DIM
GATE_W = 128


def _pick(n, cands):
    for c in cands:
        if n % c == 0:
            return c
    raise ValueError(f"no tile in {cands} divides {n}")


def _params(sem, block_bytes, temp_bytes=0):
    limit = 2 * block_bytes + temp_bytes + 4 * 1024 * 1024
    return pltpu.CompilerParams(dimension_semantics=sem,
                                vmem_limit_bytes=int(min(max(limit, 16 * 1024 * 1024), VMEM_CAP)))


def _nbytes(shape, dtype):
    return int(np.prod(shape)) * jnp.dtype(dtype).itemsize


def _dot(a, b):
    return jnp.dot(a, b, preferred_element_type=F32)


def _dot_nt(a, b):
    return lax.dot_general(a, b, (((1,), (1,)), ((), ())), preferred_element_type=F32)


def _dot_tn(a, b):
    return lax.dot_general(a, b, (((0,), (0,)), ((), ())), preferred_element_type=F32)


def _silu(x):
    return x * jax.nn.sigmoid(x)


def _mod_kernel(c_ref, w_ref, b_ref, o_ref):
    a = _silu(c_ref[...]).astype(BF16)
    o_ref[0] = _dot(a, w_ref[0].astype(BF16)) + b_ref[0]


def _mod_call(cs, w_mod, b_mod):
    depth, d, nd = w_mod.shape
    tn = _pick(nd, (1024, 512, 256, 128))
    rows = cs.shape[0]
    blocks = _nbytes((d, tn), F32) + _nbytes((rows, d), F32) + _nbytes((rows, tn), F32)
    return pl.pallas_call(
        _mod_kernel,
        out_shape=jax.ShapeDtypeStruct((depth, rows, nd), F32),
        grid=(depth, nd // tn),
        in_specs=[pl.BlockSpec((rows, d), lambda l, j: (0, 0)),
                  pl.BlockSpec((1, d, tn), lambda l, j: (l, 0, j)),
                  pl.BlockSpec((1, 1, tn), lambda l, j: (l, 0, j))],
        out_specs=pl.BlockSpec((1, rows, tn), lambda l, j: (l, 0, j)),
        compiler_params=_params(("parallel", "parallel"), blocks, _nbytes((d, tn), BF16)),
        name="mod_vectors",
    )(cs, w_mod, b_mod.reshape(depth, 1, nd))


def _normmod_kernel(x_ref, g_ref, shift_ref, scale_ref, o_ref):
    x = x_ref[...]
    ms = jnp.mean(x * x, axis=-1, keepdims=True)
    y = x * lax.rsqrt(ms + EPS) * g_ref[...]
    o_ref[...] = (y * (1.0 + scale_ref[0]) + shift_ref[0]).astype(o_ref.dtype)


def _normmod_call(x, g, shift, scale):
    m, d = x.shape
    groups = shift.shape[0]
    tr = _pick(m // groups, (512, 256, 128, 64, 8))
    per = (m // groups) // tr
    blocks = _nbytes((tr, d), F32) + _nbytes((tr, d), BF16) + 3 * _nbytes((1, d), F32)
    return pl.pallas_call(
        _normmod_kernel,
        out_shape=jax.ShapeDtypeStruct((m, d), BF16),
        grid=(m // tr,),
        in_specs=[pl.BlockSpec((tr, d), lambda i: (i, 0)),
                  pl.BlockSpec((1, d), lambda i: (0, 0)),
                  pl.BlockSpec((1, 1, d), lambda i: (i // per, 0, 0)),
                  pl.BlockSpec((1, 1, d), lambda i: (i // per, 0, 0))],
        out_specs=pl.BlockSpec((tr, d), lambda i: (i, 0)),
        compiler_params=_params(("parallel",), blocks, 2 * _nbytes((tr, d), F32)),
        name="norm_modulate",
    )(x, g.reshape(1, d), shift.reshape(groups, 1, d), scale.reshape(groups, 1, d))


def _up_kernel(u_ref, w1_ref, w3_ref, o_ref):
    u = u_ref[...]
    h1 = _dot(u, w1_ref[...].astype(BF16))
    h3 = _dot(u, w3_ref[...].astype(BF16))
    o_ref[...] = (_silu(h1) * h3).astype(o_ref.dtype)


def _up_call(u, w1, w3, layer, sub):
    m, d = u.shape
    f = w1.shape[-1]
    tm = _pick(m, (1024, 512, 256))
    tf = _pick(f, (256, 128))
    wspec = pl.BlockSpec((None, None, d, tf), lambda i, j: (layer, sub, 0, j))
    blocks = _nbytes((tm, d), BF16) + 2 * _nbytes((d, tf), F32) + _nbytes((tm, tf), BF16)
    return pl.pallas_call(
        _up_kernel,
        out_shape=jax.ShapeDtypeStruct((m, f), BF16),
        grid=(m // tm, f // tf),
        in_specs=[pl.BlockSpec((tm, d), lambda i, j: (i, 0)), wspec, wspec],
        out_specs=pl.BlockSpec((tm, tf), lambda i, j: (i, j)),
        compiler_params=_params(("parallel", "parallel"), blocks,
                                2 * _nbytes((d, tf), BF16) + 4 * _nbytes((tm, tf), F32)),
        name="ffn_up",
    )(u, w1, w3)


def _down_kernel(a_ref, w_ref, x_ref, gate_ref, o_ref, *, coef):
    acc = _dot(a_ref[...], w_ref[...].astype(BF16))
    o_ref[...] = x_ref[...] + (coef * gate_ref[0]) * acc


def _down_call(a, w, x, gate, coef, lead=()):
    m, k = a.shape
    d = w.shape[-1]
    groups = gate.shape[0]
    tm = _pick(m // groups, (1024, 512, 256))
    tn = _pick(d, (512, 256, 128) if k <= 4096 else (256, 128))
    per = (m // groups) // tm
    blocks = (_nbytes((tm, k), BF16) + _nbytes((k, tn), w.dtype) + 2 * _nbytes((tm, tn), F32)
              + _nbytes((1, tn), F32))
    temps = 2 * _nbytes((tm, tn), F32) + (_nbytes((k, tn), BF16) if w.dtype != BF16 else 0)
    wspec = pl.BlockSpec((None,) * len(lead) + (k, tn), lambda i, j: tuple(lead) + (0, j))
    return pl.pallas_call(
        functools.partial(_down_kernel, coef=coef),
        out_shape=jax.ShapeDtypeStruct((m, d), F32),
        grid=(m // tm, d // tn),
        in_specs=[pl.BlockSpec((tm, k), lambda i, j: (i, 0)),
                  wspec,
                  pl.BlockSpec((tm, tn), lambda i, j: (i, j)),
                  pl.BlockSpec((1, 1, tn), lambda i, j: (i // per, 0, j))],
        out_specs=pl.BlockSpec((tm, tn), lambda i, j: (i, j)),
        compiler_params=_params(("parallel", "parallel"), blocks, temps),
        name="proj_residual",
    )(a, w, x, gate.reshape(groups, 1, d))


def _matmul_kernel(a_ref, w_ref, o_ref):
    o_ref[...] = _dot(a_ref[...], w_ref[...].astype(BF16)).astype(o_ref.dtype)


IN_TN = 2 * A_KV_HEADS * HEAD_DIM


def _inproj_call(u, w_main, layer):
    m, d = u.shape
    tm = _pick(m, (1024, 512, 256))
    tn = IN_TN
    blocks = _nbytes((tm, d), BF16) + _nbytes((d, tn), BF16) + _nbytes((tm, tn), F32)
    return pl.pallas_call(
        _matmul_kernel,
        out_shape=jax.ShapeDtypeStruct((m, MAIN_W), F32),
        grid=(m // tm, MAIN_W // tn),
        in_specs=[pl.BlockSpec((tm, d), lambda i, j: (i, 0)),
                  pl.BlockSpec((None, d, tn), lambda i, j: (layer, 0, j))],
        out_specs=pl.BlockSpec((tm, tn), lambda i, j: (i, j)),
        compiler_params=_params(("parallel", "parallel"), blocks, 2 * _nbytes((tm, tn), F32)),
        name="in_proj",
    )(u, w_main)


def _matmul_call(a, w, out_dtype=F32):
    m, k = a.shape
    n = w.shape[1]
    tm = _pick(m, (1024, 512, 256))
    tn = _pick(n, (512, 256, 128))
    blocks = _nbytes((tm, k), BF16) + _nbytes((k, tn), BF16) + _nbytes((tm, tn), out_dtype)
    return pl.pallas_call(
        _matmul_kernel,
        out_shape=jax.ShapeDtypeStruct((m, n), out_dtype),
        grid=(m // tm, n // tn),
        in_specs=[pl.BlockSpec((tm, k), lambda i, j: (i, 0)),
                  pl.BlockSpec((k, tn), lambda i, j: (0, j))],
        out_specs=pl.BlockSpec((tm, tn), lambda i, j: (i, j)),
        compiler_params=_params(("parallel", "parallel"), blocks, _nbytes((tm, tn), F32)),
        name="in_proj",
    )(a, w)


def _merge_kernel(u_ref, oa_ref, ob_ref, oc_ref, wg_ref, bg_ref, wb_ref, y_ref):
    u = u_ref[...]
    acc = None
    for j, br_ref in enumerate((oa_ref, ob_ref, oc_ref)):
        gate = jax.nn.sigmoid(_dot(u, wg_ref[j]) + bg_ref[j])
        term = gate * _dot(br_ref[...], wb_ref[j])
        acc = term if acc is None else acc + term
    y_ref[...] = acc.astype(y_ref.dtype)


def _merge_call(u, oa, ob, oc, wg, bg, wb, layer):
    m, d = u.shape
    bw = oa.shape[1]
    tm = _pick(m, (1024, 512, 256))
    tn = _pick(d, (256, 128))
    blocks = (_nbytes((tm, d), BF16) + 3 * _nbytes((tm, bw), BF16) + 3 * _nbytes((d, tn), BF16)
              + 3 * _nbytes((bw, tn), BF16) + _nbytes((tm, tn), BF16))
    return pl.pallas_call(
        _merge_kernel,
        out_shape=jax.ShapeDtypeStruct((m, d), BF16),
        grid=(m // tm, d // tn),
        in_specs=[pl.BlockSpec((tm, d), lambda i, j: (i, 0)),
                  pl.BlockSpec((tm, bw), lambda i, j: (i, 0)),
                  pl.BlockSpec((tm, bw), lambda i, j: (i, 0)),
                  pl.BlockSpec((tm, bw), lambda i, j: (i, 0)),
                  pl.BlockSpec((None, 3, d, tn), lambda i, j: (layer, 0, 0, j)),
                  pl.BlockSpec((None, 3, 1, tn), lambda i, j: (layer, 0, 0, j)),
                  pl.BlockSpec((None, 3, bw, tn), lambda i, j: (layer, 0, 0, j))],
        out_specs=pl.BlockSpec((tm, tn), lambda i, j: (i, j)),
        compiler_params=_params(("parallel", "parallel"), blocks, 10 * _nbytes((tm, tn), F32)),
        name="gated_merge",
    )(u, oa, ob, oc, wg, bg.reshape(bg.shape[0], 3, 1, d), wb)


def _head_norm(x, g):
    ms = jnp.mean(x * x, axis=-1, keepdims=True)
    return x * lax.rsqrt(ms + EPS) * g


def _rope(y, cos, sin_signed):
    lane = lax.broadcasted_iota(jnp.int32, y.shape, 1)
    partner = jnp.where((lane % 64) < ROPE_PAIRS, pltpu.roll(y, HEAD_DIM - ROPE_PAIRS, axis=1),
                        pltpu.roll(y, ROPE_PAIRS, axis=1))
    return y * cos + partner * sin_signed


def _qkprep_kernel(p4_ref, pa_ref, g_ref, cos_ref, sin_ref,
                   qa_ref, qb_ref, kb_ref, vb_ref, ka_ref, va_ref, *, rope):
    hd = HEAD_DIM
    if rope:
        cos = cos_ref[...]
        sin = sin_ref[...]
    for h in range(A_HEADS):
        y = _head_norm(p4_ref[:, h * hd:(h + 1) * hd], g_ref[0:1, :])
        if rope:
            y = _rope(y, cos, sin)
        qa_ref[:, h * hd:(h + 1) * hd] = y.astype(BF16)
    for h in range(B_HEADS):
        c0 = BRANCH_W + h * hd
        qb_ref[:, h * hd:(h + 1) * hd] = _head_norm(p4_ref[:, c0:c0 + hd], g_ref[2:3, :]).astype(BF16)
        c0 = 2 * BRANCH_W + h * hd
        kb_ref[:, h * hd:(h + 1) * hd] = _head_norm(p4_ref[:, c0:c0 + hd], g_ref[3:4, :]).astype(BF16)
    vb_ref[...] = p4_ref[:, 3 * BRANCH_W:4 * BRANCH_W].astype(BF16)
    for h in range(A_KV_HEADS):
        y = _head_norm(pa_ref[:, h * hd:(h + 1) * hd], g_ref[1:2, :])
        if rope:
            y = _rope(y, cos, sin)
        ka_ref[:, h * hd:(h + 1) * hd] = y.astype(BF16)
    kvw = A_KV_HEADS * hd
    va_ref[...] = pa_ref[:, kvw:2 * kvw].astype(BF16)


def _qkprep_call(p_main, qk_g, cos_t, sin_t, rope, seq):
    m = p_main.shape[0]
    tr = _pick(seq, (256, 128))
    per = seq // tr
    kvw = A_KV_HEADS * HEAD_DIM
    w4 = 4 * BRANCH_W
    blocks = (_nbytes((tr, w4), F32) + _nbytes((tr, 2 * kvw), F32) + 2 * _nbytes((tr, HEAD_DIM), F32)
              + _nbytes((tr, w4), BF16) + _nbytes((tr, 2 * kvw), BF16))
    outs = [jax.ShapeDtypeStruct((m, BRANCH_W), BF16)] * 4 + [jax.ShapeDtypeStruct((m, kvw), BF16)] * 2
    return pl.pallas_call(
        functools.partial(_qkprep_kernel, rope=rope),
        out_shape=outs,
        grid=(m // tr,),
        in_specs=[pl.BlockSpec((tr, w4), lambda i: (i, 0)),
                  pl.BlockSpec((tr, 2 * kvw), lambda i: (i, (8 * BRANCH_W) // (2 * kvw))),
                  pl.BlockSpec((4, HEAD_DIM), lambda i: (0, 0)),
                  pl.BlockSpec((tr, HEAD_DIM), lambda i: (i % per, 0)),
                  pl.BlockSpec((tr, HEAD_DIM), lambda i: (i % per, 0))],
        out_specs=[pl.BlockSpec((tr, BRANCH_W), lambda i: (i, 0))] * 4
                  + [pl.BlockSpec((tr, kvw), lambda i: (i, 0))] * 2,
        compiler_params=_params(("parallel",), blocks, 4 * _nbytes((tr, HEAD_DIM), F32)),
        name="qk_prep",
    )(p_main, p_main, qk_g, cos_t, sin_t)


def _attn_a_kernel(sink_ref, q_ref, *rest, n_tok, lc, window):
    nwin = A_QBLOCKS + 2 if window else 0
    k_refs = rest[:nwin]
    v_refs = rest[nwin:2 * nwin]
    kx_ref, vx_ref, o_ref = rest[2 * nwin:]
    t = pl.program_id(1)
    hd = HEAD_DIM
    blk = A_BLOCK
    gw = A_GROUP * hd
    keys = lc + 3 * blk if window else lc
    for sb in range(A_QBLOCKS):
        rows = slice(sb * blk, (sb + 1) * blk)
        if window:
            i = t * A_QBLOCKS + sb
            row = lax.broadcasted_iota(jnp.int32, (A_GROUP * blk, keys), 0) % blk
            col = lax.broadcasted_iota(jnp.int32, (A_GROUP * blk, keys), 1)
            qpos = i * blk + row
            kpos = (i - 1) * blk + (col - lc)
            band = (jnp.abs(qpos - kpos) <= A_WINDOW) & (kpos >= 0) & (kpos < n_tok)
            visible = (col < lc) | band
        for hk in range(A_KV_HEADS):
            cs = slice(hk * hd, (hk + 1) * hd)
            if window:
                kall = jnp.concatenate([kx_ref[:, cs]] + [r[:, cs] for r in k_refs[sb:sb + 3]], axis=0)
                vall = jnp.concatenate([vx_ref[:, cs]] + [r[:, cs] for r in v_refs[sb:sb + 3]], axis=0)
            else:
                kall = kx_ref[:, cs]
                vall = vx_ref[:, cs]
            q4 = jnp.concatenate([q_ref[rows, hk * gw + g * hd:hk * gw + (g + 1) * hd] for g in range(A_GROUP)],
                                 axis=0)
            s = _dot_nt(q4, kall) * QK_SCALE_LOG2
            if window:
                s = jnp.where(visible, s, NEG)
            snk = jnp.concatenate(
                [jnp.full((blk, 1), sink_ref[hk * A_GROUP + g] * LOG2E, F32) for g in range(A_GROUP)], axis=0)
            mx = jnp.maximum(jnp.max(s, axis=-1, keepdims=True), snk)
            e = jnp.exp2(s - mx)
            den = jnp.sum(e, axis=-1, keepdims=True) + jnp.exp2(snk - mx)
            o = _dot(e.astype(BF16), vall) * (1.0 / den)
            for g in range(A_GROUP):
                o_ref[rows, hk * gw + g * hd:hk * gw + (g + 1) * hd] = (
                    o[g * blk:(g + 1) * blk].astype(o_ref.dtype))


def _attn_a_call(qa, ka, va, kx, vx, sink, bsz, window):
    n = qa.shape[0] // bsz
    lc = kx.shape[0] // bsz
    blk = A_BLOCK
    nb = n // blk
    qb = A_QBLOCKS
    assert nb % qb == 0
    nt = nb // qb
    kvw = A_KV_HEADS * HEAD_DIM
    qw = A_HEADS * HEAD_DIM
    qspec = pl.BlockSpec((qb * blk, qw), lambda b, t: (b * nt + t, 0))
    xspec = pl.BlockSpec((lc, kvw), lambda b, t: (b, 0))
    sspec = pl.BlockSpec(memory_space=pltpu.SMEM)
    if window:
        def kspec(off):
            return pl.BlockSpec((blk, kvw), lambda b, t: (b * nb + jnp.clip(t * qb + off, 0, nb - 1), 0))

        kspecs = [kspec(off) for off in range(-1, qb + 1)]
        in_specs = [sspec, qspec] + kspecs + kspecs + [xspec, xspec]
        args = (sink, qa) + (ka,) * len(kspecs) + (va,) * len(kspecs) + (kx, vx)
        keys = lc + 3 * blk
    else:
        in_specs = [sspec, qspec, xspec, xspec]
        args = (sink, qa, kx, vx)
        keys = lc
    blocks = 2 * _nbytes((qb * blk, qw), BF16) + 2 * _nbytes((keys + qb * blk, kvw), BF16)
    return pl.pallas_call(
        functools.partial(_attn_a_kernel, n_tok=n, lc=lc, window=window),
        out_shape=jax.ShapeDtypeStruct(qa.shape, BF16),
        grid=(bsz, nt),
        in_specs=in_specs,
        out_specs=qspec,
        compiler_params=_params(("parallel", "parallel"), blocks,
                                10 * qb * _nbytes((A_GROUP * blk, keys), F32)),
        name="windowed_gqa" if window else "context_gqa",
    )(*args)


def _softmax2_pv(s1, v1, s2, v2):
    mx = jnp.maximum(jnp.max(s1, axis=-1, keepdims=True), jnp.max(s2, axis=-1, keepdims=True))
    e1 = jnp.exp2(s1 - mx)
    e2 = jnp.exp2(s2 - mx)
    inv = 1.0 / (jnp.sum(e1, axis=-1, keepdims=True) + jnp.sum(e2, axis=-1, keepdims=True))
    return (_dot(e1.astype(BF16), v1) + _dot(e2.astype(BF16), v2)) * inv


def _attn_b_kernel(q_ref, k0_ref, k1_ref, k2_ref, k3_ref, v0_ref, v1_ref, v2_ref, v3_ref, kx_ref, vx_ref,
                   bias_ref, o_ref):
    hd = HEAD_DIM
    for h in range(NA_HEADS_PER_STEP):
        cs = slice(h * hd, (h + 1) * hd)
        kwin = jnp.concatenate([k0_ref[:, cs], k1_ref[:, cs], k2_ref[:, cs], k3_ref[:, cs]], axis=0)
        vwin = jnp.concatenate([v0_ref[:, cs], v1_ref[:, cs], v2_ref[:, cs], v3_ref[:, cs]], axis=0)
        q = q_ref[:, cs]
        s_nb = _dot_nt(q, kwin) * QK_SCALE_LOG2 + bias_ref[h, 0]
        s_cx = _dot_nt(q, kx_ref[:, cs]) * QK_SCALE_LOG2
        o_ref[:, cs] = _softmax2_pv(s_nb, vwin, s_cx, vx_ref[:, cs]).astype(o_ref.dtype)


def _attn_bx_kernel(q_ref, kx_ref, vx_ref, o_ref):
    hd = HEAD_DIM
    for h in range(B_HEADS):
        cs = slice(h * hd, (h + 1) * hd)
        s = _dot_nt(q_ref[:, cs], kx_ref[:, cs]) * QK_SCALE_LOG2
        mx = jnp.max(s, axis=-1, keepdims=True)
        e = jnp.exp2(s - mx)
        inv = 1.0 / jnp.sum(e, axis=-1, keepdims=True)
        o_ref[:, cs] = (_dot(e.astype(BF16), vx_ref[:, cs]) * inv).astype(o_ref.dtype)


def _na_bias_kernel(rel_ref, o_ref):
    h = pl.program_id(0)
    pos = pl.program_id(1)
    nrel_r = 2 * NA_ROWS - 1
    nrel_c = 2 * NA_COLS - 1
    qi = lax.broadcasted_iota(jnp.int32, (GRID_W, GRID_W), 0)
    ki = lax.broadcasted_iota(jnp.int32, (GRID_W, GRID_W), 1)
    dc = jnp.clip(ki - qi, -(NA_COLS - 1), NA_COLS - 1) + NA_COLS - 1
    start = jnp.clip(qi - NA_COLS // 2, 0, GRID_W - NA_COLS)
    col_in = (ki >= start) & (ki < start + NA_COLS)
    planes = []
    for dr in range(nrel_r):
        acc = jnp.zeros((GRID_W, GRID_W), F32)
        for dcv in range(nrel_c):
            acc = jnp.where(dc == dcv, rel_ref[(h * nrel_r + dr) * nrel_c + dcv], acc)
        planes.append(jnp.where(col_in, acc * LOG2E, NEG))
    masked = jnp.full((GRID_W, GRID_W), NEG, F32)
    half = NA_ROWS // 2
    first_key = {0: lambda j: max(j, half), 1: lambda j: j, 2: lambda j: min(j, half)}
    for p, lo_of in first_key.items():
        @pl.when(pos == p)
        def _(lo_of=lo_of):
            for j in range(NA_TILE_ROWS):
                lo = lo_of(j)
                for jj in range(NA_WIN_ROWS):
                    blk = planes[jj - j + half - 1] if lo <= jj < lo + NA_ROWS else masked
                    o_ref[0, 0, j * GRID_W:(j + 1) * GRID_W, jj * GRID_W:(jj + 1) * GRID_W] = blk


def _na_bias_table(relpos):
    nh = relpos.shape[0]
    shape = (nh, 3, NA_TILE, NA_WIN_ROWS * GRID_W)
    return pl.pallas_call(
        _na_bias_kernel,
        out_shape=jax.ShapeDtypeStruct(shape, F32),
        grid=(nh, 3),
        in_specs=[pl.BlockSpec(memory_space=pltpu.SMEM)],
        out_specs=pl.BlockSpec((1, 1) + shape[2:], lambda h, p: (h, p, 0, 0)),
        compiler_params=_params(("parallel", "parallel"), _nbytes(shape[2:], F32)),
        name="na_bias_table",
    )(relpos.reshape(-1))


def _attn_b_call(qb, kb, vb, kx, vx, bias, bsz):
    n = qb.shape[0] // bsz
    lc = kx.shape[0] // bsz
    rows = n // GRID_W
    assert rows % NA_TILE_ROWS == 0 and rows >= 2 * NA_TILE_ROWS
    nt = rows // NA_TILE_ROWS
    hp = NA_HEADS_PER_STEP
    hd = hp * HEAD_DIM
    sub = NA_TILE // 2
    qspec = pl.BlockSpec((NA_TILE, hd), lambda b, h, t: (b * nt + t, h))

    def kspec(s):
        return pl.BlockSpec((sub, hd),
                            lambda b, h, t: (jnp.clip(2 * t + s, 0, 2 * nt - 1) + 2 * b * nt, h))

    kspecs = [kspec(s) for s in (-1, 0, 1, 2)]
    xspec = pl.BlockSpec((lc, hd), lambda b, h, t: (b, h))
    bspec = pl.BlockSpec((hp, 1) + bias.shape[2:],
                         lambda b, h, t: (h, jnp.where(t == 0, 0, jnp.where(t == nt - 1, 2, 1)), 0, 0))
    blocks = (2 * _nbytes((NA_TILE, hd), BF16) + 8 * _nbytes((sub, hd), BF16) + 2 * _nbytes((lc, hd), BF16)
              + hp * _nbytes(bias.shape[2:], F32))
    return pl.pallas_call(
        _attn_b_kernel,
        out_shape=jax.ShapeDtypeStruct(qb.shape, BF16),
        grid=(bsz, B_HEADS // hp, nt),
        in_specs=[qspec] + kspecs + kspecs + [xspec, xspec, bspec],
        out_specs=qspec,
        compiler_params=_params(("parallel", "parallel", "parallel"), blocks,
                                5 * hp * _nbytes((NA_TILE, NA_WIN_ROWS * GRID_W + lc), F32)),
        name="neighbourhood_attn",
    )(qb, kb, kb, kb, kb, vb, vb, vb, vb, kx, vx, bias)


def _attn_bx_call(qx, kx, vx, bsz):
    lc = qx.shape[0] // bsz
    w = qx.shape[1]
    spec = pl.BlockSpec((lc, w), lambda b: (b, 0))
    return pl.pallas_call(
        _attn_bx_kernel,
        out_shape=jax.ShapeDtypeStruct(qx.shape, BF16),
        grid=(bsz,),
        in_specs=[spec, spec, spec],
        out_specs=spec,
        compiler_params=_params(("parallel",), 4 * _nbytes((lc, w), BF16), 4 * 1024 * 1024),
        name="context_full_attn",
    )(qx, kx, vx)


def _log_sigmoid(x):
    return -(jnp.maximum(-x, 0.0) + jnp.log1p(jnp.exp(-jnp.abs(x))))


CONV_HALO = 8


def _cprep_kernel(q_ref, k_ref, v_ref, qlo_ref, qhi_ref, klo_ref, khi_ref, cw_ref, cb_ref,
                  qs_ref, kt_ref, vb_ref, ext, *, per):
    pos = pl.program_id(0) % per
    tr = q_ref.shape[0]
    hw = q_ref.shape[1]
    halo = CONV_HALO
    pad = C_CONV // 2

    def conv_silu(x_ref, lo_ref, hi_ref, col0):
        ext[0:halo, :] = jnp.where(pos > 0, lo_ref[...], 0.0)
        ext[halo:halo + tr, :] = x_ref[...]
        ext[halo + tr:2 * halo + tr, :] = jnp.where(pos < per - 1, hi_ref[...], 0.0)
        out = cb_ref[:, col0:col0 + hw]
        for j in range(C_CONV):
            out = out + cw_ref[j:j + 1, col0:col0 + hw] * ext[halo - pad + j:halo - pad + j + tr, :]
        return _silu(out)

    qs_ref[...] = (conv_silu(q_ref, qlo_ref, qhi_ref, 0) * ATTN_SCALE).astype(BF16)
    kc = conv_silu(k_ref, klo_ref, khi_ref, hw)
    for ck in range(tr // M_CHUNK):
        for h in range(C_HEADS):
            blk = kc[ck * M_CHUNK:(ck + 1) * M_CHUNK, h * HEAD_DIM:(h + 1) * HEAD_DIM]
            kt_ref[ck, h * HEAD_DIM:(h + 1) * HEAD_DIM, :] = blk.T.astype(BF16)
    vb_ref[...] = v_ref[...].astype(BF16)


def _cprep_call(p_main, conv_w, conv_b, seq):
    m = p_main.shape[0]
    hw = C_HEADS * HEAD_DIM
    assert M_CHUNK == HEAD_DIM
    tr = _pick(seq, (256, 128))
    per = seq // tr
    halo = CONV_HALO
    hb = tr // halo
    nblk = m // halo
    cpb = tr // M_CHUNK

    def main(col):
        return pl.BlockSpec((tr, hw), lambda i: (i, col))

    def lo(col):
        return pl.BlockSpec((halo, hw), lambda i: (jnp.maximum(i * hb - 1, 0), col))

    def hi(col):
        return pl.BlockSpec((halo, hw), lambda i: (jnp.minimum((i + 1) * hb, nblk - 1), col))

    blocks = (3 * _nbytes((tr, hw), F32) + 4 * _nbytes((halo, hw), F32) + _nbytes((C_CONV + 1, 2 * hw), F32)
              + 3 * _nbytes((tr, hw), BF16))
    scratch = _nbytes((tr + 2 * halo, hw), F32)
    out = jax.ShapeDtypeStruct((m, hw), BF16)
    out_t = jax.ShapeDtypeStruct((m // M_CHUNK, hw, M_CHUNK), BF16)
    ospec = pl.BlockSpec((tr, hw), lambda i: (i, 0))
    tspec = pl.BlockSpec((cpb, hw, M_CHUNK), lambda i: (i, 0, 0))
    return pl.pallas_call(
        functools.partial(_cprep_kernel, per=per),
        out_shape=[out, out_t, out],
        grid=(m // tr,),
        in_specs=[main(COL_CQ), main(COL_CK), main(COL_CV), lo(COL_CQ), hi(COL_CQ), lo(COL_CK), hi(COL_CK),
                  pl.BlockSpec((C_CONV, 2 * hw), lambda i: (0, 0)),
                  pl.BlockSpec((1, 2 * hw), lambda i: (0, 0))],
        out_specs=[ospec, tspec, ospec],
        scratch_shapes=[pltpu.VMEM((tr + 2 * halo, hw), F32)],
        compiler_params=_params(("parallel",), blocks, scratch + 4 * _nbytes((tr, hw), F32)),
        name="mlstm_prep",
    )(p_main, p_main, p_main, p_main, p_main, p_main, p_main, conv_w, conv_b)


GP_PLANES = 6


def _gateprep_kernel(g_ref, gb_ref, o_ref):
    nh = C_HEADS
    L = M_CHUNK
    lane = lax.broadcasted_iota(jnp.int32, (nh, L), 1)
    for ck in range(g_ref.shape[0] // L):
        gt = (g_ref[ck * L:(ck + 1) * L, :] + gb_ref[...]).T
        for d in range(2):
            ig = gt[2 * d * nh:(2 * d + 1) * nh]
            logf = _log_sigmoid(gt[(2 * d + 1) * nh:(2 * d + 2) * nh])
            pre = logf
            sft = 1
            while sft < L:
                pre = pre + jnp.where(lane >= sft, pltpu.roll(pre, sft, axis=1), 0.0)
                sft *= 2
            total = jnp.broadcast_to(pre[:, L - 1:L], (nh, L))
            bcum = pre if d == 0 else total - pre + logf
            r_row = ig - bcum
            rmax = r_row
            sft = 1
            while sft < L:
                if d == 0:
                    rmax = jnp.maximum(rmax, jnp.where(lane >= sft, pltpu.roll(rmax, sft, axis=1), NEG))
                else:
                    rmax = jnp.maximum(rmax, jnp.where(lane < L - sft, pltpu.roll(rmax, L - sft, axis=1), NEG))
                sft *= 2
            r_top = jnp.broadcast_to(rmax[:, L - 1:L] if d == 0 else rmax[:, 0:1], (nh, L))
            w_row = jnp.exp(r_row - r_top)
            for k, plane in enumerate((r_row, rmax, bcum, total, w_row, r_top)):
                o_ref[ck, d, k * nh:(k + 1) * nh, :] = plane


def _gateprep_call(gates, gate_b, seq):
    m = gates.shape[0]
    L = M_CHUNK
    tr = _pick(seq, (1024, 512, 256, 128))
    shape = (m // L, 2, GP_PLANES * C_HEADS, L)
    blocks = _nbytes((tr, GATE_W), F32) + _nbytes((tr // L,) + shape[1:], F32)
    return pl.pallas_call(
        _gateprep_kernel,
        out_shape=jax.ShapeDtypeStruct(shape, F32),
        grid=(m // tr,),
        in_specs=[pl.BlockSpec((tr, GATE_W), lambda i: (i, 0)),
                  pl.BlockSpec((1, GATE_W), lambda i: (0, 0))],
        out_specs=pl.BlockSpec((tr // L,) + shape[1:], lambda i: (i, 0, 0, 0)),
        compiler_params=_params(("parallel",), blocks, 2 * 1024 * 1024),
        name="mlstm_gate_prep",
    )(gates, gate_b)


def _mlstm_kernel(qf_ref, ktf_ref, vf_ref, gpf_ref, qb_ref, ktb_ref, vb_ref, gpb_ref, s0_ref, m0_ref,
                  hf_ref, hb_ref, sf_ref, mf_ref, cst, mst, *, nc):
    c = pl.program_id(1)
    L = M_CHUNK
    hd = HEAD_DIM
    nh = C_HEADS

    @pl.when(c == 0)
    def _():
        cst[...] = s0_ref[...]
        mst[...] = m0_ref[...]

    ii = lax.broadcasted_iota(jnp.int32, (L, L), 0)
    jj = lax.broadcasted_iota(jnp.int32, (L, L), 1)
    ones = jnp.ones((L, hd), BF16)
    dirs = ((qf_ref, ktf_ref, vf_ref, gpf_ref, hf_ref, ii >= jj), (qb_ref, ktb_ref, vb_ref, gpb_ref, hb_ref, ii <= jj))
    for d, (q_ref, kt_ref, v_ref, gp_ref, h_ref, causal) in enumerate(dirs):
        r_row, rmax, bcum, total, w_row, r_top = (gp_ref[0, 0, k * nh:(k + 1) * nh, :] for k in range(GP_PLANES))
        m0 = mst[d]
        big_m = jnp.maximum(rmax, m0)
        si_row = jnp.exp(m0 - big_m)
        nrm_row = jnp.exp(-(bcum + big_m))
        m_loc = total + r_top
        m_new = jnp.maximum(total + m0, m_loc)
        s_prev = jnp.exp(total + m0 - m_new)
        s_loc = jnp.exp(m_loc - m_new)
        mst[d] = m_new
        xt = jnp.concatenate([big_m, si_row, nrm_row, jnp.zeros((L - 3 * nh, L), F32)], axis=0).T
        for h in range(nh):
            cs = slice(h * hd, (h + 1) * hd)
            qs = q_ref[:, cs]
            kt = kt_ref[0, cs, :]
            vext = jnp.concatenate([v_ref[:, cs], ones], axis=1)
            cext = cst[d * nh + h]
            expo = jnp.where(causal, r_row[h:h + 1, :] - xt[:, h:h + 1], NEG)
            sw = jnp.exp(expo) * _dot(qs, kt)
            si = jnp.broadcast_to(xt[:, nh + h:nh + h + 1], (L, 2 * hd))
            both = _dot(sw.astype(BF16), vext) + si * _dot(qs, cext.astype(BF16))
            nrm = jnp.broadcast_to(xt[:, 2 * nh + h:2 * nh + h + 1], (L, hd))
            h_ref[:, cs] = both[:, 0:hd] / jnp.maximum(jnp.abs(both[:, hd:2 * hd]), nrm)
            kwt = (kt.astype(F32) * w_row[h:h + 1, :]).astype(BF16)
            sp = jnp.concatenate([s_prev[h:h + 1, :], s_prev[h:h + 1, :]], axis=1)
            sl = jnp.concatenate([s_loc[h:h + 1, :], s_loc[h:h + 1, :]], axis=1)
            cst[d * nh + h] = sp * cext + sl * _dot(kwt, vext)

    @pl.when(c == nc - 1)
    def _():
        sf_ref[...] = cst[...]
        mf_ref[...] = mst[...]


def _mlstm_call(qs, kt, vb, gplanes, s0, m0, bsz):
    t = qs.shape[0] // bsz
    L = M_CHUNK
    nc = t // L
    nh = C_HEADS
    hw = nh * HEAD_DIM

    def cidx(d, b, c):
        return b * nc + (c if d == 0 else nc - 1 - c)

    def dir_specs(d):
        return [pl.BlockSpec((L, hw), lambda b, c: (cidx(d, b, c), 0)),
                pl.BlockSpec((1, hw, L), lambda b, c: (cidx(d, b, c), 0, 0)),
                pl.BlockSpec((L, hw), lambda b, c: (cidx(d, b, c), 0)),
                pl.BlockSpec((1, 1, GP_PLANES * nh, L), lambda b, c: (cidx(d, b, c), d, 0, 0))]

    sspec = pl.BlockSpec((2 * nh, HEAD_DIM, 2 * HEAD_DIM), lambda b, c: (b, 0, 0))
    mspec = pl.BlockSpec((2, nh, 128), lambda b, c: (b, 0, 0))
    blocks = (6 * _nbytes((L, hw), BF16) + 2 * _nbytes((GP_PLANES * nh, L), F32)
              + 4 * _nbytes((nh, HEAD_DIM, 2 * HEAD_DIM), F32) + 2 * _nbytes((L, hw), F32))
    scratch = 2 * _nbytes((nh, HEAD_DIM, 2 * HEAD_DIM), F32)
    hshape = jax.ShapeDtypeStruct((bsz * t, hw), F32)
    return pl.pallas_call(
        functools.partial(_mlstm_kernel, nc=nc),
        out_shape=[hshape, hshape, jax.ShapeDtypeStruct(s0.shape, F32), jax.ShapeDtypeStruct(m0.shape, F32)],
        grid=(bsz, nc),
        in_specs=dir_specs(0) + dir_specs(1) + [sspec, mspec],
        out_specs=[pl.BlockSpec((L, hw), lambda b, c: (cidx(0, b, c), 0)),
                   pl.BlockSpec((L, hw), lambda b, c: (cidx(1, b, c), 0)), sspec, mspec],
        scratch_shapes=[pltpu.VMEM((2 * nh, HEAD_DIM, 2 * HEAD_DIM), F32), pltpu.VMEM((2, nh, 128), F32)],
        compiler_params=_params(("parallel", "arbitrary"), blocks, scratch + 16 * 1024 * 1024),
        name="mlstm_scan",
    )(qs, kt, vb, gplanes, qs, kt, vb, gplanes, s0, m0)


def _mlstm_out_kernel(hf_ref, hb_ref, o_ref, g_ref, y_ref):
    hd = HEAD_DIM
    for h in range(C_HEADS):
        cs = slice(h * hd, (h + 1) * hd)
        hn = _head_norm(hf_ref[:, cs] + hb_ref[:, cs], g_ref[:, cs])
        y_ref[:, cs] = (hn * jax.nn.sigmoid(o_ref[:, cs])).astype(y_ref.dtype)


def _mlstm_out_call(hf, hb, p_main, norm_g):
    m, hw = hf.shape
    tr = _pick(m, (256, 128))
    blocks = 3 * _nbytes((tr, hw), F32) + _nbytes((tr, hw), BF16)
    hspec = pl.BlockSpec((tr, hw), lambda i: (i, 0))
    return pl.pallas_call(
        _mlstm_out_kernel,
        out_shape=jax.ShapeDtypeStruct((m, hw), BF16),
        grid=(m // tr,),
        in_specs=[hspec, hspec,
                  pl.BlockSpec((tr, hw), lambda i: (i, COL_CO)),
                  pl.BlockSpec((1, hw), lambda i: (0, 0))],
        out_specs=hspec,
        compiler_params=_params(("parallel",), blocks, 2 * 1024 * 1024),
        name="mlstm_out",
    )(hf, hb, p_main, norm_g.reshape(1, hw))


def _rope_lane_tables(n):
    t = jnp.arange(n)
    inv = ROPE_THETA ** (-jnp.arange(ROPE_PAIRS, dtype=F32) / ROPE_PAIRS)
    row = (t // GRID_W).astype(F32)[:, None] * inv
    col = (t % GRID_W).astype(F32)[:, None] * inv
    cos = jnp.concatenate([jnp.cos(row), jnp.cos(row), jnp.cos(col), jnp.cos(col)], axis=1)
    sin = jnp.concatenate([-jnp.sin(row), jnp.sin(row), -jnp.sin(col), jnp.sin(col)], axis=1)
    return cos, sin


def _split_w_in(w_in):
    aq_end = BRANCH_W
    akv_end = aq_end + 2 * A_KV_HEADS * HEAD_DIM
    main = jnp.concatenate([w_in[..., :aq_end], w_in[..., akv_end:MAIN_W], w_in[..., aq_end:akv_end]], axis=-1)
    gates = jnp.pad(w_in[..., MAIN_W:], ((0, 0), (0, 0), (0, GATE_W - 4 * C_HEADS)))
    return main.astype(BF16), gates.astype(BF16)


def kernel(x, c, ctx, c_ctx, w_mod, b_mod, norm_g, ffn_w1, ffn_w3, ffn_w2, w_in, qk_g, attn_sink, na_relpos,
           mlstm_conv_w, mlstm_conv_b, mlstm_gate_b, mlstm_norm_g, w_gate, b_gate, w_branch, w_out):
    bsz, n, d = x.shape
    lc = ctx.shape[1]
    depth = w_mod.shape[0]
    assert n % GRID_W == 0 and n % M_CHUNK == 0 and lc % M_CHUNK == 0 and n % A_BLOCK == 0

    cs = jnp.zeros((8, d), F32).at[:bsz].set(c).at[bsz].set(c_ctx)
    mods = _mod_call(cs, w_mod, b_mod).reshape(depth, 8, N_MOD, d)
    cos_t, sin_t = _rope_lane_tables(n)

    xl = x.reshape(bsz * n, d)
    xc = ctx.reshape(bsz * lc, d)
    nh = C_HEADS
    wg = w_gate.astype(BF16)
    wb = w_branch.astype(BF16)
    w_main, w_gates = _split_w_in(w_in)
    s_zero = jnp.zeros((bsz * 2 * nh, HEAD_DIM, 2 * HEAD_DIM), F32)
    m_zero = jnp.zeros((bsz * 2, nh, 128), F32)

    def ffn(xs, md, k0, g, layer, sub):
        u = _normmod_call(xs, g, md[:, k0], md[:, k0 + 1])
        h = _up_call(u, ffn_w1, ffn_w3, layer, sub)
        return _down_call(h, ffn_w2, xs, md[:, k0 + 2], 0.5, lead=(layer, sub))

    for i in range(depth):
        last = i == depth - 1
        ml = mods[i, :bsz]
        mc = mods[i, bsz:bsz + 1]
        gate_b = jnp.pad(mlstm_gate_b[i].reshape(1, 4 * nh), ((0, 0), (0, GATE_W - 4 * nh)))
        conv_b = mlstm_conv_b[i].reshape(1, -1)
        bias_tab = _na_bias_table(na_relpos[i])

        xc = ffn(xc, mc, 0, norm_g[i, 0], i, 0)
        xl = ffn(xl, ml, 0, norm_g[i, 0], i, 0)

        uc = _normmod_call(xc, norm_g[i, 1], mc[:, 3], mc[:, 4])
        ul = _normmod_call(xl, norm_g[i, 1], ml[:, 3], ml[:, 4])
        pc = _inproj_call(uc, w_main, i)
        pl_ = _inproj_call(ul, w_main, i)
        gc = _matmul_call(uc, w_gates[i])
        gl = _matmul_call(ul, w_gates[i])

        qa_c, qb_c, kb_c, vb_c, ka_c, va_c = _qkprep_call(pc, qk_g[i], cos_t, sin_t, False, lc)
        qa_l, qb_l, kb_l, vb_l, ka_l, va_l = _qkprep_call(pl_, qk_g[i], cos_t, sin_t, True, n)

        a_l = _attn_a_call(qa_l, ka_l, va_l, ka_c, va_c, attn_sink[i], bsz, True)
        n_l = _attn_b_call(qb_l, kb_l, vb_l, kb_c, vb_c, bias_tab, bsz)
        qc_c, kc_c, vc_c = _cprep_call(pc, mlstm_conv_w[i], conv_b, lc)
        qc_l, kc_l, vc_l = _cprep_call(pl_, mlstm_conv_w[i], conv_b, n)
        hc_f, hc_b, s_c, m_c = _mlstm_call(qc_c, kc_c, vc_c, _gateprep_call(gc, gate_b, lc), s_zero, m_zero, bsz)
        hl_f, hl_b, _, _ = _mlstm_call(qc_l, kc_l, vc_l, _gateprep_call(gl, gate_b, n), s_c, m_c, bsz)
        m_l = _mlstm_out_call(hl_f, hl_b, pl_, mlstm_norm_g[i])

        yl = _merge_call(ul, a_l, n_l, m_l, wg, b_gate, wb, i)
        xl = _down_call(yl, w_out, xl, ml[:, 5], 1.0, lead=(i,))
        xl = ffn(xl, ml, 6, norm_g[i, 2], i, 1)
        if not last:
            a_c = _attn_a_call(qa_c, ka_c, va_c, ka_c, va_c, attn_sink[i], bsz, False)
            n_c = _attn_bx_call(qb_c, kb_c, vb_c, bsz)
            mm_c = _mlstm_out_call(hc_f, hc_b, pc, mlstm_norm_g[i])
            yc = _merge_call(uc, a_c, n_c, mm_c, wg, b_gate, wb, i)
            xc = _down_call(yc, w_out, xc, mc[:, 5], 1.0, lead=(i,))
            xc = ffn(xc, mc, 6, norm_g[i, 2], i, 1)
    return xl.reshape(bsz, n, d)
```

```python
import functools

import numpy as np
import jax
import jax.numpy as jnp
from jax import lax
from jax.experimental import pallas as pl
from jax.experimental.pallas import tpu as pltpu

F32 = jnp.float32
BF16 = jnp.bfloat16

HEAD_DIM = 128
GRID_W = 64
ROPE_PAIRS = HEAD_DIM // 4
ROPE_THETA = 10000.0
A_HEADS = 8
A_KV_HEADS = 2
A_GROUP = A_HEADS // A_KV_HEADS
A_WINDOW = 128
A_BLOCK = 128
B_HEADS = 8
NA_ROWS = 8
NA_COLS = 16
C_HEADS = 8
C_CONV = 5
BRANCH_W = 8 * HEAD_DIM
N_MOD = 9
EPS = 1e-6
NEG = -1e30
ATTN_SCALE = HEAD_DIM ** -0.5
LOG2E = 1.4426950408889634
QK_SCALE_LOG2 = ATTN_SCALE * LOG2E

M_CHUNK = 128
NA_TILE_ROWS = 8
NA_TILE = NA_TILE_ROWS * GRID_W
NA_WIN_ROWS = NA_TILE_ROWS + NA_ROWS
NA_HEADS_PER_STEP = 4
A_QBLOCKS = 2

V7X_VMEM_BYTES = 64 * 1024 * 1024
VMEM_CAP = V7X_VMEM_BYTES - 6 * 1024 * 1024

COL_AQ, COL_BQ, COL_BK, COL_BV, COL_CQ, COL_CK, COL_CV, COL_CO = range(8)
MAIN_W = 8 * BRANCH_W + 2 * A_KV_HEADS * HEAD_DIM
GATE_W = 128


def _pick(n, cands):
    for c in cands:
        if n % c == 0:
            return c
    raise ValueError(f"no tile in {cands} divides {n}")


def _params(sem, block_bytes, temp_bytes=0):
    limit = 2 * block_bytes + temp_bytes + 4 * 1024 * 1024
    return pltpu.CompilerParams(dimension_semantics=sem,
                                vmem_limit_bytes=int(min(max(limit, 16 * 1024 * 1024), VMEM_CAP)))


def _nbytes(shape, dtype):
    return int(np.prod(shape)) * jnp.dtype(dtype).itemsize


def _dot(a, b):
    return jnp.dot(a, b, preferred_element_type=F32)


def _dot_nt(a, b):
    return lax.dot_general(a, b, (((1,), (1,)), ((), ())), preferred_element_type=F32)


def _dot_tn(a, b):
    return lax.dot_general(a, b, (((0,), (0,)), ((), ())), preferred_element_type=F32)


def _silu(x):
    return x * jax.nn.sigmoid(x)


def _mod_kernel(c_ref, w_ref, b_ref, o_ref):
    a = _silu(c_ref[...]).astype(BF16)
    o_ref[0] = _dot(a, w_ref[0].astype(BF16)) + b_ref[0]


def _mod_call(cs, w_mod, b_mod):
    depth, d, nd = w_mod.shape
    tn = _pick(nd, (1024, 512, 256, 128))
    rows = cs.shape[0]
    blocks = _nbytes((d, tn), F32) + _nbytes((rows, d), F32) + _nbytes((rows, tn), F32)
    return pl.pallas_call(
        _mod_kernel,
        out_shape=jax.ShapeDtypeStruct((depth, rows, nd), F32),
        grid=(depth, nd // tn),
        in_specs=[pl.BlockSpec((rows, d), lambda l, j: (0, 0)),
                  pl.BlockSpec((1, d, tn), lambda l, j: (l, 0, j)),
                  pl.BlockSpec((1, 1, tn), lambda l, j: (l, 0, j))],
        out_specs=pl.BlockSpec((1, rows, tn), lambda l, j: (l, 0, j)),
        compiler_params=_params(("parallel", "parallel"), blocks, _nbytes((d, tn), BF16)),
        name="mod_vectors",
    )(cs, w_mod, b_mod.reshape(depth, 1, nd))


def _normmod_kernel(x_ref, g_ref, shift_ref, scale_ref, o_ref):
    x = x_ref[...]
    ms = jnp.mean(x * x, axis=-1, keepdims=True)
    y = x * lax.rsqrt(ms + EPS) * g_ref[...]
    o_ref[...] = (y * (1.0 + scale_ref[0]) + shift_ref[0]).astype(o_ref.dtype)


def _normmod_call(x, g, shift, scale):
    m, d = x.shape
    groups = shift.shape[0]
    tr = _pick(m // groups, (512, 256, 128, 64, 8))
    per = (m // groups) // tr
    blocks = _nbytes((tr, d), F32) + _nbytes((tr, d), BF16) + 3 * _nbytes((1, d), F32)
    return pl.pallas_call(
        _normmod_kernel,
        out_shape=jax.ShapeDtypeStruct((m, d), BF16),
        grid=(m // tr,),
        in_specs=[pl.BlockSpec((tr, d), lambda i: (i, 0)),
                  pl.BlockSpec((1, d), lambda i: (0, 0)),
                  pl.BlockSpec((1, 1, d), lambda i: (i // per, 0, 0)),
                  pl.BlockSpec((1, 1, d), lambda i: (i // per, 0, 0))],
        out_specs=pl.BlockSpec((tr, d), lambda i: (i, 0)),
        compiler_params=_params(("parallel",), blocks, 2 * _nbytes((tr, d), F32)),
        name="norm_modulate",
    )(x, g.reshape(1, d), shift.reshape(groups, 1, d), scale.reshape(groups, 1, d))


def _up_kernel(u_ref, w1_ref, w3_ref, o_ref):
    u = u_ref[...]
    h1 = _dot(u, w1_ref[...].astype(BF16))
    h3 = _dot(u, w3_ref[...].astype(BF16))
    o_ref[...] = (_silu(h1) * h3).astype(o_ref.dtype)


def _up_call(u, w1, w3, layer, sub):
    m, d = u.shape
    f = w1.shape[-1]
    tm = _pick(m, (1024, 512, 256))
    tf = _pick(f, (256, 128))
    wspec = pl.BlockSpec((None, None, d, tf), lambda i, j: (layer, sub, 0, j))
    blocks = _nbytes((tm, d), BF16) + 2 * _nbytes((d, tf), F32) + _nbytes((tm, tf), BF16)
    return pl.pallas_call(
        _up_kernel,
        out_shape=jax.ShapeDtypeStruct((m, f), BF16),
        grid=(m // tm, f // tf),
        in_specs=[pl.BlockSpec((tm, d), lambda i, j: (i, 0)), wspec, wspec],
        out_specs=pl.BlockSpec((tm, tf), lambda i, j: (i, j)),
        compiler_params=_params(("parallel", "parallel"), blocks,
                                2 * _nbytes((d, tf), BF16) + 4 * _nbytes((tm, tf), F32)),
        name="ffn_up",
    )(u, w1, w3)


def _down_kernel(a_ref, w_ref, x_ref, gate_ref, o_ref, *, coef):
    acc = _dot(a_ref[...], w_ref[...].astype(BF16))
    o_ref[...] = x_ref[...] + (coef * gate_ref[0]) * acc


def _down_call(a, w, x, gate, coef, lead=()):
    m, k = a.shape
    d = w.shape[-1]
    groups = gate.shape[0]
    tm = _pick(m // groups, (1024, 512, 256))
    tn = _pick(d, (512, 256, 128) if k <= 4096 else (256, 128))
    per = (m // groups) // tm
    blocks = (_nbytes((tm, k), BF16) + _nbytes((k, tn), w.dtype) + 2 * _nbytes((tm, tn), F32)
              + _nbytes((1, tn), F32))
    temps = 2 * _nbytes((tm, tn), F32) + (_nbytes((k, tn), BF16) if w.dtype != BF16 else 0)
    wspec = pl.BlockSpec((None,) * len(lead) + (k, tn), lambda i, j: tuple(lead) + (0, j))
    return pl.pallas_call(
        functools.partial(_down_kernel, coef=coef),
        out_shape=jax.ShapeDtypeStruct((m, d), F32),
        grid=(m // tm, d // tn),
        in_specs=[pl.BlockSpec((tm, k), lambda i, j: (i, 0)),
                  wspec,
                  pl.BlockSpec((tm, tn), lambda i, j: (i, j)),
                  pl.BlockSpec((1, 1, tn), lambda i, j: (i // per, 0, j))],
        out_specs=pl.BlockSpec((tm, tn), lambda i, j: (i, j)),
        compiler_params=_params(("parallel", "parallel"), blocks, temps),
        name="proj_residual",
    )(a, w, x, gate.reshape(groups, 1, d))


def _matmul_kernel(a_ref, w_ref, o_ref):
    o_ref[...] = _dot(a_ref[...], w_ref[...].astype(BF16)).astype(o_ref.dtype)


IN_TN = 2 * A_KV_HEADS * HEAD_DIM


def _inproj_call(u, w_main, layer):
    m, d = u.shape
    tm = _pick(m, (1024, 512, 256))
    tn = IN_TN
    blocks = _nbytes((tm, d), BF16) + _nbytes((d, tn), BF16) + _nbytes((tm, tn), F32)
    return pl.pallas_call(
        _matmul_kernel,
        out_shape=jax.ShapeDtypeStruct((m, MAIN_W), F32),
        grid=(m // tm, MAIN_W // tn),
        in_specs=[pl.BlockSpec((tm, d), lambda i, j: (i, 0)),
                  pl.BlockSpec((None, d, tn), lambda i, j: (layer, 0, j))],
        out_specs=pl.BlockSpec((tm, tn), lambda i, j: (i, j)),
        compiler_params=_params(("parallel", "parallel"), blocks, 2 * _nbytes((tm, tn), F32)),
        name="in_proj",
    )(u, w_main)


def _matmul_call(a, w, out_dtype=F32):
    m, k = a.shape
    n = w.shape[1]
    tm = _pick(m, (1024, 512, 256))
    tn = _pick(n, (512, 256, 128))
    blocks = _nbytes((tm, k), BF16) + _nbytes((k, tn), BF16) + _nbytes((tm, tn), out_dtype)
    return pl.pallas_call(
        _matmul_kernel,
        out_shape=jax.ShapeDtypeStruct((m, n), out_dtype),
        grid=(m // tm, n // tn),
        in_specs=[pl.BlockSpec((tm, k), lambda i, j: (i, 0)),
                  pl.BlockSpec((k, tn), lambda i, j: (0, j))],
        out_specs=pl.BlockSpec((tm, tn), lambda i, j: (i, j)),
        compiler_params=_params(("parallel", "parallel"), blocks, _nbytes((tm, tn), F32)),
        name="in_proj",
    )(a, w)


def _merge_kernel(u_ref, oa_ref, ob_ref, oc_ref, wg_ref, bg_ref, wb_ref, y_ref):
    u = u_ref[...]
    acc = None
    for j, br_ref in enumerate((oa_ref, ob_ref, oc_ref)):
        gate = jax.nn.sigmoid(_dot(u, wg_ref[j]) + bg_ref[j])
        term = gate * _dot(br_ref[...], wb_ref[j])
        acc = term if acc is None else acc + term
    y_ref[...] = acc.astype(y_ref.dtype)


def _merge_call(u, oa, ob, oc, wg, bg, wb, layer):
    m, d = u.shape
    bw = oa.shape[1]
    tm = _pick(m, (1024, 512, 256))
    tn = _pick(d, (256, 128))
    blocks = (_nbytes((tm, d), BF16) + 3 * _nbytes((tm, bw), BF16) + 3 * _nbytes((d, tn), BF16)
              + 3 * _nbytes((bw, tn), BF16) + _nbytes((tm, tn), BF16))
    return pl.pallas_call(
        _merge_kernel,
        out_shape=jax.ShapeDtypeStruct((m, d), BF16),
        grid=(m // tm, d // tn),
        in_specs=[pl.BlockSpec((tm, d), lambda i, j: (i, 0)),
                  pl.BlockSpec((tm, bw), lambda i, j: (i, 0)),
                  pl.BlockSpec((tm, bw), lambda i, j: (i, 0)),
                  pl.BlockSpec((tm, bw), lambda i, j: (i, 0)),
                  pl.BlockSpec((None, 3, d, tn), lambda i, j: (layer, 0, 0, j)),
                  pl.BlockSpec((None, 3, 1, tn), lambda i, j: (layer, 0, 0, j)),
                  pl.BlockSpec((None, 3, bw, tn), lambda i, j: (layer, 0, 0, j))],
        out_specs=pl.BlockSpec((tm, tn), lambda i, j: (i, j)),
        compiler_params=_params(("parallel", "parallel"), blocks, 10 * _nbytes((tm, tn), F32)),
        name="gated_merge",
    )(u, oa, ob, oc, wg, bg.reshape(bg.shape[0], 3, 1, d), wb)


def _head_norm(x, g):
    ms = jnp.mean(x * x, axis=-1, keepdims=True)
    return x * lax.rsqrt(ms + EPS) * g


def _rope(y, cos, sin_signed):
    lane = lax.broadcasted_iota(jnp.int32, y.shape, 1)
    partner = jnp.where((lane % 64) < ROPE_PAIRS, pltpu.roll(y, HEAD_DIM - ROPE_PAIRS, axis=1),
                        pltpu.roll(y, ROPE_PAIRS, axis=1))
    return y * cos + partner * sin_signed


def _qkprep_kernel(p4_ref, pa_ref, g_ref, cos_ref, sin_ref,
                   qa_ref, qb_ref, kb_ref, vb_ref, ka_ref, va_ref, *, rope):
    hd = HEAD_DIM
    if rope:
        cos = cos_ref[...]
        sin = sin_ref[...]
    for h in range(A_HEADS):
        y = _head_norm(p4_ref[:, h * hd:(h + 1) * hd], g_ref[0:1, :])
        if rope:
            y = _rope(y, cos, sin)
        qa_ref[:, h * hd:(h + 1) * hd] = y.astype(BF16)
    for h in range(B_HEADS):
        c0 = BRANCH_W + h * hd
        qb_ref[:, h * hd:(h + 1) * hd] = _head_norm(p4_ref[:, c0:c0 + hd], g_ref[2:3, :]).astype(BF16)
        c0 = 2 * BRANCH_W + h * hd
        kb_ref[:, h * hd:(h + 1) * hd] = _head_norm(p4_ref[:, c0:c0 + hd], g_ref[3:4, :]).astype(BF16)
    vb_ref[...] = p4_ref[:, 3 * BRANCH_W:4 * BRANCH_W].astype(BF16)
    for h in range(A_KV_HEADS):
        y = _head_norm(pa_ref[:, h * hd:(h + 1) * hd], g_ref[1:2, :])
        if rope:
            y = _rope(y, cos, sin)
        ka_ref[:, h * hd:(h + 1) * hd] = y.astype(BF16)
    kvw = A_KV_HEADS * hd
    va_ref[...] = pa_ref[:, kvw:2 * kvw].astype(BF16)


def _qkprep_call(p_main, qk_g, cos_t, sin_t, rope, seq):
    m = p_main.shape[0]
    tr = _pick(seq, (256, 128))
    per = seq // tr
    kvw = A_KV_HEADS * HEAD_DIM
    w4 = 4 * BRANCH_W
    blocks = (_nbytes((tr, w4), F32) + _nbytes((tr, 2 * kvw), F32) + 2 * _nbytes((tr, HEAD_DIM), F32)
              + _nbytes((tr, w4), BF16) + _nbytes((tr, 2 * kvw), BF16))
    outs = [jax.ShapeDtypeStruct((m, BRANCH_W), BF16)] * 4 + [jax.ShapeDtypeStruct((m, kvw), BF16)] * 2
    return pl.pallas_call(
        functools.partial(_qkprep_kernel, rope=rope),
        out_shape=outs,
        grid=(m // tr,),
        in_specs=[pl.BlockSpec((tr, w4), lambda i: (i, 0)),
                  pl.BlockSpec((tr, 2 * kvw), lambda i: (i, (8 * BRANCH_W) // (2 * kvw))),
                  pl.BlockSpec((4, HEAD_DIM), lambda i: (0, 0)),
                  pl.BlockSpec((tr, HEAD_DIM), lambda i: (i % per, 0)),
                  pl.BlockSpec((tr, HEAD_DIM), lambda i: (i % per, 0))],
        out_specs=[pl.BlockSpec((tr, BRANCH_W), lambda i: (i, 0))] * 4
                  + [pl.BlockSpec((tr, kvw), lambda i: (i, 0))] * 2,
        compiler_params=_params(("parallel",), blocks, 4 * _nbytes((tr, HEAD_DIM), F32)),
        name="qk_prep",
    )(p_main, p_main, qk_g, cos_t, sin_t)


def _attn_a_kernel(sink_ref, q_ref, *rest, lc, window):
    nwin = A_QBLOCKS + 2 if window else 0
    nmask = A_QBLOCKS if window else 0
    k_refs = rest[:nwin]
    v_refs = rest[nwin:2 * nwin]
    mask_refs = rest[2 * nwin:2 * nwin + nmask]
    kx_ref, vx_ref, o_ref = rest[2 * nwin + nmask:]
    hd = HEAD_DIM
    blk = A_BLOCK
    gw = A_GROUP * hd
    keys = lc + 3 * blk if window else lc
    ones = jnp.ones((keys, hd), BF16)
    for sb in range(A_QBLOCKS):
        rows = slice(sb * blk, (sb + 1) * blk)
        for hk in range(A_KV_HEADS):
            cs = slice(hk * hd, (hk + 1) * hd)
            if window:
                kall = jnp.concatenate([kx_ref[:, cs]] + [r[:, cs] for r in k_refs[sb:sb + 3]], axis=0)
                vall = jnp.concatenate([vx_ref[:, cs]] + [r[:, cs] for r in v_refs[sb:sb + 3]], axis=0)
            else:
                kall = kx_ref[:, cs]
                vall = vx_ref[:, cs]
            q4 = jnp.concatenate([q_ref[rows, hk * gw + g * hd:hk * gw + (g + 1) * hd] for g in range(A_GROUP)],
                                 axis=0)
            s = _dot_nt(q4, kall) * QK_SCALE_LOG2
            if window:
                s = s + mask_refs[sb][0]
            snk = jnp.concatenate(
                [jnp.full((blk, 1), sink_ref[hk * A_GROUP + g] * LOG2E, F32) for g in range(A_GROUP)], axis=0)
            mx = jnp.maximum(jnp.max(s, axis=-1, keepdims=True), snk)
            e = jnp.exp2(s - mx).astype(BF16)
            both = _dot(e, jnp.concatenate([vall, ones], axis=1))
            o = both[:, 0:hd] * (1.0 / (both[:, hd:2 * hd] + jnp.exp2(snk - mx)))
            for g in range(A_GROUP):
                o_ref[rows, hk * gw + g * hd:hk * gw + (g + 1) * hd] = (
                    o[g * blk:(g + 1) * blk].astype(o_ref.dtype))


def _gqa_mask_table(lc, nb):
    blk = A_BLOCK
    r = np.arange(blk)[:, None]
    kc = np.arange(3 * blk)[None, :]
    band = np.abs(r + blk - kc) <= A_WINDOW
    tabs = []
    for first, last in ((True, False), (False, False), (False, True)):
        ok = band & ~(first & (kc < blk)) & ~(last & (kc >= 2 * blk))
        win = np.where(ok, 0.0, NEG).astype(np.float32)
        tabs.append(np.tile(np.concatenate([np.zeros((blk, lc), np.float32), win], axis=1), (A_GROUP, 1)))
    assert nb >= 2
    return np.stack(tabs)


def _attn_a_call(qa, ka, va, kx, vx, sink, bsz, window):
    n = qa.shape[0] // bsz
    lc = kx.shape[0] // bsz
    blk = A_BLOCK
    nb = n // blk
    qb = A_QBLOCKS
    assert nb % qb == 0
    nt = nb // qb
    kvw = A_KV_HEADS * HEAD_DIM
    qw = A_HEADS * HEAD_DIM
    qspec = pl.BlockSpec((qb * blk, qw), lambda b, t: (b * nt + t, 0))
    xspec = pl.BlockSpec((lc, kvw), lambda b, t: (b, 0))
    sspec = pl.BlockSpec(memory_space=pltpu.SMEM)
    if window:
        def kspec(off):
            return pl.BlockSpec((blk, kvw), lambda b, t: (b * nb + jnp.clip(t * qb + off, 0, nb - 1), 0))

        def mspec(sb):
            def variant(b, t):
                i = t * qb + sb
                return (jnp.where(i == 0, 0, jnp.where(i == nb - 1, 2, 1)), 0, 0)
            return pl.BlockSpec((1, A_GROUP * blk, lc + 3 * blk), variant)

        kspecs = [kspec(off) for off in range(-1, qb + 1)]
        mspecs = [mspec(sb) for sb in range(qb)]
        mask = jnp.asarray(_gqa_mask_table(lc, nb))
        in_specs = [sspec, qspec] + kspecs + kspecs + mspecs + [xspec, xspec]
        args = (sink, qa) + (ka,) * len(kspecs) + (va,) * len(kspecs) + (mask,) * qb + (kx, vx)
        keys = lc + 3 * blk
    else:
        in_specs = [sspec, qspec, xspec, xspec]
        args = (sink, qa, kx, vx)
        keys = lc
    blocks = (2 * _nbytes((qb * blk, qw), BF16) + 2 * _nbytes((keys + qb * blk, kvw), BF16)
              + (qb * _nbytes((A_GROUP * blk, keys), F32) if window else 0))
    return pl.pallas_call(
        functools.partial(_attn_a_kernel, lc=lc, window=window),
        out_shape=jax.ShapeDtypeStruct(qa.shape, BF16),
        grid=(bsz, nt),
        in_specs=in_specs,
        out_specs=qspec,
        compiler_params=_params(("parallel", "parallel"), blocks,
                                10 * qb * _nbytes((A_GROUP * blk, keys), F32)),
        name="windowed_gqa" if window else "context_gqa",
    )(*args)


def _softmax2_pv(s1, v1, s2, v2):
    hd = v1.shape[1]
    mx = jnp.maximum(jnp.max(s1, axis=-1, keepdims=True), jnp.max(s2, axis=-1, keepdims=True))
    e1 = jnp.exp2(s1 - mx).astype(BF16)
    e2 = jnp.exp2(s2 - mx).astype(BF16)
    v1e = jnp.concatenate([v1, jnp.ones(v1.shape, BF16)], axis=1)
    v2e = jnp.concatenate([v2, jnp.ones(v2.shape, BF16)], axis=1)
    both = _dot(e1, v1e) + _dot(e2, v2e)
    return both[:, 0:hd] * (1.0 / both[:, hd:2 * hd])


def _attn_b_kernel(q_ref, k0_ref, k1_ref, k2_ref, k3_ref, v0_ref, v1_ref, v2_ref, v3_ref, kx_ref, vx_ref,
                   bias_ref, o_ref):
    hd = HEAD_DIM
    for h in range(NA_HEADS_PER_STEP):
        cs = slice(h * hd, (h + 1) * hd)
        kwin = jnp.concatenate([k0_ref[:, cs], k1_ref[:, cs], k2_ref[:, cs], k3_ref[:, cs]], axis=0)
        vwin = jnp.concatenate([v0_ref[:, cs], v1_ref[:, cs], v2_ref[:, cs], v3_ref[:, cs]], axis=0)
        q = q_ref[:, cs]
        s_nb = _dot_nt(q, kwin) * QK_SCALE_LOG2 + bias_ref[h, 0]
        s_cx = _dot_nt(q, kx_ref[:, cs]) * QK_SCALE_LOG2
        o_ref[:, cs] = _softmax2_pv(s_nb, vwin, s_cx, vx_ref[:, cs]).astype(o_ref.dtype)


def _attn_bx_kernel(q_ref, kx_ref, vx_ref, o_ref):
    hd = HEAD_DIM
    for h in range(B_HEADS):
        cs = slice(h * hd, (h + 1) * hd)
        s = _dot_nt(q_ref[:, cs], kx_ref[:, cs]) * QK_SCALE_LOG2
        mx = jnp.max(s, axis=-1, keepdims=True)
        e = jnp.exp2(s - mx)
        inv = 1.0 / jnp.sum(e, axis=-1, keepdims=True)
        o_ref[:, cs] = (_dot(e.astype(BF16), vx_ref[:, cs]) * inv).astype(o_ref.dtype)


def _na_bias_kernel(rel_ref, o_ref):
    h = pl.program_id(0)
    pos = pl.program_id(1)
    nrel_r = 2 * NA_ROWS - 1
    nrel_c = 2 * NA_COLS - 1
    qi = lax.broadcasted_iota(jnp.int32, (GRID_W, GRID_W), 0)
    ki = lax.broadcasted_iota(jnp.int32, (GRID_W, GRID_W), 1)
    dc = jnp.clip(ki - qi, -(NA_COLS - 1), NA_COLS - 1) + NA_COLS - 1
    start = jnp.clip(qi - NA_COLS // 2, 0, GRID_W - NA_COLS)
    col_in = (ki >= start) & (ki < start + NA_COLS)
    planes = []
    for dr in range(nrel_r):
        acc = jnp.zeros((GRID_W, GRID_W), F32)
        for dcv in range(nrel_c):
            acc = jnp.where(dc == dcv, rel_ref[(h * nrel_r + dr) * nrel_c + dcv], acc)
        planes.append(jnp.where(col_in, acc * LOG2E, NEG))
    masked = jnp.full((GRID_W, GRID_W), NEG, F32)
    half = NA_ROWS // 2
    first_key = {0: lambda j: max(j, half), 1: lambda j: j, 2: lambda j: min(j, half)}
    for p, lo_of in first_key.items():
        @pl.when(pos == p)
        def _(lo_of=lo_of):
            for j in range(NA_TILE_ROWS):
                lo = lo_of(j)
                for jj in range(NA_WIN_ROWS):
                    blk = planes[jj - j + half - 1] if lo <= jj < lo + NA_ROWS else masked
                    o_ref[0, 0, j * GRID_W:(j + 1) * GRID_W, jj * GRID_W:(jj + 1) * GRID_W] = blk


def _na_bias_table(relpos):
    nh = relpos.shape[0]
    shape = (nh, 3, NA_TILE, NA_WIN_ROWS * GRID_W)
    return pl.pallas_call(
        _na_bias_kernel,
        out_shape=jax.ShapeDtypeStruct(shape, F32),
        grid=(nh, 3),
        in_specs=[pl.BlockSpec(memory_space=pltpu.SMEM)],
        out_specs=pl.BlockSpec((1, 1) + shape[2:], lambda h, p: (h, p, 0, 0)),
        compiler_params=_params(("parallel", "parallel"), _nbytes(shape[2:], F32)),
        name="na_bias_table",
    )(relpos.reshape(-1))


def _attn_b_call(qb, kb, vb, kx, vx, bias, bsz):
    n = qb.shape[0] // bsz
    lc = kx.shape[0] // bsz
    rows = n // GRID_W
    assert rows % NA_TILE_ROWS == 0 and rows >= 2 * NA_TILE_ROWS
    nt = rows // NA_TILE_ROWS
    hp = NA_HEADS_PER_STEP
    hd = hp * HEAD_DIM
    sub = NA_TILE // 2
    qspec = pl.BlockSpec((NA_TILE, hd), lambda b, h, t: (b * nt + t, h))

    def kspec(s):
        return pl.BlockSpec((sub, hd),
                            lambda b, h, t: (jnp.clip(2 * t + s, 0, 2 * nt - 1) + 2 * b * nt, h))

    kspecs = [kspec(s) for s in (-1, 0, 1, 2)]
    xspec = pl.BlockSpec((lc, hd), lambda b, h, t: (b, h))
    bspec = pl.BlockSpec((hp, 1) + bias.shape[2:],
                         lambda b, h, t: (h, jnp.where(t == 0, 0, jnp.where(t == nt - 1, 2, 1)), 0, 0))
    blocks = (2 * _nbytes((NA_TILE, hd), BF16) + 8 * _nbytes((sub, hd), BF16) + 2 * _nbytes((lc, hd), BF16)
              + hp * _nbytes(bias.shape[2:], F32))
    return pl.pallas_call(
        _attn_b_kernel,
        out_shape=jax.ShapeDtypeStruct(qb.shape, BF16),
        grid=(bsz, B_HEADS // hp, nt),
        in_specs=[qspec] + kspecs + kspecs + [xspec, xspec, bspec],
        out_specs=qspec,
        compiler_params=_params(("parallel", "parallel", "parallel"), blocks,
                                5 * hp * _nbytes((NA_TILE, NA_WIN_ROWS * GRID_W + lc), F32)),
        name="neighbourhood_attn",
    )(qb, kb, kb, kb, kb, vb, vb, vb, vb, kx, vx, bias)


def _attn_bx_call(qx, kx, vx, bsz):
    lc = qx.shape[0] // bsz
    w = qx.shape[1]
    spec = pl.BlockSpec((lc, w), lambda b: (b, 0))
    return pl.pallas_call(
        _attn_bx_kernel,
        out_shape=jax.ShapeDtypeStruct(qx.shape, BF16),
        grid=(bsz,),
        in_specs=[spec, spec, spec],
        out_specs=spec,
        compiler_params=_params(("parallel",), 4 * _nbytes((lc, w), BF16), 4 * 1024 * 1024),
        name="context_full_attn",
    )(qx, kx, vx)


def _log_sigmoid(x):
    return -(jnp.maximum(-x, 0.0) + jnp.log1p(jnp.exp(-jnp.abs(x))))


CONV_HALO = 8


def _cprep_kernel(q_ref, k_ref, v_ref, qlo_ref, qhi_ref, klo_ref, khi_ref, cw_ref, cb_ref,
                  qs_ref, kt_ref, vb_ref, ext, *, per):
    pos = pl.program_id(0) % per
    tr = q_ref.shape[0]
    hw = q_ref.shape[1]
    halo = CONV_HALO
    pad = C_CONV // 2

    def conv_silu(x_ref, lo_ref, hi_ref, col0):
        ext[0:halo, :] = jnp.where(pos > 0, lo_ref[...], 0.0)
        ext[halo:halo + tr, :] = x_ref[...]
        ext[halo + tr:2 * halo + tr, :] = jnp.where(pos < per - 1, hi_ref[...], 0.0)
        out = cb_ref[:, col0:col0 + hw]
        for j in range(C_CONV):
            out = out + cw_ref[j:j + 1, col0:col0 + hw] * ext[halo - pad + j:halo - pad + j + tr, :]
        return _silu(out)

    qs_ref[...] = (conv_silu(q_ref, qlo_ref, qhi_ref, 0) * ATTN_SCALE).astype(BF16)
    kc = conv_silu(k_ref, klo_ref, khi_ref, hw)
    for ck in range(tr // M_CHUNK):
        for h in range(C_HEADS):
            blk = kc[ck * M_CHUNK:(ck + 1) * M_CHUNK, h * HEAD_DIM:(h + 1) * HEAD_DIM]
            kt_ref[ck, h * HEAD_DIM:(h + 1) * HEAD_DIM, :] = blk.T.astype(BF16)
    vb_ref[...] = v_ref[...].astype(BF16)


def _cprep_call(p_main, conv_w, conv_b, seq):
    m = p_main.shape[0]
    hw = C_HEADS * HEAD_DIM
    assert M_CHUNK == HEAD_DIM
    tr = _pick(seq, (256, 128))
    per = seq // tr
    halo = CONV_HALO
    hb = tr // halo
    nblk = m // halo
    cpb = tr // M_CHUNK

    def main(col):
        return pl.BlockSpec((tr, hw), lambda i: (i, col))

    def lo(col):
        return pl.BlockSpec((halo, hw), lambda i: (jnp.maximum(i * hb - 1, 0), col))

    def hi(col):
        return pl.BlockSpec((halo, hw), lambda i: (jnp.minimum((i + 1) * hb, nblk - 1), col))

    blocks = (3 * _nbytes((tr, hw), F32) + 4 * _nbytes((halo, hw), F32) + _nbytes((C_CONV + 1, 2 * hw), F32)
              + 3 * _nbytes((tr, hw), BF16))
    scratch = _nbytes((tr + 2 * halo, hw), F32)
    out = jax.ShapeDtypeStruct((m, hw), BF16)
    out_t = jax.ShapeDtypeStruct((m // M_CHUNK, hw, M_CHUNK), BF16)
    ospec = pl.BlockSpec((tr, hw), lambda i: (i, 0))
    tspec = pl.BlockSpec((cpb, hw, M_CHUNK), lambda i: (i, 0, 0))
    return pl.pallas_call(
        functools.partial(_cprep_kernel, per=per),
        out_shape=[out, out_t, out],
        grid=(m // tr,),
        in_specs=[main(COL_CQ), main(COL_CK), main(COL_CV), lo(COL_CQ), hi(COL_CQ), lo(COL_CK), hi(COL_CK),
                  pl.BlockSpec((C_CONV, 2 * hw), lambda i: (0, 0)),
                  pl.BlockSpec((1, 2 * hw), lambda i: (0, 0))],
        out_specs=[ospec, tspec, ospec],
        scratch_shapes=[pltpu.VMEM((tr + 2 * halo, hw), F32)],
        compiler_params=_params(("parallel",), blocks, scratch + 4 * _nbytes((tr, hw), F32)),
        name="mlstm_prep",
    )(p_main, p_main, p_main, p_main, p_main, p_main, p_main, conv_w, conv_b)


GP_PLANES = 6


def _gateprep_kernel(g_ref, gb_ref, o_ref):
    nh = C_HEADS
    L = M_CHUNK
    lane = lax.broadcasted_iota(jnp.int32, (nh, L), 1)
    for ck in range(g_ref.shape[0] // L):
        gt = (g_ref[ck * L:(ck + 1) * L, :] + gb_ref[...]).T
        for d in range(2):
            ig = gt[2 * d * nh:(2 * d + 1) * nh]
            logf = _log_sigmoid(gt[(2 * d + 1) * nh:(2 * d + 2) * nh])
            pre = logf
            sft = 1
            while sft < L:
                pre = pre + jnp.where(lane >= sft, pltpu.roll(pre, sft, axis=1), 0.0)
                sft *= 2
            total = jnp.broadcast_to(pre[:, L - 1:L], (nh, L))
            bcum = pre if d == 0 else total - pre + logf
            r_row = ig - bcum
            rmax = r_row
            sft = 1
            while sft < L:
                if d == 0:
                    rmax = jnp.maximum(rmax, jnp.where(lane >= sft, pltpu.roll(rmax, sft, axis=1), NEG))
                else:
                    rmax = jnp.maximum(rmax, jnp.where(lane < L - sft, pltpu.roll(rmax, L - sft, axis=1), NEG))
                sft *= 2
            r_top = jnp.broadcast_to(rmax[:, L - 1:L] if d == 0 else rmax[:, 0:1], (nh, L))
            w_row = jnp.exp(r_row - r_top)
            for k, plane in enumerate((r_row, rmax, bcum, total, w_row, r_top)):
                o_ref[ck, d, k * nh:(k + 1) * nh, :] = plane


def _gateprep_call(gates, gate_b, seq):
    m = gates.shape[0]
    L = M_CHUNK
    tr = _pick(seq, (1024, 512, 256, 128))
    shape = (m // L, 2, GP_PLANES * C_HEADS, L)
    blocks = _nbytes((tr, GATE_W), F32) + _nbytes((tr // L,) + shape[1:], F32)
    return pl.pallas_call(
        _gateprep_kernel,
        out_shape=jax.ShapeDtypeStruct(shape, F32),
        grid=(m // tr,),
        in_specs=[pl.BlockSpec((tr, GATE_W), lambda i: (i, 0)),
                  pl.BlockSpec((1, GATE_W), lambda i: (0, 0))],
        out_specs=pl.BlockSpec((tr // L,) + shape[1:], lambda i: (i, 0, 0, 0)),
        compiler_params=_params(("parallel",), blocks, 2 * 1024 * 1024),
        name="mlstm_gate_prep",
    )(gates, gate_b)


def _mlstm_kernel(qf_ref, ktf_ref, vf_ref, gpf_ref, qb_ref, ktb_ref, vb_ref, gpb_ref, s0_ref, m0_ref,
                  hf_ref, hb_ref, sf_ref, mf_ref, cst, mst, *, nc):
    c = pl.program_id(1)
    L = M_CHUNK
    hd = HEAD_DIM
    nh = C_HEADS

    @pl.when(c == 0)
    def _():
        cst[...] = s0_ref[...]
        mst[...] = m0_ref[...]

    ii = lax.broadcasted_iota(jnp.int32, (L, L), 0)
    jj = lax.broadcasted_iota(jnp.int32, (L, L), 1)
    ones = jnp.ones((L, hd), BF16)
    dirs = ((qf_ref, ktf_ref, vf_ref, gpf_ref, hf_ref, ii >= jj), (qb_ref, ktb_ref, vb_ref, gpb_ref, hb_ref, ii <= jj))
    for d, (q_ref, kt_ref, v_ref, gp_ref, h_ref, causal) in enumerate(dirs):
        r_row, rmax, bcum, total, w_row, r_top = (gp_ref[0, 0, k * nh:(k + 1) * nh, :] for k in range(GP_PLANES))
        m0 = mst[d]
        big_m = jnp.maximum(rmax, m0)
        si_row = jnp.exp(m0 - big_m)
        nrm_row = jnp.exp(-(bcum + big_m))
        m_loc = total + r_top
        m_new = jnp.maximum(total + m0, m_loc)
        s_prev = jnp.exp(total + m0 - m_new)
        s_loc = jnp.exp(m_loc - m_new)
        mst[d] = m_new
        xt = jnp.concatenate([big_m, si_row, nrm_row, jnp.zeros((L - 3 * nh, L), F32)], axis=0).T
        for h in range(nh):
            cs = slice(h * hd, (h + 1) * hd)
            qs = q_ref[:, cs]
            kt = kt_ref[0, cs, :]
            vext = jnp.concatenate([v_ref[:, cs], ones], axis=1)
            cext = cst[d * nh + h]
            expo = jnp.where(causal, r_row[h:h + 1, :] - xt[:, h:h + 1], NEG)
            sw = jnp.exp(expo) * _dot(qs, kt)
            si = jnp.broadcast_to(xt[:, nh + h:nh + h + 1], (L, 2 * hd))
            both = _dot(sw.astype(BF16), vext) + si * _dot(qs, cext.astype(BF16))
            nrm = jnp.broadcast_to(xt[:, 2 * nh + h:2 * nh + h + 1], (L, hd))
            h_ref[:, cs] = both[:, 0:hd] / jnp.maximum(jnp.abs(both[:, hd:2 * hd]), nrm)
            kwt = (kt.astype(F32) * w_row[h:h + 1, :]).astype(BF16)
            sp = jnp.concatenate([s_prev[h:h + 1, :], s_prev[h:h + 1, :]], axis=1)
            sl = jnp.concatenate([s_loc[h:h + 1, :], s_loc[h:h + 1, :]], axis=1)
            cst[d * nh + h] = sp * cext + sl * _dot(kwt, vext)

    @pl.when(c == nc - 1)
    def _():
        sf_ref[...] = cst[...]
        mf_ref[...] = mst[...]


def _mlstm_call(qs, kt, vb, gplanes, s0, m0, bsz):
    t = qs.shape[0] // bsz
    L = M_CHUNK
    nc = t // L
    nh = C_HEADS
    hw = nh * HEAD_DIM

    def cidx(d, b, c):
        return b * nc + (c if d == 0 else nc - 1 - c)

    def dir_specs(d):
        return [pl.BlockSpec((L, hw), lambda b, c: (cidx(d, b, c), 0)),
                pl.BlockSpec((1, hw, L), lambda b, c: (cidx(d, b, c), 0, 0)),
                pl.BlockSpec((L, hw), lambda b, c: (cidx(d, b, c), 0)),
                pl.BlockSpec((1, 1, GP_PLANES * nh, L), lambda b, c: (cidx(d, b, c), d, 0, 0))]

    sspec = pl.BlockSpec((2 * nh, HEAD_DIM, 2 * HEAD_DIM), lambda b, c: (b, 0, 0))
    mspec = pl.BlockSpec((2, nh, 128), lambda b, c: (b, 0, 0))
    blocks = (6 * _nbytes((L, hw), BF16) + 2 * _nbytes((GP_PLANES * nh, L), F32)
              + 4 * _nbytes((nh, HEAD_DIM, 2 * HEAD_DIM), F32) + 2 * _nbytes((L, hw), F32))
    scratch = 2 * _nbytes((nh, HEAD_DIM, 2 * HEAD_DIM), F32)
    hshape = jax.ShapeDtypeStruct((bsz * t, hw), F32)
    return pl.pallas_call(
        functools.partial(_mlstm_kernel, nc=nc),
        out_shape=[hshape, hshape, jax.ShapeDtypeStruct(s0.shape, F32), jax.ShapeDtypeStruct(m0.shape, F32)],
        grid=(bsz, nc),
        in_specs=dir_specs(0) + dir_specs(1) + [sspec, mspec],
        out_specs=[pl.BlockSpec((L, hw), lambda b, c: (cidx(0, b, c), 0)),
                   pl.BlockSpec((L, hw), lambda b, c: (cidx(1, b, c), 0)), sspec, mspec],
        scratch_shapes=[pltpu.VMEM((2 * nh, HEAD_DIM, 2 * HEAD_DIM), F32), pltpu.VMEM((2, nh, 128), F32)],
        compiler_params=_params(("parallel", "arbitrary"), blocks, scratch + 16 * 1024 * 1024),
        name="mlstm_scan",
    )(qs, kt, vb, gplanes, qs, kt, vb, gplanes, s0, m0)


def _mlstm_out_kernel(hf_ref, hb_ref, o_ref, g_ref, y_ref):
    hd = HEAD_DIM
    for h in range(C_HEADS):
        cs = slice(h * hd, (h + 1) * hd)
        hn = _head_norm(hf_ref[:, cs] + hb_ref[:, cs], g_ref[:, cs])
        y_ref[:, cs] = (hn * jax.nn.sigmoid(o_ref[:, cs])).astype(y_ref.dtype)


def _mlstm_out_call(hf, hb, p_main, norm_g):
    m, hw = hf.shape
    tr = _pick(m, (256, 128))
    blocks = 3 * _nbytes((tr, hw), F32) + _nbytes((tr, hw), BF16)
    hspec = pl.BlockSpec((tr, hw), lambda i: (i, 0))
    return pl.pallas_call(
        _mlstm_out_kernel,
        out_shape=jax.ShapeDtypeStruct((m, hw), BF16),
        grid=(m // tr,),
        in_specs=[hspec, hspec,
                  pl.BlockSpec((tr, hw), lambda i: (i, COL_CO)),
                  pl.BlockSpec((1, hw), lambda i: (0, 0))],
        out_specs=hspec,
        compiler_params=_params(("parallel",), blocks, 2 * 1024 * 1024),
        name="mlstm_out",
    )(hf, hb, p_main, norm_g.reshape(1, hw))


def _rope_lane_tables(n):
    t = jnp.arange(n)
    inv = ROPE_THETA ** (-jnp.arange(ROPE_PAIRS, dtype=F32) / ROPE_PAIRS)
    row = (t // GRID_W).astype(F32)[:, None] * inv
    col = (t % GRID_W).astype(F32)[:, None] * inv
    cos = jnp.concatenate([jnp.cos(row), jnp.cos(row), jnp.cos(col), jnp.cos(col)], axis=1)
    sin = jnp.concatenate([-jnp.sin(row), jnp.sin(row), -jnp.sin(col), jnp.sin(col)], axis=1)
    return cos, sin


def _split_w_in(w_in):
    aq_end = BRANCH_W
    akv_end = aq_end + 2 * A_KV_HEADS * HEAD_DIM
    main = jnp.concatenate([w_in[..., :aq_end], w_in[..., akv_end:MAIN_W], w_in[..., aq_end:akv_end]], axis=-1)
    gates = jnp.pad(w_in[..., MAIN_W:], ((0, 0), (0, 0), (0, GATE_W - 4 * C_HEADS)))
    return main.astype(BF16), gates.astype(BF16)


def kernel(x, c, ctx, c_ctx, w_mod, b_mod, norm_g, ffn_w1, ffn_w3, ffn_w2, w_in, qk_g, attn_sink, na_relpos,
           mlstm_conv_w, mlstm_conv_b, mlstm_gate_b, mlstm_norm_g, w_gate, b_gate, w_branch, w_out):
    bsz, n, d = x.shape
    lc = ctx.shape[1]
    depth = w_mod.shape[0]
    assert n % GRID_W == 0 and n % M_CHUNK == 0 and lc % M_CHUNK == 0 and n % A_BLOCK == 0

    cs = jnp.zeros((8, d), F32).at[:bsz].set(c).at[bsz].set(c_ctx)
    mods = _mod_call(cs, w_mod, b_mod).reshape(depth, 8, N_MOD, d)
    cos_t, sin_t = _rope_lane_tables(n)

    xl = x.reshape(bsz * n, d)
    xc = ctx.reshape(bsz * lc, d)
    nh = C_HEADS
    wg = w_gate.astype(BF16)
    wb = w_branch.astype(BF16)
    w_main, w_gates = _split_w_in(w_in)
    s_zero = jnp.zeros((bsz * 2 * nh, HEAD_DIM, 2 * HEAD_DIM), F32)
    m_zero = jnp.zeros((bsz * 2, nh, 128), F32)

    def ffn(xs, md, k0, g, layer, sub):
        u = _normmod_call(xs, g, md[:, k0], md[:, k0 + 1])
        h = _up_call(u, ffn_w1, ffn_w3, layer, sub)
        return _down_call(h, ffn_w2, xs, md[:, k0 + 2], 0.5, lead=(layer, sub))

    for i in range(depth):
        last = i == depth - 1
        ml = mods[i, :bsz]
        mc = mods[i, bsz:bsz + 1]
        gate_b = jnp.pad(mlstm_gate_b[i].reshape(1, 4 * nh), ((0, 0), (0, GATE_W - 4 * nh)))
        conv_b = mlstm_conv_b[i].reshape(1, -1)
        bias_tab = _na_bias_table(na_relpos[i])

        xc = ffn(xc, mc, 0, norm_g[i, 0], i, 0)
        xl = ffn(xl, ml, 0, norm_g[i, 0], i, 0)

        uc = _normmod_call(xc, norm_g[i, 1], mc[:, 3], mc[:, 4])
        ul = _normmod_call(xl, norm_g[i, 1], ml[:, 3], ml[:, 4])
        pc = _inproj_call(uc, w_main, i)
        pl_ = _inproj_call(ul, w_main, i)
        gc = _matmul_call(uc, w_gates[i])
        gl = _matmul_call(ul, w_gates[i])

        qa_c, qb_c, kb_c, vb_c, ka_c, va_c = _qkprep_call(pc, qk_g[i], cos_t, sin_t, False, lc)
        qa_l, qb_l, kb_l, vb_l, ka_l, va_l = _qkprep_call(pl_, qk_g[i], cos_t, sin_t, True, n)

        a_l = _attn_a_call(qa_l, ka_l, va_l, ka_c, va_c, attn_sink[i], bsz, True)
        n_l = _attn_b_call(qb_l, kb_l, vb_l, kb_c, vb_c, bias_tab, bsz)
        qc_c, kc_c, vc_c = _cprep_call(pc, mlstm_conv_w[i], conv_b, lc)
        qc_l, kc_l, vc_l = _cprep_call(pl_, mlstm_conv_w[i], conv_b, n)
        hc_f, hc_b, s_c, m_c = _mlstm_call(qc_c, kc_c, vc_c, _gateprep_call(gc, gate_b, lc), s_zero, m_zero, bsz)
        hl_f, hl_b, _, _ = _mlstm_call(qc_l, kc_l, vc_l, _gateprep_call(gl, gate_b, n), s_c, m_c, bsz)
        m_l = _mlstm_out_call(hl_f, hl_b, pl_, mlstm_norm_g[i])

        yl = _merge_call(ul, a_l, n_l, m_l, wg, b_gate, wb, i)
        xl = _down_call(yl, w_out, xl, ml[:, 5], 1.0, lead=(i,))
        xl = ffn(xl, ml, 6, norm_g[i, 2], i, 1)
        if not last:
            a_c = _attn_a_call(qa_c, ka_c, va_c, ka_c, va_c, attn_sink[i], bsz, False)
            n_c = _attn_bx_call(qb_c, kb_c, vb_c, bsz)
            mm_c = _mlstm_out_call(hc_f, hc_b, pc, mlstm_norm_g[i])
            yc = _merge_call(uc, a_c, n_c, mm_c, wg, b_gate, wb, i)
            xc = _down_call(yc, w_out, xc, mc[:, 5], 1.0, lead=(i,))
            xc = ffn(xc, mc, 6, norm_g[i, 2], i, 1)
    return xl.reshape(bsz, n, d)
```

```python
import functools

import numpy as np
import jax
import jax.numpy as jnp
from jax import lax
from jax.experimental import pallas as pl
from jax.experimental.pallas import tpu as pltpu

F32 = jnp.float32
BF16 = jnp.bfloat16

HEAD_DIM = 128
GRID_W = 64
ROPE_PAIRS = HEAD_DIM // 4
ROPE_THETA = 10000.0
A_HEADS = 8
A_KV_HEADS = 2
A_GROUP = A_HEADS // A_KV_HEADS
A_WINDOW = 128
A_BLOCK = 128
B_HEADS = 8
NA_ROWS = 8
NA_COLS = 16
C_HEADS = 8
C_CONV = 5
BRANCH_W = 8 * HEAD_DIM
N_MOD = 9
EPS = 1e-6
NEG = -1e30
ATTN_SCALE = HEAD_DIM ** -0.5
LOG2E = 1.4426950408889634
QK_SCALE_LOG2 = ATTN_SCALE * LOG2E

M_CHUNK = 128
NA_TILE_ROWS = 8
NA_TILE = NA_TILE_ROWS * GRID_W
NA_WIN_ROWS = NA_TILE_ROWS + NA_ROWS
NA_HEADS_PER_STEP = 4
A_QBLOCKS = 2

BF16_SUBLANES = 16
V7X_VMEM_BYTES = 64 * 1024 * 1024
VMEM_CAP = V7X_VMEM_BYTES - 6 * 1024 * 1024

COL_AQ, COL_BQ, COL_BK, COL_BV, COL_CQ, COL_CK, COL_CV, COL_CO = range(8)
MAIN_W = 8 * BRANCH_W + 2 * A_KV_HEADS * HEAD_DIM
GATE_W = 128


def _pick(n, cands):
    for c in cands:
        if n % c == 0:
            return c
    raise ValueError(f"no tile in {cands} divides {n}")


def _params(sem, block_bytes, temp_bytes=0):
    limit = 2 * block_bytes + temp_bytes + 4 * 1024 * 1024
    return pltpu.CompilerParams(dimension_semantics=sem,
                                vmem_limit_bytes=int(min(max(limit, 16 * 1024 * 1024), VMEM_CAP)))


def _nbytes(shape, dtype):
    return int(np.prod(shape)) * jnp.dtype(dtype).itemsize


def _dot(a, b):
    return jnp.dot(a, b, preferred_element_type=F32)


def _dot_nt(a, b):
    return lax.dot_general(a, b, (((1,), (1,)), ((), ())), preferred_element_type=F32)


def _dot_tn(a, b):
    return lax.dot_general(a, b, (((0,), (0,)), ((), ())), preferred_element_type=F32)


def _silu(x):
    return x * jax.nn.sigmoid(x)


def _mod_kernel(c_ref, w_ref, b_ref, o_ref):
    a = _silu(c_ref[...]).astype(BF16)
    o_ref[0] = _dot(a, w_ref[0].astype(BF16)) + b_ref[0]


def _mod_call(cs, w_mod, b_mod):
    depth, d, nd = w_mod.shape
    tn = _pick(nd, (1024, 512, 256, 128))
    rows = cs.shape[0]
    blocks = _nbytes((d, tn), F32) + _nbytes((rows, d), F32) + _nbytes((rows, tn), F32)
    return pl.pallas_call(
        _mod_kernel,
        out_shape=jax.ShapeDtypeStruct((depth, rows, nd), F32),
        grid=(depth, nd // tn),
        in_specs=[pl.BlockSpec((rows, d), lambda l, j: (0, 0)),
                  pl.BlockSpec((1, d, tn), lambda l, j: (l, 0, j)),
                  pl.BlockSpec((1, 1, tn), lambda l, j: (l, 0, j))],
        out_specs=pl.BlockSpec((1, rows, tn), lambda l, j: (l, 0, j)),
        compiler_params=_params(("parallel", "parallel"), blocks, _nbytes((d, tn), BF16)),
        name="mod_vectors",
    )(cs, w_mod, b_mod.reshape(depth, 1, nd))


def _normmod_kernel(x_ref, g_ref, shift_ref, scale_ref, o_ref):
    x = x_ref[...]
    ms = jnp.mean(x * x, axis=-1, keepdims=True)
    y = x * lax.rsqrt(ms + EPS) * g_ref[...]
    o_ref[...] = (y * (1.0 + scale_ref[0]) + shift_ref[0]).astype(o_ref.dtype)


def _normmod_call(x, g, shift, scale):
    m, d = x.shape
    groups = shift.shape[0]
    tr = _pick(m // groups, (512, 256, 128, 64, 8))
    per = (m // groups) // tr
    blocks = _nbytes((tr, d), F32) + _nbytes((tr, d), BF16) + 3 * _nbytes((1, d), F32)
    return pl.pallas_call(
        _normmod_kernel,
        out_shape=jax.ShapeDtypeStruct((m, d), BF16),
        grid=(m // tr,),
        in_specs=[pl.BlockSpec((tr, d), lambda i: (i, 0)),
                  pl.BlockSpec((1, d), lambda i: (0, 0)),
                  pl.BlockSpec((1, 1, d), lambda i: (i // per, 0, 0)),
                  pl.BlockSpec((1, 1, d), lambda i: (i // per, 0, 0))],
        out_specs=pl.BlockSpec((tr, d), lambda i: (i, 0)),
        compiler_params=_params(("parallel",), blocks, 2 * _nbytes((tr, d), F32)),
        name="norm_modulate",
    )(x, g.reshape(1, d), shift.reshape(groups, 1, d), scale.reshape(groups, 1, d))


def _up_kernel(u_ref, w1_ref, w3_ref, o_ref):
    u = u_ref[...]
    h1 = _dot(u, w1_ref[...].astype(BF16))
    h3 = _dot(u, w3_ref[...].astype(BF16))
    o_ref[...] = (_silu(h1) * h3).astype(o_ref.dtype)


def _up_call(u, w1, w3, layer, sub):
    m, d = u.shape
    f = w1.shape[-1]
    tm = _pick(m, (1024, 512, 256))
    tf = _pick(f, (256, 128))
    wspec = pl.BlockSpec((None, None, d, tf), lambda i, j: (layer, sub, 0, j))
    blocks = _nbytes((tm, d), BF16) + 2 * _nbytes((d, tf), F32) + _nbytes((tm, tf), BF16)
    return pl.pallas_call(
        _up_kernel,
        out_shape=jax.ShapeDtypeStruct((m, f), BF16),
        grid=(m // tm, f // tf),
        in_specs=[pl.BlockSpec((tm, d), lambda i, j: (i, 0)), wspec, wspec],
        out_specs=pl.BlockSpec((tm, tf), lambda i, j: (i, j)),
        compiler_params=_params(("parallel", "parallel"), blocks,
                                2 * _nbytes((d, tf), BF16) + 4 * _nbytes((tm, tf), F32)),
        name="ffn_up",
    )(u, w1, w3)


def _down_kernel(a_hbm, w_ref, x_ref, gate_ref, o_ref, abuf, sem, *, coef, ni, nj, tm):
    i = pl.program_id(0)
    j = pl.program_id(1)
    slot = i % 2
    ch = tm // nj

    def chunk_copy(blk, c, slot_):
        return pltpu.make_async_copy(a_hbm.at[pl.ds(blk * tm + c * ch, ch), :],
                                     abuf.at[slot_, pl.ds(c * ch, ch), :], sem.at[slot_])

    @pl.when((i == 0) & (j == 0))
    def _():
        for c in range(nj):
            chunk_copy(0, c, 0).start()

    @pl.when(j == 0)
    def _():
        for c in range(nj):
            chunk_copy(i, c, slot).wait()

    @pl.when(i + 1 < ni)
    def _():
        chunk_copy(i + 1, j, 1 - slot).start()

    acc = _dot(abuf[slot], w_ref[...].astype(BF16))
    o_ref[...] = x_ref[...] + (coef * gate_ref[0]) * acc


def _down_call(a, w, x, gate, coef, lead=()):
    m, k = a.shape
    d = w.shape[-1]
    groups = gate.shape[0]
    tm = _pick(m // groups, (1024, 512, 256))
    tn = _pick(d, (512, 256, 128) if k <= 4096 else (256, 128))
    per = (m // groups) // tm
    ni, nj = m // tm, d // tn
    assert tm % nj == 0 and (tm // nj) % BF16_SUBLANES == 0
    blocks = _nbytes((k, tn), w.dtype) + 2 * _nbytes((tm, tn), F32) + _nbytes((1, tn), F32)
    temps = (2 * _nbytes((tm, k), BF16) + 2 * _nbytes((tm, tn), F32)
             + (_nbytes((k, tn), BF16) if w.dtype != BF16 else 0))
    wspec = pl.BlockSpec((None,) * len(lead) + (k, tn), lambda i, j: tuple(lead) + (0, j))
    return pl.pallas_call(
        functools.partial(_down_kernel, coef=coef, ni=ni, nj=nj, tm=tm),
        out_shape=jax.ShapeDtypeStruct((m, d), F32),
        grid=(ni, nj),
        in_specs=[pl.BlockSpec(memory_space=pl.ANY),
                  wspec,
                  pl.BlockSpec((tm, tn), lambda i, j: (i, j)),
                  pl.BlockSpec((1, 1, tn), lambda i, j: (i // per, 0, j))],
        out_specs=pl.BlockSpec((tm, tn), lambda i, j: (i, j)),
        scratch_shapes=[pltpu.VMEM((2, tm, k), BF16), pltpu.SemaphoreType.DMA((2,))],
        compiler_params=_params(("arbitrary", "arbitrary"), blocks, temps),
        name="proj_residual",
    )(a, w, x, gate.reshape(groups, 1, d))


def _matmul_kernel(a_ref, w_ref, o_ref):
    o_ref[...] = _dot(a_ref[...], w_ref[...].astype(BF16)).astype(o_ref.dtype)


IN_TN = 2 * A_KV_HEADS * HEAD_DIM


def _inproj_call(u, w_main, layer):
    m, d = u.shape
    tm = _pick(m, (1024, 512, 256))
    tn = IN_TN
    blocks = _nbytes((tm, d), BF16) + _nbytes((d, tn), BF16) + _nbytes((tm, tn), F32)
    return pl.pallas_call(
        _matmul_kernel,
        out_shape=jax.ShapeDtypeStruct((m, MAIN_W), F32),
        grid=(m // tm, MAIN_W // tn),
        in_specs=[pl.BlockSpec((tm, d), lambda i, j: (i, 0)),
                  pl.BlockSpec((None, d, tn), lambda i, j: (layer, 0, j))],
        out_specs=pl.BlockSpec((tm, tn), lambda i, j: (i, j)),
        compiler_params=_params(("parallel", "parallel"), blocks, 2 * _nbytes((tm, tn), F32)),
        name="in_proj",
    )(u, w_main)


def _matmul_call(a, w, out_dtype=F32):
    m, k = a.shape
    n = w.shape[1]
    tm = _pick(m, (1024, 512, 256))
    tn = _pick(n, (512, 256, 128))
    blocks = _nbytes((tm, k), BF16) + _nbytes((k, tn), BF16) + _nbytes((tm, tn), out_dtype)
    return pl.pallas_call(
        _matmul_kernel,
        out_shape=jax.ShapeDtypeStruct((m, n), out_dtype),
        grid=(m // tm, n // tn),
        in_specs=[pl.BlockSpec((tm, k), lambda i, j: (i, 0)),
                  pl.BlockSpec((k, tn), lambda i, j: (0, j))],
        out_specs=pl.BlockSpec((tm, tn), lambda i, j: (i, j)),
        compiler_params=_params(("parallel", "parallel"), blocks, _nbytes((tm, tn), F32)),
        name="in_proj",
    )(a, w)


def _merge_kernel(u_ref, oa_ref, ob_ref, oc_ref, wg_ref, bg_ref, wb_ref, y_ref):
    u = u_ref[...]
    acc = None
    for j, br_ref in enumerate((oa_ref, ob_ref, oc_ref)):
        gate = jax.nn.sigmoid(_dot(u, wg_ref[j]) + bg_ref[j])
        term = gate * _dot(br_ref[...], wb_ref[j])
        acc = term if acc is None else acc + term
    y_ref[...] = acc.astype(y_ref.dtype)


def _merge_call(u, oa, ob, oc, wg, bg, wb, layer):
    m, d = u.shape
    bw = oa.shape[1]
    tm = _pick(m, (1024, 512, 256))
    tn = _pick(d, (256, 128))
    blocks = (_nbytes((tm, d), BF16) + 3 * _nbytes((tm, bw), BF16) + 3 * _nbytes((d, tn), BF16)
              + 3 * _nbytes((bw, tn), BF16) + _nbytes((tm, tn), BF16))
    return pl.pallas_call(
        _merge_kernel,
        out_shape=jax.ShapeDtypeStruct((m, d), BF16),
        grid=(m // tm, d // tn),
        in_specs=[pl.BlockSpec((tm, d), lambda i, j: (i, 0)),
                  pl.BlockSpec((tm, bw), lambda i, j: (i, 0)),
                  pl.BlockSpec((tm, bw), lambda i, j: (i, 0)),
                  pl.BlockSpec((tm, bw), lambda i, j: (i, 0)),
                  pl.BlockSpec((None, 3, d, tn), lambda i, j: (layer, 0, 0, j)),
                  pl.BlockSpec((None, 3, 1, tn), lambda i, j: (layer, 0, 0, j)),
                  pl.BlockSpec((None, 3, bw, tn), lambda i, j: (layer, 0, 0, j))],
        out_specs=pl.BlockSpec((tm, tn), lambda i, j: (i, j)),
        compiler_params=_params(("parallel", "parallel"), blocks, 10 * _nbytes((tm, tn), F32)),
        name="gated_merge",
    )(u, oa, ob, oc, wg, bg.reshape(bg.shape[0], 3, 1, d), wb)


def _head_norm(x, g):
    ms = jnp.mean(x * x, axis=-1, keepdims=True)
    return x * lax.rsqrt(ms + EPS) * g


def _rope(y, cos, sin_signed):
    lane = lax.broadcasted_iota(jnp.int32, y.shape, 1)
    partner = jnp.where((lane % 64) < ROPE_PAIRS, pltpu.roll(y, HEAD_DIM - ROPE_PAIRS, axis=1),
                        pltpu.roll(y, ROPE_PAIRS, axis=1))
    return y * cos + partner * sin_signed


def _qkprep_kernel(p4_ref, pa_ref, g_ref, cos_ref, sin_ref,
                   qa_ref, qb_ref, kb_ref, vb_ref, ka_ref, va_ref, *, rope):
    hd = HEAD_DIM
    if rope:
        cos = cos_ref[...]
        sin = sin_ref[...]
    for h in range(A_HEADS):
        y = _head_norm(p4_ref[:, h * hd:(h + 1) * hd], g_ref[0:1, :])
        if rope:
            y = _rope(y, cos, sin)
        qa_ref[:, h * hd:(h + 1) * hd] = y.astype(BF16)
    for h in range(B_HEADS):
        c0 = BRANCH_W + h * hd
        qb_ref[:, h * hd:(h + 1) * hd] = _head_norm(p4_ref[:, c0:c0 + hd], g_ref[2:3, :]).astype(BF16)
        c0 = 2 * BRANCH_W + h * hd
        kb_ref[:, h * hd:(h + 1) * hd] = _head_norm(p4_ref[:, c0:c0 + hd], g_ref[3:4, :]).astype(BF16)
    vb_ref[...] = p4_ref[:, 3 * BRANCH_W:4 * BRANCH_W].astype(BF16)
    for h in range(A_KV_HEADS):
        y = _head_norm(pa_ref[:, h * hd:(h + 1) * hd], g_ref[1:2, :])
        if rope:
            y = _rope(y, cos, sin)
        ka_ref[:, h * hd:(h + 1) * hd] = y.astype(BF16)
    kvw = A_KV_HEADS * hd
    va_ref[...] = pa_ref[:, kvw:2 * kvw].astype(BF16)


def _qkprep_call(p_main, qk_g, cos_t, sin_t, rope, seq):
    m = p_main.shape[0]
    tr = _pick(seq, (256, 128))
    per = seq // tr
    kvw = A_KV_HEADS * HEAD_DIM
    w4 = 4 * BRANCH_W
    blocks = (_nbytes((tr, w4), F32) + _nbytes((tr, 2 * kvw), F32) + 2 * _nbytes((tr, HEAD_DIM), F32)
              + _nbytes((tr, w4), BF16) + _nbytes((tr, 2 * kvw), BF16))
    outs = [jax.ShapeDtypeStruct((m, BRANCH_W), BF16)] * 4 + [jax.ShapeDtypeStruct((m, kvw), BF16)] * 2
    return pl.pallas_call(
        functools.partial(_qkprep_kernel, rope=rope),
        out_shape=outs,
        grid=(m // tr,),
        in_specs=[pl.BlockSpec((tr, w4), lambda i: (i, 0)),
                  pl.BlockSpec((tr, 2 * kvw), lambda i: (i, (8 * BRANCH_W) // (2 * kvw))),
                  pl.BlockSpec((4, HEAD_DIM), lambda i: (0, 0)),
                  pl.BlockSpec((tr, HEAD_DIM), lambda i: (i % per, 0)),
                  pl.BlockSpec((tr, HEAD_DIM), lambda i: (i % per, 0))],
        out_specs=[pl.BlockSpec((tr, BRANCH_W), lambda i: (i, 0))] * 4
                  + [pl.BlockSpec((tr, kvw), lambda i: (i, 0))] * 2,
        compiler_params=_params(("parallel",), blocks, 4 * _nbytes((tr, HEAD_DIM), F32)),
        name="qk_prep",
    )(p_main, p_main, qk_g, cos_t, sin_t)


def _attn_a_kernel(sink_ref, q_ref, *rest, lc, window):
    nwin = A_QBLOCKS + 2 if window else 0
    nmask = A_QBLOCKS if window else 0
    k_refs = rest[:nwin]
    v_refs = rest[nwin:2 * nwin]
    mask_refs = rest[2 * nwin:2 * nwin + nmask]
    kx_ref, vx_ref, o_ref = rest[2 * nwin + nmask:]
    hd = HEAD_DIM
    blk = A_BLOCK
    gw = A_GROUP * hd
    keys = lc + 3 * blk if window else lc
    ones = jnp.ones((keys, hd), BF16)
    for sb in range(A_QBLOCKS):
        rows = slice(sb * blk, (sb + 1) * blk)
        for hk in range(A_KV_HEADS):
            cs = slice(hk * hd, (hk + 1) * hd)
            if window:
                kall = jnp.concatenate([kx_ref[:, cs]] + [r[:, cs] for r in k_refs[sb:sb + 3]], axis=0)
                vall = jnp.concatenate([vx_ref[:, cs]] + [r[:, cs] for r in v_refs[sb:sb + 3]], axis=0)
            else:
                kall = kx_ref[:, cs]
                vall = vx_ref[:, cs]
            q4 = jnp.concatenate([q_ref[rows, hk * gw + g * hd:hk * gw + (g + 1) * hd] for g in range(A_GROUP)],
                                 axis=0)
            s = _dot_nt(q4, kall) * QK_SCALE_LOG2
            if window:
                s = s + mask_refs[sb][0]
            snk = jnp.concatenate(
                [jnp.full((blk, 1), sink_ref[hk * A_GROUP + g] * LOG2E, F32) for g in range(A_GROUP)], axis=0)
            mx = jnp.maximum(jnp.max(s, axis=-1, keepdims=True), snk)
            e = jnp.exp2(s - mx).astype(BF16)
            both = _dot(e, jnp.concatenate([vall, ones], axis=1))
            o = both[:, 0:hd] * (1.0 / (both[:, hd:2 * hd] + jnp.exp2(snk - mx)))
            for g in range(A_GROUP):
                o_ref[rows, hk * gw + g * hd:hk * gw + (g + 1) * hd] = (
                    o[g * blk:(g + 1) * blk].astype(o_ref.dtype))


def _gqa_mask_table(lc, nb):
    blk = A_BLOCK
    r = np.arange(blk)[:, None]
    kc = np.arange(3 * blk)[None, :]
    band = np.abs(r + blk - kc) <= A_WINDOW
    tabs = []
    for first, last in ((True, False), (False, False), (False, True)):
        ok = band & ~(first & (kc < blk)) & ~(last & (kc >= 2 * blk))
        win = np.where(ok, 0.0, NEG).astype(np.float32)
        tabs.append(np.tile(np.concatenate([np.zeros((blk, lc), np.float32), win], axis=1), (A_GROUP, 1)))
    assert nb >= 2
    return np.stack(tabs)


def _attn_a_call(qa, ka, va, kx, vx, sink, bsz, window):
    n = qa.shape[0] // bsz
    lc = kx.shape[0] // bsz
    blk = A_BLOCK
    nb = n // blk
    qb = A_QBLOCKS
    assert nb % qb == 0
    nt = nb // qb
    kvw = A_KV_HEADS * HEAD_DIM
    qw = A_HEADS * HEAD_DIM
    qspec = pl.BlockSpec((qb * blk, qw), lambda b, t: (b * nt + t, 0))
    xspec = pl.BlockSpec((lc, kvw), lambda b, t: (b, 0))
    sspec = pl.BlockSpec(memory_space=pltpu.SMEM)
    if window:
        def kspec(off):
            return pl.BlockSpec((blk, kvw), lambda b, t: (b * nb + jnp.clip(t * qb + off, 0, nb - 1), 0))

        def mspec(sb):
            def variant(b, t):
                i = t * qb + sb
                return (jnp.where(i == 0, 0, jnp.where(i == nb - 1, 2, 1)), 0, 0)
            return pl.BlockSpec((1, A_GROUP * blk, lc + 3 * blk), variant)

        kspecs = [kspec(off) for off in range(-1, qb + 1)]
        mspecs = [mspec(sb) for sb in range(qb)]
        mask = jnp.asarray(_gqa_mask_table(lc, nb))
        in_specs = [sspec, qspec] + kspecs + kspecs + mspecs + [xspec, xspec]
        args = (sink, qa) + (ka,) * len(kspecs) + (va,) * len(kspecs) + (mask,) * qb + (kx, vx)
        keys = lc + 3 * blk
    else:
        in_specs = [sspec, qspec, xspec, xspec]
        args = (sink, qa, kx, vx)
        keys = lc
    blocks = (2 * _nbytes((qb * blk, qw), BF16) + 2 * _nbytes((keys + qb * blk, kvw), BF16)
              + (qb * _nbytes((A_GROUP * blk, keys), F32) if window else 0))
    return pl.pallas_call(
        functools.partial(_attn_a_kernel, lc=lc, window=window),
        out_shape=jax.ShapeDtypeStruct(qa.shape, BF16),
        grid=(bsz, nt),
        in_specs=in_specs,
        out_specs=qspec,
        compiler_params=_params(("parallel", "parallel"), blocks,
                                10 * qb * _nbytes((A_GROUP * blk, keys), F32)),
        name="windowed_gqa" if window else "context_gqa",
    )(*args)


def _softmax2_pv(s1, v1, s2, v2):
    hd = v1.shape[1]
    mx = jnp.maximum(jnp.max(s1, axis=-1, keepdims=True), jnp.max(s2, axis=-1, keepdims=True))
    e1 = jnp.exp2(s1 - mx).astype(BF16)
    e2 = jnp.exp2(s2 - mx).astype(BF16)
    v1e = jnp.concatenate([v1, jnp.ones(v1.shape, BF16)], axis=1)
    v2e = jnp.concatenate([v2, jnp.ones(v2.shape, BF16)], axis=1)
    both = _dot(e1, v1e) + _dot(e2, v2e)
    return both[:, 0:hd] * (1.0 / both[:, hd:2 * hd])


def _attn_b_kernel(q_ref, k0_ref, k1_ref, k2_ref, k3_ref, v0_ref, v1_ref, v2_ref, v3_ref, kx_ref, vx_ref,
                   bias_ref, o_ref):
    hd = HEAD_DIM
    for h in range(NA_HEADS_PER_STEP):
        cs = slice(h * hd, (h + 1) * hd)
        kwin = jnp.concatenate([k0_ref[:, cs], k1_ref[:, cs], k2_ref[:, cs], k3_ref[:, cs]], axis=0)
        vwin = jnp.concatenate([v0_ref[:, cs], v1_ref[:, cs], v2_ref[:, cs], v3_ref[:, cs]], axis=0)
        q = q_ref[:, cs]
        s_nb = _dot_nt(q, kwin) * QK_SCALE_LOG2 + bias_ref[h, 0]
        s_cx = _dot_nt(q, kx_ref[:, cs]) * QK_SCALE_LOG2
        o_ref[:, cs] = _softmax2_pv(s_nb, vwin, s_cx, vx_ref[:, cs]).astype(o_ref.dtype)


def _attn_bx_kernel(q_ref, kx_ref, vx_ref, o_ref):
    hd = HEAD_DIM
    for h in range(B_HEADS):
        cs = slice(h * hd, (h + 1) * hd)
        s = _dot_nt(q_ref[:, cs], kx_ref[:, cs]) * QK_SCALE_LOG2
        mx = jnp.max(s, axis=-1, keepdims=True)
        e = jnp.exp2(s - mx)
        inv = 1.0 / jnp.sum(e, axis=-1, keepdims=True)
        o_ref[:, cs] = (_dot(e.astype(BF16), vx_ref[:, cs]) * inv).astype(o_ref.dtype)


def _na_bias_kernel(rel_ref, o_ref):
    h = pl.program_id(0)
    pos = pl.program_id(1)
    nrel_r = 2 * NA_ROWS - 1
    nrel_c = 2 * NA_COLS - 1
    qi = lax.broadcasted_iota(jnp.int32, (GRID_W, GRID_W), 0)
    ki = lax.broadcasted_iota(jnp.int32, (GRID_W, GRID_W), 1)
    dc = jnp.clip(ki - qi, -(NA_COLS - 1), NA_COLS - 1) + NA_COLS - 1
    start = jnp.clip(qi - NA_COLS // 2, 0, GRID_W - NA_COLS)
    col_in = (ki >= start) & (ki < start + NA_COLS)
    planes = []
    for dr in range(nrel_r):
        acc = jnp.zeros((GRID_W, GRID_W), F32)
        for dcv in range(nrel_c):
            acc = jnp.where(dc == dcv, rel_ref[(h * nrel_r + dr) * nrel_c + dcv], acc)
        planes.append(jnp.where(col_in, acc * LOG2E, NEG))
    masked = jnp.full((GRID_W, GRID_W), NEG, F32)
    half = NA_ROWS // 2
    first_key = {0: lambda j: max(j, half), 1: lambda j: j, 2: lambda j: min(j, half)}
    for p, lo_of in first_key.items():
        @pl.when(pos == p)
        def _(lo_of=lo_of):
            for j in range(NA_TILE_ROWS):
                lo = lo_of(j)
                for jj in range(NA_WIN_ROWS):
                    blk = planes[jj - j + half - 1] if lo <= jj < lo + NA_ROWS else masked
                    o_ref[0, 0, j * GRID_W:(j + 1) * GRID_W, jj * GRID_W:(jj + 1) * GRID_W] = blk


def _na_bias_table(relpos):
    nh = relpos.shape[0]
    shape = (nh, 3, NA_TILE, NA_WIN_ROWS * GRID_W)
    return pl.pallas_call(
        _na_bias_kernel,
        out_shape=jax.ShapeDtypeStruct(shape, F32),
        grid=(nh, 3),
        in_specs=[pl.BlockSpec(memory_space=pltpu.SMEM)],
        out_specs=pl.BlockSpec((1, 1) + shape[2:], lambda h, p: (h, p, 0, 0)),
        compiler_params=_params(("parallel", "parallel"), _nbytes(shape[2:], F32)),
        name="na_bias_table",
    )(relpos.reshape(-1))


def _attn_b_call(qb, kb, vb, kx, vx, bias, bsz):
    n = qb.shape[0] // bsz
    lc = kx.shape[0] // bsz
    rows = n // GRID_W
    assert rows % NA_TILE_ROWS == 0 and rows >= 2 * NA_TILE_ROWS
    nt = rows // NA_TILE_ROWS
    hp = NA_HEADS_PER_STEP
    hd = hp * HEAD_DIM
    sub = NA_TILE // 2
    qspec = pl.BlockSpec((NA_TILE, hd), lambda b, h, t: (b * nt + t, h))

    def kspec(s):
        return pl.BlockSpec((sub, hd),
                            lambda b, h, t: (jnp.clip(2 * t + s, 0, 2 * nt - 1) + 2 * b * nt, h))

    kspecs = [kspec(s) for s in (-1, 0, 1, 2)]
    xspec = pl.BlockSpec((lc, hd), lambda b, h, t: (b, h))
    bspec = pl.BlockSpec((hp, 1) + bias.shape[2:],
                         lambda b, h, t: (h, jnp.where(t == 0, 0, jnp.where(t == nt - 1, 2, 1)), 0, 0))
    blocks = (2 * _nbytes((NA_TILE, hd), BF16) + 8 * _nbytes((sub, hd), BF16) + 2 * _nbytes((lc, hd), BF16)
              + hp * _nbytes(bias.shape[2:], F32))
    return pl.pallas_call(
        _attn_b_kernel,
        out_shape=jax.ShapeDtypeStruct(qb.shape, BF16),
        grid=(bsz, B_HEADS // hp, nt),
        in_specs=[qspec] + kspecs + kspecs + [xspec, xspec, bspec],
        out_specs=qspec,
        compiler_params=_params(("parallel", "parallel", "parallel"), blocks,
                                5 * hp * _nbytes((NA_TILE, NA_WIN_ROWS * GRID_W + lc), F32)),
        name="neighbourhood_attn",
    )(qb, kb, kb, kb, kb, vb, vb, vb, vb, kx, vx, bias)


def _attn_bx_call(qx, kx, vx, bsz):
    lc = qx.shape[0] // bsz
    w = qx.shape[1]
    spec = pl.BlockSpec((lc, w), lambda b: (b, 0))
    return pl.pallas_call(
        _attn_bx_kernel,
        out_shape=jax.ShapeDtypeStruct(qx.shape, BF16),
        grid=(bsz,),
        in_specs=[spec, spec, spec],
        out_specs=spec,
        compiler_params=_params(("parallel",), 4 * _nbytes((lc, w), BF16), 4 * 1024 * 1024),
        name="context_full_attn",
    )(qx, kx, vx)


def _log_sigmoid(x):
    return -(jnp.maximum(-x, 0.0) + jnp.log1p(jnp.exp(-jnp.abs(x))))


CONV_HALO = 8


def _cprep_kernel(q_ref, k_ref, v_ref, qlo_ref, qhi_ref, klo_ref, khi_ref, cw_ref, cb_ref,
                  qs_ref, kt_ref, vb_ref, ext, *, per):
    pos = pl.program_id(0) % per
    tr = q_ref.shape[0]
    hw = q_ref.shape[1]
    halo = CONV_HALO
    pad = C_CONV // 2

    def conv_silu(x_ref, lo_ref, hi_ref, col0):
        ext[0:halo, :] = jnp.where(pos > 0, lo_ref[...], 0.0)
        ext[halo:halo + tr, :] = x_ref[...]
        ext[halo + tr:2 * halo + tr, :] = jnp.where(pos < per - 1, hi_ref[...], 0.0)
        out = cb_ref[:, col0:col0 + hw]
        for j in range(C_CONV):
            out = out + cw_ref[j:j + 1, col0:col0 + hw] * ext[halo - pad + j:halo - pad + j + tr, :]
        return _silu(out)

    qs_ref[...] = (conv_silu(q_ref, qlo_ref, qhi_ref, 0) * ATTN_SCALE).astype(BF16)
    kc = conv_silu(k_ref, klo_ref, khi_ref, hw)
    for ck in range(tr // M_CHUNK):
        for h in range(C_HEADS):
            blk = kc[ck * M_CHUNK:(ck + 1) * M_CHUNK, h * HEAD_DIM:(h + 1) * HEAD_DIM]
            kt_ref[ck, h * HEAD_DIM:(h + 1) * HEAD_DIM, :] = blk.T.astype(BF16)
    vb_ref[...] = v_ref[...].astype(BF16)


def _cprep_call(p_main, conv_w, conv_b, seq):
    m = p_main.shape[0]
    hw = C_HEADS * HEAD_DIM
    assert M_CHUNK == HEAD_DIM
    tr = _pick(seq, (256, 128))
    per = seq // tr
    halo = CONV_HALO
    hb = tr // halo
    nblk = m // halo
    cpb = tr // M_CHUNK

    def main(col):
        return pl.BlockSpec((tr, hw), lambda i: (i, col))

    def lo(col):
        return pl.BlockSpec((halo, hw), lambda i: (jnp.maximum(i * hb - 1, 0), col))

    def hi(col):
        return pl.BlockSpec((halo, hw), lambda i: (jnp.minimum((i + 1) * hb, nblk - 1), col))

    blocks = (3 * _nbytes((tr, hw), F32) + 4 * _nbytes((halo, hw), F32) + _nbytes((C_CONV + 1, 2 * hw), F32)
              + 3 * _nbytes((tr, hw), BF16))
    scratch = _nbytes((tr + 2 * halo, hw), F32)
    out = jax.ShapeDtypeStruct((m, hw), BF16)
    out_t = jax.ShapeDtypeStruct((m // M_CHUNK, hw, M_CHUNK), BF16)
    ospec = pl.BlockSpec((tr, hw), lambda i: (i, 0))
    tspec = pl.BlockSpec((cpb, hw, M_CHUNK), lambda i: (i, 0, 0))
    return pl.pallas_call(
        functools.partial(_cprep_kernel, per=per),
        out_shape=[out, out_t, out],
        grid=(m // tr,),
        in_specs=[main(COL_CQ), main(COL_CK), main(COL_CV), lo(COL_CQ), hi(COL_CQ), lo(COL_CK), hi(COL_CK),
                  pl.BlockSpec((C_CONV, 2 * hw), lambda i: (0, 0)),
                  pl.BlockSpec((1, 2 * hw), lambda i: (0, 0))],
        out_specs=[ospec, tspec, ospec],
        scratch_shapes=[pltpu.VMEM((tr + 2 * halo, hw), F32)],
        compiler_params=_params(("parallel",), blocks, scratch + 4 * _nbytes((tr, hw), F32)),
        name="mlstm_prep",
    )(p_main, p_main, p_main, p_main, p_main, p_main, p_main, conv_w, conv_b)


GP_PLANES = 6


def _gateprep_kernel(g_ref, gb_ref, o_ref):
    nh = C_HEADS
    L = M_CHUNK
    lane = lax.broadcasted_iota(jnp.int32, (nh, L), 1)
    for ck in range(g_ref.shape[0] // L):
        gt = (g_ref[ck * L:(ck + 1) * L, :] + gb_ref[...]).T
        for d in range(2):
            ig = gt[2 * d * nh:(2 * d + 1) * nh]
            logf = _log_sigmoid(gt[(2 * d + 1) * nh:(2 * d + 2) * nh])
            pre = logf
            sft = 1
            while sft < L:
                pre = pre + jnp.where(lane >= sft, pltpu.roll(pre, sft, axis=1), 0.0)
                sft *= 2
            total = jnp.broadcast_to(pre[:, L - 1:L], (nh, L))
            bcum = pre if d == 0 else total - pre + logf
            r_row = ig - bcum
            rmax = r_row
            sft = 1
            while sft < L:
                if d == 0:
                    rmax = jnp.maximum(rmax, jnp.where(lane >= sft, pltpu.roll(rmax, sft, axis=1), NEG))
                else:
                    rmax = jnp.maximum(rmax, jnp.where(lane < L - sft, pltpu.roll(rmax, L - sft, axis=1), NEG))
                sft *= 2
            r_top = jnp.broadcast_to(rmax[:, L - 1:L] if d == 0 else rmax[:, 0:1], (nh, L))
            w_row = jnp.exp(r_row - r_top)
            for k, plane in enumerate((r_row, rmax, bcum, total, w_row, r_top)):
                o_ref[ck, d, k * nh:(k + 1) * nh, :] = plane


def _gateprep_call(gates, gate_b, seq):
    m = gates.shape[0]
    L = M_CHUNK
    tr = _pick(seq, (1024, 512, 256, 128))
    shape = (m // L, 2, GP_PLANES * C_HEADS, L)
    blocks = _nbytes((tr, GATE_W), F32) + _nbytes((tr // L,) + shape[1:], F32)
    return pl.pallas_call(
        _gateprep_kernel,
        out_shape=jax.ShapeDtypeStruct(shape, F32),
        grid=(m // tr,),
        in_specs=[pl.BlockSpec((tr, GATE_W), lambda i: (i, 0)),
                  pl.BlockSpec((1, GATE_W), lambda i: (0, 0))],
        out_specs=pl.BlockSpec((tr // L,) + shape[1:], lambda i: (i, 0, 0, 0)),
        compiler_params=_params(("parallel",), blocks, 2 * 1024 * 1024),
        name="mlstm_gate_prep",
    )(gates, gate_b)


def _mlstm_kernel(qf_ref, ktf_ref, vf_ref, gpf_ref, qb_ref, ktb_ref, vb_ref, gpb_ref, s0_ref, m0_ref,
                  hf_ref, hb_ref, sf_ref, mf_ref, cst, mst, *, nc):
    c = pl.program_id(1)
    L = M_CHUNK
    hd = HEAD_DIM
    nh = C_HEADS

    @pl.when(c == 0)
    def _():
        cst[...] = s0_ref[...]
        mst[...] = m0_ref[...]

    ii = lax.broadcasted_iota(jnp.int32, (L, L), 0)
    jj = lax.broadcasted_iota(jnp.int32, (L, L), 1)
    ones = jnp.ones((L, hd), BF16)
    dirs = ((qf_ref, ktf_ref, vf_ref, gpf_ref, hf_ref, ii >= jj), (qb_ref, ktb_ref, vb_ref, gpb_ref, hb_ref, ii <= jj))
    for d, (q_ref, kt_ref, v_ref, gp_ref, h_ref, causal) in enumerate(dirs):
        r_row, rmax, bcum, total, w_row, r_top = (gp_ref[0, 0, k * nh:(k + 1) * nh, :] for k in range(GP_PLANES))
        m0 = mst[d]
        big_m = jnp.maximum(rmax, m0)
        si_row = jnp.exp(m0 - big_m)
        nrm_row = jnp.exp(-(bcum + big_m))
        m_loc = total + r_top
        m_new = jnp.maximum(total + m0, m_loc)
        s_prev = jnp.exp(total + m0 - m_new)
        s_loc = jnp.exp(m_loc - m_new)
        mst[d] = m_new
        xt = jnp.concatenate([big_m, si_row, nrm_row, jnp.zeros((L - 3 * nh, L), F32)], axis=0).T
        for h in range(nh):
            cs = slice(h * hd, (h + 1) * hd)
            qs = q_ref[:, cs]
            kt = kt_ref[0, cs, :]
            vext = jnp.concatenate([v_ref[:, cs], ones], axis=1)
            cext = cst[d * nh + h]
            expo = jnp.where(causal, r_row[h:h + 1, :] - xt[:, h:h + 1], NEG)
            sw = jnp.exp(expo) * _dot(qs, kt)
            si = jnp.broadcast_to(xt[:, nh + h:nh + h + 1], (L, 2 * hd))
            both = _dot(sw.astype(BF16), vext) + si * _dot(qs, cext.astype(BF16))
            nrm = jnp.broadcast_to(xt[:, 2 * nh + h:2 * nh + h + 1], (L, hd))
            h_ref[:, cs] = both[:, 0:hd] / jnp.maximum(jnp.abs(both[:, hd:2 * hd]), nrm)
            kwt = (kt.astype(F32) * w_row[h:h + 1, :]).astype(BF16)
            sp = jnp.concatenate([s_prev[h:h + 1, :], s_prev[h:h + 1, :]], axis=1)
            sl = jnp.concatenate([s_loc[h:h + 1, :], s_loc[h:h + 1, :]], axis=1)
            cst[d * nh + h] = sp * cext + sl * _dot(kwt, vext)

    @pl.when(c == nc - 1)
    def _():
        sf_ref[...] = cst[...]
        mf_ref[...] = mst[...]


def _mlstm_call(qs, kt, vb, gplanes, s0, m0, bsz):
    t = qs.shape[0] // bsz
    L = M_CHUNK
    nc = t // L
    nh = C_HEADS
    hw = nh * HEAD_DIM

    def cidx(d, b, c):
        return b * nc + (c if d == 0 else nc - 1 - c)

    def dir_specs(d):
        return [pl.BlockSpec((L, hw), lambda b, c: (cidx(d, b, c), 0)),
                pl.BlockSpec((1, hw, L), lambda b, c: (cidx(d, b, c), 0, 0)),
                pl.BlockSpec((L, hw), lambda b, c: (cidx(d, b, c), 0)),
                pl.BlockSpec((1, 1, GP_PLANES * nh, L), lambda b, c: (cidx(d, b, c), d, 0, 0))]

    sspec = pl.BlockSpec((2 * nh, HEAD_DIM, 2 * HEAD_DIM), lambda b, c: (b, 0, 0))
    mspec = pl.BlockSpec((2, nh, 128), lambda b, c: (b, 0, 0))
    blocks = (6 * _nbytes((L, hw), BF16) + 2 * _nbytes((GP_PLANES * nh, L), F32)
              + 4 * _nbytes((nh, HEAD_DIM, 2 * HEAD_DIM), F32) + 2 * _nbytes((L, hw), F32))
    scratch = 2 * _nbytes((nh, HEAD_DIM, 2 * HEAD_DIM), F32)
    hshape = jax.ShapeDtypeStruct((bsz * t, hw), F32)
    return pl.pallas_call(
        functools.partial(_mlstm_kernel, nc=nc),
        out_shape=[hshape, hshape, jax.ShapeDtypeStruct(s0.shape, F32), jax.ShapeDtypeStruct(m0.shape, F32)],
        grid=(bsz, nc),
        in_specs=dir_specs(0) + dir_specs(1) + [sspec, mspec],
        out_specs=[pl.BlockSpec((L, hw), lambda b, c: (cidx(0, b, c), 0)),
                   pl.BlockSpec((L, hw), lambda b, c: (cidx(1, b, c), 0)), sspec, mspec],
        scratch_shapes=[pltpu.VMEM((2 * nh, HEAD_DIM, 2 * HEAD_DIM), F32), pltpu.VMEM((2, nh, 128), F32)],
        compiler_params=_params(("parallel", "arbitrary"), blocks, scratch + 16 * 1024 * 1024),
        name="mlstm_scan",
    )(qs, kt, vb, gplanes, qs, kt, vb, gplanes, s0, m0)


def _mlstm_out_kernel(hf_ref, hb_ref, o_ref, g_ref, y_ref):
    hd = HEAD_DIM
    for h in range(C_HEADS):
        cs = slice(h * hd, (h + 1) * hd)
        hn = _head_norm(hf_ref[:, cs] + hb_ref[:, cs], g_ref[:, cs])
        y_ref[:, cs] = (hn * jax.nn.sigmoid(o_ref[:, cs])).astype(y_ref.dtype)


def _mlstm_out_call(hf, hb, p_main, norm_g):
    m, hw = hf.shape
    tr = _pick(m, (256, 128))
    blocks = 3 * _nbytes((tr, hw), F32) + _nbytes((tr, hw), BF16)
    hspec = pl.BlockSpec((tr, hw), lambda i: (i, 0))
    return pl.pallas_call(
        _mlstm_out_kernel,
        out_shape=jax.ShapeDtypeStruct((m, hw), BF16),
        grid=(m // tr,),
        in_specs=[hspec, hspec,
                  pl.BlockSpec((tr, hw), lambda i: (i, COL_CO)),
                  pl.BlockSpec((1, hw), lambda i: (0, 0))],
        out_specs=hspec,
        compiler_params=_params(("parallel",), blocks, 2 * 1024 * 1024),
        name="mlstm_out",
    )(hf, hb, p_main, norm_g.reshape(1, hw))


def _rope_lane_tables(n):
    t = jnp.arange(n)
    inv = ROPE_THETA ** (-jnp.arange(ROPE_PAIRS, dtype=F32) / ROPE_PAIRS)
    row = (t // GRID_W).astype(F32)[:, None] * inv
    col = (t % GRID_W).astype(F32)[:, None] * inv
    cos = jnp.concatenate([jnp.cos(row), jnp.cos(row), jnp.cos(col), jnp.cos(col)], axis=1)
    sin = jnp.concatenate([-jnp.sin(row), jnp.sin(row), -jnp.sin(col), jnp.sin(col)], axis=1)
    return cos, sin


def _split_w_in(w_in):
    aq_end = BRANCH_W
    akv_end = aq_end + 2 * A_KV_HEADS * HEAD_DIM
    main = jnp.concatenate([w_in[..., :aq_end], w_in[..., akv_end:MAIN_W], w_in[..., aq_end:akv_end]], axis=-1)
    gates = jnp.pad(w_in[..., MAIN_W:], ((0, 0), (0, 0), (0, GATE_W - 4 * C_HEADS)))
    return main.astype(BF16), gates.astype(BF16)


def kernel(x, c, ctx, c_ctx, w_mod, b_mod, norm_g, ffn_w1, ffn_w3, ffn_w2, w_in, qk_g, attn_sink, na_relpos,
           mlstm_conv_w, mlstm_conv_b, mlstm_gate_b, mlstm_norm_g, w_gate, b_gate, w_branch, w_out):
    bsz, n, d = x.shape
    lc = ctx.shape[1]
    depth = w_mod.shape[0]
    assert n % GRID_W == 0 and n % M_CHUNK == 0 and lc % M_CHUNK == 0 and n % A_BLOCK == 0

    cs = jnp.zeros((8, d), F32).at[:bsz].set(c).at[bsz].set(c_ctx)
    mods = _mod_call(cs, w_mod, b_mod).reshape(depth, 8, N_MOD, d)
    cos_t, sin_t = _rope_lane_tables(n)

    xl = x.reshape(bsz * n, d)
    xc = ctx.reshape(bsz * lc, d)
    nh = C_HEADS
    wg = w_gate.astype(BF16)
    wb = w_branch.astype(BF16)
    w_main, w_gates = _split_w_in(w_in)
    s_zero = jnp.zeros((bsz * 2 * nh, HEAD_DIM, 2 * HEAD_DIM), F32)
    m_zero = jnp.zeros((bsz * 2, nh, 128), F32)

    def ffn(xs, md, k0, g, layer, sub):
        u = _normmod_call(xs, g, md[:, k0], md[:, k0 + 1])
        h = _up_call(u, ffn_w1, ffn_w3, layer, sub)
        return _down_call(h, ffn_w2, xs, md[:, k0 + 2], 0.5, lead=(layer, sub))

    for i in range(depth):
        last = i == depth - 1
        ml = mods[i, :bsz]
        mc = mods[i, bsz:bsz + 1]
        gate_b = jnp.pad(mlstm_gate_b[i].reshape(1, 4 * nh), ((0, 0), (0, GATE_W - 4 * nh)))
        conv_b = mlstm_conv_b[i].reshape(1, -1)
        bias_tab = _na_bias_table(na_relpos[i])

        xc = ffn(xc, mc, 0, norm_g[i, 0], i, 0)
        xl = ffn(xl, ml, 0, norm_g[i, 0], i, 0)

        uc = _normmod_call(xc, norm_g[i, 1], mc[:, 3], mc[:, 4])
        ul = _normmod_call(xl, norm_g[i, 1], ml[:, 3], ml[:, 4])
        pc = _inproj_call(uc, w_main, i)
        pl_ = _inproj_call(ul, w_main, i)
        gc = _matmul_call(uc, w_gates[i])
        gl = _matmul_call(ul, w_gates[i])

        qa_c, qb_c, kb_c, vb_c, ka_c, va_c = _qkprep_call(pc, qk_g[i], cos_t, sin_t, False, lc)
        qa_l, qb_l, kb_l, vb_l, ka_l, va_l = _qkprep_call(pl_, qk_g[i], cos_t, sin_t, True, n)

        a_l = _attn_a_call(qa_l, ka_l, va_l, ka_c, va_c, attn_sink[i], bsz, True)
        n_l = _attn_b_call(qb_l, kb_l, vb_l, kb_c, vb_c, bias_tab, bsz)
        qc_c, kc_c, vc_c = _cprep_call(pc, mlstm_conv_w[i], conv_b, lc)
        qc_l, kc_l, vc_l = _cprep_call(pl_, mlstm_conv_w[i], conv_b, n)
        hc_f, hc_b, s_c, m_c = _mlstm_call(qc_c, kc_c, vc_c, _gateprep_call(gc, gate_b, lc), s_zero, m_zero, bsz)
        hl_f, hl_b, _, _ = _mlstm_call(qc_l, kc_l, vc_l, _gateprep_call(gl, gate_b, n), s_c, m_c, bsz)
        m_l = _mlstm_out_call(hl_f, hl_b, pl_, mlstm_norm_g[i])

        yl = _merge_call(ul, a_l, n_l, m_l, wg, b_gate, wb, i)
        xl = _down_call(yl, w_out, xl, ml[:, 5], 1.0, lead=(i,))
        xl = ffn(xl, ml, 6, norm_g[i, 2], i, 1)
        if not last:
            a_c = _attn_a_call(qa_c, ka_c, va_c, ka_c, va_c, attn_sink[i], bsz, False)
            n_c = _attn_bx_call(qb_c, kb_c, vb_c, bsz)
            mm_c = _mlstm_out_call(hc_f, hc_b, pc, mlstm_norm_g[i])
            yc = _merge_call(uc, a_c, n_c, mm_c, wg, b_gate, wb, i)
            xc = _down_call(yc, w_out, xc, mc[:, 5], 1.0, lead=(i,))
            xc = ffn(xc, mc, 6, norm_g[i, 2], i, 1)
    return xl.reshape(bsz, n, d)
```

```python
import functools

import numpy as np
import jax
import jax.numpy as jnp
from jax import lax
from jax.experimental import pallas as pl
from jax.experimental.pallas import tpu as pltpu

F32 = jnp.float32
BF16 = jnp.bfloat16

HEAD_DIM = 128
GRID_W = 64
ROPE_PAIRS = HEAD_DIM // 4
ROPE_THETA = 10000.0
A_HEADS = 8
A_KV_HEADS = 2
A_GROUP = A_HEADS // A_KV_HEADS
A_WINDOW = 128
A_BLOCK = 128
B_HEADS = 8
NA_ROWS = 8
NA_COLS = 16
C_HEADS = 8
C_CONV = 5
BRANCH_W = 8 * HEAD_DIM
N_MOD = 9
EPS = 1e-6
NEG = -1e30
ATTN_SCALE = HEAD_DIM ** -0.5
LOG2E = 1.4426950408889634
QK_SCALE_LOG2 = ATTN_SCALE * LOG2E

M_CHUNK = 128
NA_TILE_ROWS = 8
NA_TILE = NA_TILE_ROWS * GRID_W
NA_WIN_ROWS = NA_TILE_ROWS + NA_ROWS
NA_HEADS_PER_STEP = 4
A_QBLOCKS = 2

BF16_SUBLANES = 16
V7X_VMEM_BYTES = 64 * 1024 * 1024
VMEM_CAP = V7X_VMEM_BYTES - 6 * 1024 * 1024

COL_AQ, COL_BQ, COL_BK, COL_BV, COL_CQ, COL_CK, COL_CV, COL_CO = range(8)
MAIN_W = 8 * BRANCH_W + 2 * A_KV_HEADS * HEAD_DIM
GATE_W = 128


def _pick(n, cands):
    for c in cands:
        if n % c == 0:
            return c
    raise ValueError(f"no tile in {cands} divides {n}")


def _params(sem, block_bytes, temp_bytes=0):
    limit = 2 * block_bytes + temp_bytes + 4 * 1024 * 1024
    return pltpu.CompilerParams(dimension_semantics=sem,
                                vmem_limit_bytes=int(min(max(limit, 16 * 1024 * 1024), VMEM_CAP)))


def _nbytes(shape, dtype):
    return int(np.prod(shape)) * jnp.dtype(dtype).itemsize


def _dot(a, b):
    return jnp.dot(a, b, preferred_element_type=F32)


def _dot_nt(a, b):
    return lax.dot_general(a, b, (((1,), (1,)), ((), ())), preferred_element_type=F32)


def _dot_tn(a, b):
    return lax.dot_general(a, b, (((0,), (0,)), ((), ())), preferred_element_type=F32)


def _silu(x):
    return x * jax.nn.sigmoid(x)


def _mod_kernel(c_ref, w_ref, b_ref, o_ref):
    a = _silu(c_ref[...]).astype(BF16)
    o_ref[0] = _dot(a, w_ref[0].astype(BF16)) + b_ref[0]


def _mod_call(cs, w_mod, b_mod):
    depth, d, nd = w_mod.shape
    tn = _pick(nd, (1024, 512, 256, 128))
    rows = cs.shape[0]
    blocks = _nbytes((d, tn), F32) + _nbytes((rows, d), F32) + _nbytes((rows, tn), F32)
    return pl.pallas_call(
        _mod_kernel,
        out_shape=jax.ShapeDtypeStruct((depth, rows, nd), F32),
        grid=(depth, nd // tn),
        in_specs=[pl.BlockSpec((rows, d), lambda l, j: (0, 0)),
                  pl.BlockSpec((1, d, tn), lambda l, j: (l, 0, j)),
                  pl.BlockSpec((1, 1, tn), lambda l, j: (l, 0, j))],
        out_specs=pl.BlockSpec((1, rows, tn), lambda l, j: (l, 0, j)),
        compiler_params=_params(("parallel", "parallel"), blocks, _nbytes((d, tn), BF16)),
        name="mod_vectors",
    )(cs, w_mod, b_mod.reshape(depth, 1, nd))


def _normmod_kernel(x_ref, g_ref, shift_ref, scale_ref, o_ref):
    x = x_ref[...]
    ms = jnp.mean(x * x, axis=-1, keepdims=True)
    y = x * lax.rsqrt(ms + EPS) * g_ref[...]
    o_ref[...] = (y * (1.0 + scale_ref[0]) + shift_ref[0]).astype(o_ref.dtype)


def _normmod_call(x, g, shift, scale):
    m, d = x.shape
    groups = shift.shape[0]
    tr = _pick(m // groups, (512, 256, 128, 64, 8))
    per = (m // groups) // tr
    blocks = _nbytes((tr, d), F32) + _nbytes((tr, d), BF16) + 3 * _nbytes((1, d), F32)
    return pl.pallas_call(
        _normmod_kernel,
        out_shape=jax.ShapeDtypeStruct((m, d), BF16),
        grid=(m // tr,),
        in_specs=[pl.BlockSpec((tr, d), lambda i: (i, 0)),
                  pl.BlockSpec((1, d), lambda i: (0, 0)),
                  pl.BlockSpec((1, 1, d), lambda i: (i // per, 0, 0)),
                  pl.BlockSpec((1, 1, d), lambda i: (i // per, 0, 0))],
        out_specs=pl.BlockSpec((tr, d), lambda i: (i, 0)),
        compiler_params=_params(("parallel",), blocks, 2 * _nbytes((tr, d), F32)),
        name="norm_modulate",
    )(x, g.reshape(1, d), shift.reshape(groups, 1, d), scale.reshape(groups, 1, d))


def _prefetch_chunks(tm, nj):
    n = 1
    while 2 * n <= nj and tm % (2 * n) == 0 and (tm // (2 * n)) % BF16_SUBLANES == 0:
        n *= 2
    return n


def _rowblock(a_hbm, abuf, sem, *, ni, nj, tm):
    i = pl.program_id(0)
    j = pl.program_id(1)
    slot = i % 2
    nch = _prefetch_chunks(tm, nj)
    ch = tm // nch

    def chunk_copy(blk, c, slot_):
        return pltpu.make_async_copy(a_hbm.at[pl.ds(blk * tm + c * ch, ch), :],
                                     abuf.at[slot_, pl.ds(c * ch, ch), :], sem.at[slot_])

    @pl.when((i == 0) & (j == 0))
    def _():
        for c in range(nch):
            chunk_copy(0, c, 0).start()

    @pl.when(j == 0)
    def _():
        for c in range(nch):
            chunk_copy(i, c, slot).wait()

    @pl.when((i + 1 < ni) & (j < nch))
    def _():
        chunk_copy(i + 1, j, 1 - slot).start()

    return abuf[slot]


def _rowblock_scratch(tm, k):
    return [pltpu.VMEM((2, tm, k), BF16), pltpu.SemaphoreType.DMA((2,))]


def _up_kernel(u_hbm, w1_ref, w3_ref, o_ref, ubuf, sem, *, ni, nj, tm):
    u = _rowblock(u_hbm, ubuf, sem, ni=ni, nj=nj, tm=tm)
    h1 = _dot(u, w1_ref[...].astype(BF16))
    h3 = _dot(u, w3_ref[...].astype(BF16))
    o_ref[...] = (_silu(h1) * h3).astype(o_ref.dtype)


def _up_call(u, w1, w3, layer, sub):
    m, d = u.shape
    f = w1.shape[-1]
    tm = _pick(m, (1024, 512, 256))
    tf = _pick(f, (256, 128))
    wspec = pl.BlockSpec((None, None, d, tf), lambda i, j: (layer, sub, 0, j))
    ni, nj = m // tm, f // tf
    blocks = 2 * _nbytes((d, tf), F32) + _nbytes((tm, tf), BF16)
    return pl.pallas_call(
        functools.partial(_up_kernel, ni=ni, nj=nj, tm=tm),
        out_shape=jax.ShapeDtypeStruct((m, f), BF16),
        grid=(ni, nj),
        in_specs=[pl.BlockSpec(memory_space=pl.ANY), wspec, wspec],
        out_specs=pl.BlockSpec((tm, tf), lambda i, j: (i, j)),
        scratch_shapes=_rowblock_scratch(tm, d),
        compiler_params=_params(("arbitrary", "arbitrary"), blocks,
                                2 * _nbytes((tm, d), BF16) + 2 * _nbytes((d, tf), BF16)
                                + 4 * _nbytes((tm, tf), F32)),
        name="ffn_up",
    )(u, w1, w3)


def _down_kernel(a_hbm, w_ref, x_ref, gate_ref, o_ref, abuf, sem, *, coef, ni, nj, tm):
    a = _rowblock(a_hbm, abuf, sem, ni=ni, nj=nj, tm=tm)
    acc = _dot(a, w_ref[...].astype(BF16))
    o_ref[...] = x_ref[...] + (coef * gate_ref[0]) * acc


def _down_call(a, w, x, gate, coef, lead=()):
    m, k = a.shape
    d = w.shape[-1]
    groups = gate.shape[0]
    tm = _pick(m // groups, (1024, 512, 256))
    tn = _pick(d, (512, 256, 128) if k <= 4096 else (256, 128))
    per = (m // groups) // tm
    ni, nj = m // tm, d // tn
    blocks = _nbytes((k, tn), w.dtype) + 2 * _nbytes((tm, tn), F32) + _nbytes((1, tn), F32)
    temps = (2 * _nbytes((tm, k), BF16) + 2 * _nbytes((tm, tn), F32)
             + (_nbytes((k, tn), BF16) if w.dtype != BF16 else 0))
    wspec = pl.BlockSpec((None,) * len(lead) + (k, tn), lambda i, j: tuple(lead) + (0, j))
    return pl.pallas_call(
        functools.partial(_down_kernel, coef=coef, ni=ni, nj=nj, tm=tm),
        out_shape=jax.ShapeDtypeStruct((m, d), F32),
        grid=(ni, nj),
        in_specs=[pl.BlockSpec(memory_space=pl.ANY),
                  wspec,
                  pl.BlockSpec((tm, tn), lambda i, j: (i, j)),
                  pl.BlockSpec((1, 1, tn), lambda i, j: (i // per, 0, j))],
        out_specs=pl.BlockSpec((tm, tn), lambda i, j: (i, j)),
        scratch_shapes=_rowblock_scratch(tm, k),
        compiler_params=_params(("arbitrary", "arbitrary"), blocks, temps),
        name="proj_residual",
    )(a, w, x, gate.reshape(groups, 1, d))


def _matmul_kernel(a_ref, w_ref, o_ref):
    o_ref[...] = _dot(a_ref[...], w_ref[...]).astype(o_ref.dtype)


def _inproj_kernel(u_hbm, w_ref, o_ref, ubuf, sem, *, ni, nj, tm):
    u = _rowblock(u_hbm, ubuf, sem, ni=ni, nj=nj, tm=tm)
    o_ref[...] = _dot(u, w_ref[...])


IN_TN = 2 * A_KV_HEADS * HEAD_DIM


def _inproj_call(u, w_main, layer):
    m, d = u.shape
    tm = _pick(m, (1024, 512, 256))
    tn = IN_TN
    ni, nj = m // tm, MAIN_W // tn
    blocks = _nbytes((d, tn), BF16) + _nbytes((tm, tn), F32)
    return pl.pallas_call(
        functools.partial(_inproj_kernel, ni=ni, nj=nj, tm=tm),
        out_shape=jax.ShapeDtypeStruct((m, MAIN_W), F32),
        grid=(ni, nj),
        in_specs=[pl.BlockSpec(memory_space=pl.ANY),
                  pl.BlockSpec((None, d, tn), lambda i, j: (layer, 0, j))],
        out_specs=pl.BlockSpec((tm, tn), lambda i, j: (i, j)),
        scratch_shapes=_rowblock_scratch(tm, d),
        compiler_params=_params(("arbitrary", "arbitrary"), blocks,
                                2 * _nbytes((tm, d), BF16) + 2 * _nbytes((tm, tn), F32)),
        name="in_proj",
    )(u, w_main)


def _matmul_call(a, w, out_dtype=F32):
    m, k = a.shape
    n = w.shape[1]
    tm = _pick(m, (1024, 512, 256))
    tn = _pick(n, (512, 256, 128))
    blocks = _nbytes((tm, k), BF16) + _nbytes((k, tn), BF16) + _nbytes((tm, tn), out_dtype)
    return pl.pallas_call(
        _matmul_kernel,
        out_shape=jax.ShapeDtypeStruct((m, n), out_dtype),
        grid=(m // tm, n // tn),
        in_specs=[pl.BlockSpec((tm, k), lambda i, j: (i, 0)),
                  pl.BlockSpec((k, tn), lambda i, j: (0, j))],
        out_specs=pl.BlockSpec((tm, tn), lambda i, j: (i, j)),
        compiler_params=_params(("parallel", "parallel"), blocks, _nbytes((tm, tn), F32)),
        name="in_proj",
    )(a, w)


def _merge_kernel(u_ref, oa_ref, ob_ref, oc_ref, wg_ref, bg_ref, wb_ref, y_ref):
    u = u_ref[...]
    acc = None
    for j, br_ref in enumerate((oa_ref, ob_ref, oc_ref)):
        gate = jax.nn.sigmoid(_dot(u, wg_ref[j]) + bg_ref[j])
        term = gate * _dot(br_ref[...], wb_ref[j])
        acc = term if acc is None else acc + term
    y_ref[...] = acc.astype(y_ref.dtype)


def _merge_call(u, oa, ob, oc, wg, bg, wb, layer):
    m, d = u.shape
    bw = oa.shape[1]
    tm = _pick(m, (1024, 512, 256))
    tn = _pick(d, (256, 128))
    blocks = (_nbytes((tm, d), BF16) + 3 * _nbytes((tm, bw), BF16) + 3 * _nbytes((d, tn), BF16)
              + 3 * _nbytes((bw, tn), BF16) + _nbytes((tm, tn), BF16))
    return pl.pallas_call(
        _merge_kernel,
        out_shape=jax.ShapeDtypeStruct((m, d), BF16),
        grid=(m // tm, d // tn),
        in_specs=[pl.BlockSpec((tm, d), lambda i, j: (i, 0)),
                  pl.BlockSpec((tm, bw), lambda i, j: (i, 0)),
                  pl.BlockSpec((tm, bw), lambda i, j: (i, 0)),
                  pl.BlockSpec((tm, bw), lambda i, j: (i, 0)),
                  pl.BlockSpec((None, 3, d, tn), lambda i, j: (layer, 0, 0, j)),
                  pl.BlockSpec((None, 3, 1, tn), lambda i, j: (layer, 0, 0, j)),
                  pl.BlockSpec((None, 3, bw, tn), lambda i, j: (layer, 0, 0, j))],
        out_specs=pl.BlockSpec((tm, tn), lambda i, j: (i, j)),
        compiler_params=_params(("parallel", "parallel"), blocks, 10 * _nbytes((tm, tn), F32)),
        name="gated_merge",
    )(u, oa, ob, oc, wg, bg.reshape(bg.shape[0], 3, 1, d), wb)


def _head_norm(x, g):
    ms = jnp.mean(x * x, axis=-1, keepdims=True)
    return x * lax.rsqrt(ms + EPS) * g


def _rope(y, cos, sin_signed):
    lane = lax.broadcasted_iota(jnp.int32, y.shape, 1)
    partner = jnp.where((lane % 64) < ROPE_PAIRS, pltpu.roll(y, HEAD_DIM - ROPE_PAIRS, axis=1),
                        pltpu.roll(y, ROPE_PAIRS, axis=1))
    return y * cos + partner * sin_signed


def _qkprep_kernel(p4_ref, pa_ref, g_ref, cos_ref, sin_ref,
                   qa_ref, qb_ref, kb_ref, vb_ref, ka_ref, va_ref, *, rope):
    hd = HEAD_DIM
    if rope:
        cos = cos_ref[...]
        sin = sin_ref[...]
    for h in range(A_HEADS):
        y = _head_norm(p4_ref[:, h * hd:(h + 1) * hd], g_ref[0:1, :])
        if rope:
            y = _rope(y, cos, sin)
        qa_ref[:, h * hd:(h + 1) * hd] = y.astype(BF16)
    for h in range(B_HEADS):
        c0 = BRANCH_W + h * hd
        qb_ref[:, h * hd:(h + 1) * hd] = _head_norm(p4_ref[:, c0:c0 + hd], g_ref[2:3, :]).astype(BF16)
        c0 = 2 * BRANCH_W + h * hd
        kb_ref[:, h * hd:(h + 1) * hd] = _head_norm(p4_ref[:, c0:c0 + hd], g_ref[3:4, :]).astype(BF16)
    vb_ref[...] = p4_ref[:, 3 * BRANCH_W:4 * BRANCH_W].astype(BF16)
    for h in range(A_KV_HEADS):
        y = _head_norm(pa_ref[:, h * hd:(h + 1) * hd], g_ref[1:2, :])
        if rope:
            y = _rope(y, cos, sin)
        ka_ref[:, h * hd:(h + 1) * hd] = y.astype(BF16)
    kvw = A_KV_HEADS * hd
    va_ref[...] = pa_ref[:, kvw:2 * kvw].astype(BF16)


def _qkprep_call(p_main, qk_g, cos_t, sin_t, rope, seq):
    m = p_main.shape[0]
    tr = _pick(seq, (256, 128))
    per = seq // tr
    kvw = A_KV_HEADS * HEAD_DIM
    w4 = 4 * BRANCH_W
    blocks = (_nbytes((tr, w4), F32) + _nbytes((tr, 2 * kvw), F32) + 2 * _nbytes((tr, HEAD_DIM), F32)
              + _nbytes((tr, w4), BF16) + _nbytes((tr, 2 * kvw), BF16))
    outs = [jax.ShapeDtypeStruct((m, BRANCH_W), BF16)] * 4 + [jax.ShapeDtypeStruct((m, kvw), BF16)] * 2
    return pl.pallas_call(
        functools.partial(_qkprep_kernel, rope=rope),
        out_shape=outs,
        grid=(m // tr,),
        in_specs=[pl.BlockSpec((tr, w4), lambda i: (i, 0)),
                  pl.BlockSpec((tr, 2 * kvw), lambda i: (i, (8 * BRANCH_W) // (2 * kvw))),
                  pl.BlockSpec((4, HEAD_DIM), lambda i: (0, 0)),
                  pl.BlockSpec((tr, HEAD_DIM), lambda i: (i % per, 0)),
                  pl.BlockSpec((tr, HEAD_DIM), lambda i: (i % per, 0))],
        out_specs=[pl.BlockSpec((tr, BRANCH_W), lambda i: (i, 0))] * 4
                  + [pl.BlockSpec((tr, kvw), lambda i: (i, 0))] * 2,
        compiler_params=_params(("parallel",), blocks, 4 * _nbytes((tr, HEAD_DIM), F32)),
        name="qk_prep",
    )(p_main, p_main, qk_g, cos_t, sin_t)


def _attn_a_kernel(sink_ref, q_ref, *rest, lc, window):
    nwin = A_QBLOCKS + 2 if window else 0
    nmask = A_QBLOCKS if window else 0
    k_refs = rest[:nwin]
    v_refs = rest[nwin:2 * nwin]
    mask_refs = rest[2 * nwin:2 * nwin + nmask]
    kx_ref, vx_ref, o_ref = rest[2 * nwin + nmask:]
    hd = HEAD_DIM
    blk = A_BLOCK
    gw = A_GROUP * hd
    keys = lc + 3 * blk if window else lc
    ones = jnp.ones((keys, hd), BF16)
    for sb in range(A_QBLOCKS):
        rows = slice(sb * blk, (sb + 1) * blk)
        for hk in range(A_KV_HEADS):
            cs = slice(hk * hd, (hk + 1) * hd)
            if window:
                kall = jnp.concatenate([kx_ref[:, cs]] + [r[:, cs] for r in k_refs[sb:sb + 3]], axis=0)
                vall = jnp.concatenate([vx_ref[:, cs]] + [r[:, cs] for r in v_refs[sb:sb + 3]], axis=0)
            else:
                kall = kx_ref[:, cs]
                vall = vx_ref[:, cs]
            q4 = jnp.concatenate([q_ref[rows, hk * gw + g * hd:hk * gw + (g + 1) * hd] for g in range(A_GROUP)],
                                 axis=0)
            s = _dot_nt(q4, kall) * QK_SCALE_LOG2
            if window:
                s = s + mask_refs[sb][0]
            snk = jnp.concatenate(
                [jnp.full((blk, 1), sink_ref[hk * A_GROUP + g] * LOG2E, F32) for g in range(A_GROUP)], axis=0)
            mx = jnp.maximum(jnp.max(s, axis=-1, keepdims=True), snk)
            e = jnp.exp2(s - mx).astype(BF16)
            both = _dot(e, jnp.concatenate([vall, ones], axis=1))
            o = both[:, 0:hd] * (1.0 / (both[:, hd:2 * hd] + jnp.exp2(snk - mx)))
            for g in range(A_GROUP):
                o_ref[rows, hk * gw + g * hd:hk * gw + (g + 1) * hd] = (
                    o[g * blk:(g + 1) * blk].astype(o_ref.dtype))


def _gqa_mask_table(lc, nb):
    blk = A_BLOCK
    r = np.arange(blk)[:, None]
    kc = np.arange(3 * blk)[None, :]
    band = np.abs(r + blk - kc) <= A_WINDOW
    tabs = []
    for first, last in ((True, False), (False, False), (False, True)):
        ok = band & ~(first & (kc < blk)) & ~(last & (kc >= 2 * blk))
        win = np.where(ok, 0.0, NEG).astype(np.float32)
        tabs.append(np.tile(np.concatenate([np.zeros((blk, lc), np.float32), win], axis=1), (A_GROUP, 1)))
    assert nb >= 2
    return np.stack(tabs)


def _attn_a_call(qa, ka, va, kx, vx, sink, bsz, window):
    n = qa.shape[0] // bsz
    lc = kx.shape[0] // bsz
    blk = A_BLOCK
    nb = n // blk
    qb = A_QBLOCKS
    assert nb % qb == 0
    nt = nb // qb
    kvw = A_KV_HEADS * HEAD_DIM
    qw = A_HEADS * HEAD_DIM
    qspec = pl.BlockSpec((qb * blk, qw), lambda b, t: (b * nt + t, 0))
    xspec = pl.BlockSpec((lc, kvw), lambda b, t: (b, 0))
    sspec = pl.BlockSpec(memory_space=pltpu.SMEM)
    if window:
        def kspec(off):
            return pl.BlockSpec((blk, kvw), lambda b, t: (b * nb + jnp.clip(t * qb + off, 0, nb - 1), 0))

        def mspec(sb):
            def variant(b, t):
                i = t * qb + sb
                return (jnp.where(i == 0, 0, jnp.where(i == nb - 1, 2, 1)), 0, 0)
            return pl.BlockSpec((1, A_GROUP * blk, lc + 3 * blk), variant)

        kspecs = [kspec(off) for off in range(-1, qb + 1)]
        mspecs = [mspec(sb) for sb in range(qb)]
        mask = jnp.asarray(_gqa_mask_table(lc, nb))
        in_specs = [sspec, qspec] + kspecs + kspecs + mspecs + [xspec, xspec]
        args = (sink, qa) + (ka,) * len(kspecs) + (va,) * len(kspecs) + (mask,) * qb + (kx, vx)
        keys = lc + 3 * blk
    else:
        in_specs = [sspec, qspec, xspec, xspec]
        args = (sink, qa, kx, vx)
        keys = lc
    blocks = (2 * _nbytes((qb * blk, qw), BF16) + 2 * _nbytes((keys + qb * blk, kvw), BF16)
              + (qb * _nbytes((A_GROUP * blk, keys), F32) if window else 0))
    return pl.pallas_call(
        functools.partial(_attn_a_kernel, lc=lc, window=window),
        out_shape=jax.ShapeDtypeStruct(qa.shape, BF16),
        grid=(bsz, nt),
        in_specs=in_specs,
        out_specs=qspec,
        compiler_params=_params(("parallel", "parallel"), blocks,
                                10 * qb * _nbytes((A_GROUP * blk, keys), F32)),
        name="windowed_gqa" if window else "context_gqa",
    )(*args)


def _softmax2_pv(s1, v1, s2, v2):
    hd = v1.shape[1]
    mx = jnp.maximum(jnp.max(s1, axis=-1, keepdims=True), jnp.max(s2, axis=-1, keepdims=True))
    e1 = jnp.exp2(s1 - mx).astype(BF16)
    e2 = jnp.exp2(s2 - mx).astype(BF16)
    v1e = jnp.concatenate([v1, jnp.ones(v1.shape, BF16)], axis=1)
    v2e = jnp.concatenate([v2, jnp.ones(v2.shape, BF16)], axis=1)
    both = _dot(e1, v1e) + _dot(e2, v2e)
    return both[:, 0:hd] * (1.0 / both[:, hd:2 * hd])


def _attn_b_kernel(q_ref, k0_ref, k1_ref, k2_ref, k3_ref, v0_ref, v1_ref, v2_ref, v3_ref, kx_ref, vx_ref,
                   bias_ref, o_ref):
    hd = HEAD_DIM
    for h in range(NA_HEADS_PER_STEP):
        cs = slice(h * hd, (h + 1) * hd)
        kwin = jnp.concatenate([k0_ref[:, cs], k1_ref[:, cs], k2_ref[:, cs], k3_ref[:, cs]], axis=0)
        vwin = jnp.concatenate([v0_ref[:, cs], v1_ref[:, cs], v2_ref[:, cs], v3_ref[:, cs]], axis=0)
        q = q_ref[:, cs]
        s_nb = _dot_nt(q, kwin) * QK_SCALE_LOG2 + bias_ref[h, 0]
        s_cx = _dot_nt(q, kx_ref[:, cs]) * QK_SCALE_LOG2
        o_ref[:, cs] = _softmax2_pv(s_nb, vwin, s_cx, vx_ref[:, cs]).astype(o_ref.dtype)


def _attn_bx_kernel(q_ref, kx_ref, vx_ref, o_ref):
    hd = HEAD_DIM
    for h in range(B_HEADS):
        cs = slice(h * hd, (h + 1) * hd)
        s = _dot_nt(q_ref[:, cs], kx_ref[:, cs]) * QK_SCALE_LOG2
        mx = jnp.max(s, axis=-1, keepdims=True)
        e = jnp.exp2(s - mx)
        inv = 1.0 / jnp.sum(e, axis=-1, keepdims=True)
        o_ref[:, cs] = (_dot(e.astype(BF16), vx_ref[:, cs]) * inv).astype(o_ref.dtype)


def _na_bias_kernel(rel_ref, o_ref):
    h = pl.program_id(0)
    pos = pl.program_id(1)
    nrel_r = 2 * NA_ROWS - 1
    nrel_c = 2 * NA_COLS - 1
    qi = lax.broadcasted_iota(jnp.int32, (GRID_W, GRID_W), 0)
    ki = lax.broadcasted_iota(jnp.int32, (GRID_W, GRID_W), 1)
    dc = jnp.clip(ki - qi, -(NA_COLS - 1), NA_COLS - 1) + NA_COLS - 1
    start = jnp.clip(qi - NA_COLS // 2, 0, GRID_W - NA_COLS)
    col_in = (ki >= start) & (ki < start + NA_COLS)
    planes = []
    for dr in range(nrel_r):
        acc = jnp.zeros((GRID_W, GRID_W), F32)
        for dcv in range(nrel_c):
            acc = jnp.where(dc == dcv, rel_ref[(h * nrel_r + dr) * nrel_c + dcv], acc)
        planes.append(jnp.where(col_in, acc * LOG2E, NEG))
    masked = jnp.full((GRID_W, GRID_W), NEG, F32)
    half = NA_ROWS // 2
    first_key = {0: lambda j: max(j, half), 1: lambda j: j, 2: lambda j: min(j, half)}
    for p, lo_of in first_key.items():
        @pl.when(pos == p)
        def _(lo_of=lo_of):
            for j in range(NA_TILE_ROWS):
                lo = lo_of(j)
                for jj in range(NA_WIN_ROWS):
                    blk = planes[jj - j + half - 1] if lo <= jj < lo + NA_ROWS else masked
                    o_ref[0, 0, j * GRID_W:(j + 1) * GRID_W, jj * GRID_W:(jj + 1) * GRID_W] = blk


def _na_bias_table(relpos):
    nh = relpos.shape[0]
    shape = (nh, 3, NA_TILE, NA_WIN_ROWS * GRID_W)
    return pl.pallas_call(
        _na_bias_kernel,
        out_shape=jax.ShapeDtypeStruct(shape, F32),
        grid=(nh, 3),
        in_specs=[pl.BlockSpec(memory_space=pltpu.SMEM)],
        out_specs=pl.BlockSpec((1, 1) + shape[2:], lambda h, p: (h, p, 0, 0)),
        compiler_params=_params(("parallel", "parallel"), _nbytes(shape[2:], F32)),
        name="na_bias_table",
    )(relpos.reshape(-1))


def _attn_b_call(qb, kb, vb, kx, vx, bias, bsz):
    n = qb.shape[0] // bsz
    lc = kx.shape[0] // bsz
    rows = n // GRID_W
    assert rows % NA_TILE_ROWS == 0 and rows >= 2 * NA_TILE_ROWS
    nt = rows // NA_TILE_ROWS
    hp = NA_HEADS_PER_STEP
    hd = hp * HEAD_DIM
    sub = NA_TILE // 2
    qspec = pl.BlockSpec((NA_TILE, hd), lambda b, h, t: (b * nt + t, h))

    def kspec(s):
        return pl.BlockSpec((sub, hd),
                            lambda b, h, t: (jnp.clip(2 * t + s, 0, 2 * nt - 1) + 2 * b * nt, h))

    kspecs = [kspec(s) for s in (-1, 0, 1, 2)]
    xspec = pl.BlockSpec((lc, hd), lambda b, h, t: (b, h))
    bspec = pl.BlockSpec((hp, 1) + bias.shape[2:],
                         lambda b, h, t: (h, jnp.where(t == 0, 0, jnp.where(t == nt - 1, 2, 1)), 0, 0))
    blocks = (2 * _nbytes((NA_TILE, hd), BF16) + 8 * _nbytes((sub, hd), BF16) + 2 * _nbytes((lc, hd), BF16)
              + hp * _nbytes(bias.shape[2:], F32))
    return pl.pallas_call(
        _attn_b_kernel,
        out_shape=jax.ShapeDtypeStruct(qb.shape, BF16),
        grid=(bsz, B_HEADS // hp, nt),
        in_specs=[qspec] + kspecs + kspecs + [xspec, xspec, bspec],
        out_specs=qspec,
        compiler_params=_params(("parallel", "parallel", "parallel"), blocks,
                                5 * hp * _nbytes((NA_TILE, NA_WIN_ROWS * GRID_W + lc), F32)),
        name="neighbourhood_attn",
    )(qb, kb, kb, kb, kb, vb, vb, vb, vb, kx, vx, bias)


def _attn_bx_call(qx, kx, vx, bsz):
    lc = qx.shape[0] // bsz
    w = qx.shape[1]
    spec = pl.BlockSpec((lc, w), lambda b: (b, 0))
    return pl.pallas_call(
        _attn_bx_kernel,
        out_shape=jax.ShapeDtypeStruct(qx.shape, BF16),
        grid=(bsz,),
        in_specs=[spec, spec, spec],
        out_specs=spec,
        compiler_params=_params(("parallel",), 4 * _nbytes((lc, w), BF16), 4 * 1024 * 1024),
        name="context_full_attn",
    )(qx, kx, vx)


def _log_sigmoid(x):
    return -(jnp.maximum(-x, 0.0) + jnp.log1p(jnp.exp(-jnp.abs(x))))


CONV_HALO = 8


def _cprep_kernel(q_ref, k_ref, v_ref, qlo_ref, qhi_ref, klo_ref, khi_ref, cw_ref, cb_ref,
                  qs_ref, kt_ref, vb_ref, ext, *, per):
    pos = pl.program_id(0) % per
    tr = q_ref.shape[0]
    hw = q_ref.shape[1]
    halo = CONV_HALO
    pad = C_CONV // 2

    def conv_silu(x_ref, lo_ref, hi_ref, col0):
        ext[0:halo, :] = jnp.where(pos > 0, lo_ref[...], 0.0)
        ext[halo:halo + tr, :] = x_ref[...]
        ext[halo + tr:2 * halo + tr, :] = jnp.where(pos < per - 1, hi_ref[...], 0.0)
        out = cb_ref[:, col0:col0 + hw]
        for j in range(C_CONV):
            out = out + cw_ref[j:j + 1, col0:col0 + hw] * ext[halo - pad + j:halo - pad + j + tr, :]
        return _silu(out)

    qs_ref[...] = (conv_silu(q_ref, qlo_ref, qhi_ref, 0) * ATTN_SCALE).astype(BF16)
    kc = conv_silu(k_ref, klo_ref, khi_ref, hw)
    for ck in range(tr // M_CHUNK):
        for h in range(C_HEADS):
            blk = kc[ck * M_CHUNK:(ck + 1) * M_CHUNK, h * HEAD_DIM:(h + 1) * HEAD_DIM]
            kt_ref[ck, h * HEAD_DIM:(h + 1) * HEAD_DIM, :] = blk.T.astype(BF16)
    vb_ref[...] = v_ref[...].astype(BF16)


def _cprep_call(p_main, conv_w, conv_b, seq):
    m = p_main.shape[0]
    hw = C_HEADS * HEAD_DIM
    assert M_CHUNK == HEAD_DIM
    tr = _pick(seq, (256, 128))
    per = seq // tr
    halo = CONV_HALO
    hb = tr // halo
    nblk = m // halo
    cpb = tr // M_CHUNK

    def main(col):
        return pl.BlockSpec((tr, hw), lambda i: (i, col))

    def lo(col):
        return pl.BlockSpec((halo, hw), lambda i: (jnp.maximum(i * hb - 1, 0), col))

    def hi(col):
        return pl.BlockSpec((halo, hw), lambda i: (jnp.minimum((i + 1) * hb, nblk - 1), col))

    blocks = (3 * _nbytes((tr, hw), F32) + 4 * _nbytes((halo, hw), F32) + _nbytes((C_CONV + 1, 2 * hw), F32)
              + 3 * _nbytes((tr, hw), BF16))
    scratch = _nbytes((tr + 2 * halo, hw), F32)
    out = jax.ShapeDtypeStruct((m, hw), BF16)
    out_t = jax.ShapeDtypeStruct((m // M_CHUNK, hw, M_CHUNK), BF16)
    ospec = pl.BlockSpec((tr, hw), lambda i: (i, 0))
    tspec = pl.BlockSpec((cpb, hw, M_CHUNK), lambda i: (i, 0, 0))
    return pl.pallas_call(
        functools.partial(_cprep_kernel, per=per),
        out_shape=[out, out_t, out],
        grid=(m // tr,),
        in_specs=[main(COL_CQ), main(COL_CK), main(COL_CV), lo(COL_CQ), hi(COL_CQ), lo(COL_CK), hi(COL_CK),
                  pl.BlockSpec((C_CONV, 2 * hw), lambda i: (0, 0)),
                  pl.BlockSpec((1, 2 * hw), lambda i: (0, 0))],
        out_specs=[ospec, tspec, ospec],
        scratch_shapes=[pltpu.VMEM((tr + 2 * halo, hw), F32)],
        compiler_params=_params(("parallel",), blocks, scratch + 4 * _nbytes((tr, hw), F32)),
        name="mlstm_prep",
    )(p_main, p_main, p_main, p_main, p_main, p_main, p_main, conv_w, conv_b)


GP_PLANES = 6


def _gateprep_kernel(g_ref, gb_ref, o_ref):
    nh = C_HEADS
    L = M_CHUNK
    lane = lax.broadcasted_iota(jnp.int32, (nh, L), 1)
    for ck in range(g_ref.shape[0] // L):
        gt = (g_ref[ck * L:(ck + 1) * L, :] + gb_ref[...]).T
        for d in range(2):
            ig = gt[2 * d * nh:(2 * d + 1) * nh]
            logf = _log_sigmoid(gt[(2 * d + 1) * nh:(2 * d + 2) * nh])
            pre = logf
            sft = 1
            while sft < L:
                pre = pre + jnp.where(lane >= sft, pltpu.roll(pre, sft, axis=1), 0.0)
                sft *= 2
            total = jnp.broadcast_to(pre[:, L - 1:L], (nh, L))
            bcum = pre if d == 0 else total - pre + logf
            r_row = ig - bcum
            rmax = r_row
            sft = 1
            while sft < L:
                if d == 0:
                    rmax = jnp.maximum(rmax, jnp.where(lane >= sft, pltpu.roll(rmax, sft, axis=1), NEG))
                else:
                    rmax = jnp.maximum(rmax, jnp.where(lane < L - sft, pltpu.roll(rmax, L - sft, axis=1), NEG))
                sft *= 2
            r_top = jnp.broadcast_to(rmax[:, L - 1:L] if d == 0 else rmax[:, 0:1], (nh, L))
            w_row = jnp.exp(r_row - r_top)
            for k, plane in enumerate((r_row, rmax, bcum, total, w_row, r_top)):
                o_ref[ck, d, k * nh:(k + 1) * nh, :] = plane


def _gateprep_call(gates, gate_b, seq):
    m = gates.shape[0]
    L = M_CHUNK
    tr = _pick(seq, (1024, 512, 256, 128))
    shape = (m // L, 2, GP_PLANES * C_HEADS, L)
    blocks = _nbytes((tr, GATE_W), F32) + _nbytes((tr // L,) + shape[1:], F32)
    return pl.pallas_call(
        _gateprep_kernel,
        out_shape=jax.ShapeDtypeStruct(shape, F32),
        grid=(m // tr,),
        in_specs=[pl.BlockSpec((tr, GATE_W), lambda i: (i, 0)),
                  pl.BlockSpec((1, GATE_W), lambda i: (0, 0))],
        out_specs=pl.BlockSpec((tr // L,) + shape[1:], lambda i: (i, 0, 0, 0)),
        compiler_params=_params(("parallel",), blocks, 2 * 1024 * 1024),
        name="mlstm_gate_prep",
    )(gates, gate_b)


def _mlstm_kernel(qf_ref, ktf_ref, vf_ref, gpf_ref, qb_ref, ktb_ref, vb_ref, gpb_ref, s0_ref, m0_ref,
                  hf_ref, hb_ref, sf_ref, mf_ref, cst, mst, *, nc):
    c = pl.program_id(1)
    L = M_CHUNK
    hd = HEAD_DIM
    nh = C_HEADS

    @pl.when(c == 0)
    def _():
        cst[...] = s0_ref[...]
        mst[...] = m0_ref[...]

    ii = lax.broadcasted_iota(jnp.int32, (L, L), 0)
    jj = lax.broadcasted_iota(jnp.int32, (L, L), 1)
    ones = jnp.ones((L, hd), BF16)
    dirs = ((qf_ref, ktf_ref, vf_ref, gpf_ref, hf_ref, ii >= jj), (qb_ref, ktb_ref, vb_ref, gpb_ref, hb_ref, ii <= jj))
    for d, (q_ref, kt_ref, v_ref, gp_ref, h_ref, causal) in enumerate(dirs):
        r_row, rmax, bcum, total, w_row, r_top = (gp_ref[0, 0, k * nh:(k + 1) * nh, :] for k in range(GP_PLANES))
        m0 = mst[d]
        big_m = jnp.maximum(rmax, m0)
        si_row = jnp.exp(m0 - big_m)
        nrm_row = jnp.exp(-(bcum + big_m))
        m_loc = total + r_top
        m_new = jnp.maximum(total + m0, m_loc)
        s_prev = jnp.exp(total + m0 - m_new)
        s_loc = jnp.exp(m_loc - m_new)
        mst[d] = m_new
        xt = jnp.concatenate([big_m, si_row, nrm_row, jnp.zeros((L - 3 * nh, L), F32)], axis=0).T
        for h in range(nh):
            cs = slice(h * hd, (h + 1) * hd)
            qs = q_ref[:, cs]
            kt = kt_ref[0, cs, :]
            vext = jnp.concatenate([v_ref[:, cs], ones], axis=1)
            cext = cst[d * nh + h]
            expo = jnp.where(causal, r_row[h:h + 1, :] - xt[:, h:h + 1], NEG)
            sw = jnp.exp(expo) * _dot(qs, kt)
            si = jnp.broadcast_to(xt[:, nh + h:nh + h + 1], (L, 2 * hd))
            both = _dot(sw.astype(BF16), vext) + si * _dot(qs, cext.astype(BF16))
            nrm = jnp.broadcast_to(xt[:, 2 * nh + h:2 * nh + h + 1], (L, hd))
            h_ref[:, cs] = both[:, 0:hd] / jnp.maximum(jnp.abs(both[:, hd:2 * hd]), nrm)
            kwt = (kt.astype(F32) * w_row[h:h + 1, :]).astype(BF16)
            sp = jnp.concatenate([s_prev[h:h + 1, :], s_prev[h:h + 1, :]], axis=1)
            sl = jnp.concatenate([s_loc[h:h + 1, :], s_loc[h:h + 1, :]], axis=1)
            cst[d * nh + h] = sp * cext + sl * _dot(kwt, vext)

    @pl.when(c == nc - 1)
    def _():
        sf_ref[...] = cst[...]
        mf_ref[...] = mst[...]


def _mlstm_call(qs, kt, vb, gplanes, s0, m0, bsz):
    t = qs.shape[0] // bsz
    L = M_CHUNK
    nc = t // L
    nh = C_HEADS
    hw = nh * HEAD_DIM

    def cidx(d, b, c):
        return b * nc + (c if d == 0 else nc - 1 - c)

    def dir_specs(d):
        return [pl.BlockSpec((L, hw), lambda b, c: (cidx(d, b, c), 0)),
                pl.BlockSpec((1, hw, L), lambda b, c: (cidx(d, b, c), 0, 0)),
                pl.BlockSpec((L, hw), lambda b, c: (cidx(d, b, c), 0)),
                pl.BlockSpec((1, 1, GP_PLANES * nh, L), lambda b, c: (cidx(d, b, c), d, 0, 0))]

    sspec = pl.BlockSpec((2 * nh, HEAD_DIM, 2 * HEAD_DIM), lambda b, c: (b, 0, 0))
    mspec = pl.BlockSpec((2, nh, 128), lambda b, c: (b, 0, 0))
    blocks = (6 * _nbytes((L, hw), BF16) + 2 * _nbytes((GP_PLANES * nh, L), F32)
              + 4 * _nbytes((nh, HEAD_DIM, 2 * HEAD_DIM), F32) + 2 * _nbytes((L, hw), F32))
    scratch = 2 * _nbytes((nh, HEAD_DIM, 2 * HEAD_DIM), F32)
    hshape = jax.ShapeDtypeStruct((bsz * t, hw), F32)
    return pl.pallas_call(
        functools.partial(_mlstm_kernel, nc=nc),
        out_shape=[hshape, hshape, jax.ShapeDtypeStruct(s0.shape, F32), jax.ShapeDtypeStruct(m0.shape, F32)],
        grid=(bsz, nc),
        in_specs=dir_specs(0) + dir_specs(1) + [sspec, mspec],
        out_specs=[pl.BlockSpec((L, hw), lambda b, c: (cidx(0, b, c), 0)),
                   pl.BlockSpec((L, hw), lambda b, c: (cidx(1, b, c), 0)), sspec, mspec],
        scratch_shapes=[pltpu.VMEM((2 * nh, HEAD_DIM, 2 * HEAD_DIM), F32), pltpu.VMEM((2, nh, 128), F32)],
        compiler_params=_params(("parallel", "arbitrary"), blocks, scratch + 16 * 1024 * 1024),
        name="mlstm_scan",
    )(qs, kt, vb, gplanes, qs, kt, vb, gplanes, s0, m0)


def _mlstm_out_kernel(hf_ref, hb_ref, o_ref, g_ref, y_ref):
    hd = HEAD_DIM
    for h in range(C_HEADS):
        cs = slice(h * hd, (h + 1) * hd)
        hn = _head_norm(hf_ref[:, cs] + hb_ref[:, cs], g_ref[:, cs])
        y_ref[:, cs] = (hn * jax.nn.sigmoid(o_ref[:, cs])).astype(y_ref.dtype)


def _mlstm_out_call(hf, hb, p_main, norm_g):
    m, hw = hf.shape
    tr = _pick(m, (256, 128))
    blocks = 3 * _nbytes((tr, hw), F32) + _nbytes((tr, hw), BF16)
    hspec = pl.BlockSpec((tr, hw), lambda i: (i, 0))
    return pl.pallas_call(
        _mlstm_out_kernel,
        out_shape=jax.ShapeDtypeStruct((m, hw), BF16),
        grid=(m // tr,),
        in_specs=[hspec, hspec,
                  pl.BlockSpec((tr, hw), lambda i: (i, COL_CO)),
                  pl.BlockSpec((1, hw), lambda i: (0, 0))],
        out_specs=hspec,
        compiler_params=_params(("parallel",), blocks, 2 * 1024 * 1024),
        name="mlstm_out",
    )(hf, hb, p_main, norm_g.reshape(1, hw))


def _rope_lane_tables(n):
    t = jnp.arange(n)
    inv = ROPE_THETA ** (-jnp.arange(ROPE_PAIRS, dtype=F32) / ROPE_PAIRS)
    row = (t // GRID_W).astype(F32)[:, None] * inv
    col = (t % GRID_W).astype(F32)[:, None] * inv
    cos = jnp.concatenate([jnp.cos(row), jnp.cos(row), jnp.cos(col), jnp.cos(col)], axis=1)
    sin = jnp.concatenate([-jnp.sin(row), jnp.sin(row), -jnp.sin(col), jnp.sin(col)], axis=1)
    return cos, sin


def _split_w_in(w_in):
    aq_end = BRANCH_W
    akv_end = aq_end + 2 * A_KV_HEADS * HEAD_DIM
    main = jnp.concatenate([w_in[..., :aq_end], w_in[..., akv_end:MAIN_W], w_in[..., aq_end:akv_end]], axis=-1)
    gates = jnp.pad(w_in[..., MAIN_W:], ((0, 0), (0, 0), (0, GATE_W - 4 * C_HEADS)))
    return main.astype(BF16), gates.astype(BF16)


def kernel(x, c, ctx, c_ctx, w_mod, b_mod, norm_g, ffn_w1, ffn_w3, ffn_w2, w_in, qk_g, attn_sink, na_relpos,
           mlstm_conv_w, mlstm_conv_b, mlstm_gate_b, mlstm_norm_g, w_gate, b_gate, w_branch, w_out):
    bsz, n, d = x.shape
    lc = ctx.shape[1]
    depth = w_mod.shape[0]
    assert n % GRID_W == 0 and n % M_CHUNK == 0 and lc % M_CHUNK == 0 and n % A_BLOCK == 0

    cs = jnp.zeros((8, d), F32).at[:bsz].set(c).at[bsz].set(c_ctx)
    mods = _mod_call(cs, w_mod, b_mod).reshape(depth, 8, N_MOD, d)
    cos_t, sin_t = _rope_lane_tables(n)

    xl = x.reshape(bsz * n, d)
    xc = ctx.reshape(bsz * lc, d)
    nh = C_HEADS
    wg = w_gate.astype(BF16)
    wb = w_branch.astype(BF16)
    w_main, w_gates = _split_w_in(w_in)
    s_zero = jnp.zeros((bsz * 2 * nh, HEAD_DIM, 2 * HEAD_DIM), F32)
    m_zero = jnp.zeros((bsz * 2, nh, 128), F32)

    def ffn(xs, md, k0, g, layer, sub):
        u = _normmod_call(xs, g, md[:, k0], md[:, k0 + 1])
        h = _up_call(u, ffn_w1, ffn_w3, layer, sub)
        return _down_call(h, ffn_w2, xs, md[:, k0 + 2], 0.5, lead=(layer, sub))

    for i in range(depth):
        last = i == depth - 1
        ml = mods[i, :bsz]
        mc = mods[i, bsz:bsz + 1]
        gate_b = jnp.pad(mlstm_gate_b[i].reshape(1, 4 * nh), ((0, 0), (0, GATE_W - 4 * nh)))
        conv_b = mlstm_conv_b[i].reshape(1, -1)
        bias_tab = _na_bias_table(na_relpos[i])

        xc = ffn(xc, mc, 0, norm_g[i, 0], i, 0)
        xl = ffn(xl, ml, 0, norm_g[i, 0], i, 0)

        uc = _normmod_call(xc, norm_g[i, 1], mc[:, 3], mc[:, 4])
        ul = _normmod_call(xl, norm_g[i, 1], ml[:, 3], ml[:, 4])
        pc = _inproj_call(uc, w_main, i)
        pl_ = _inproj_call(ul, w_main, i)
        gc = _matmul_call(uc, w_gates[i])
        gl = _matmul_call(ul, w_gates[i])

        qa_c, qb_c, kb_c, vb_c, ka_c, va_c = _qkprep_call(pc, qk_g[i], cos_t, sin_t, False, lc)
        qa_l, qb_l, kb_l, vb_l, ka_l, va_l = _qkprep_call(pl_, qk_g[i], cos_t, sin_t, True, n)

        a_l = _attn_a_call(qa_l, ka_l, va_l, ka_c, va_c, attn_sink[i], bsz, True)
        n_l = _attn_b_call(qb_l, kb_l, vb_l, kb_c, vb_c, bias_tab, bsz)
        qc_c, kc_c, vc_c = _cprep_call(pc, mlstm_conv_w[i], conv_b, lc)
        qc_l, kc_l, vc_l = _cprep_call(pl_, mlstm_conv_w[i], conv_b, n)
        hc_f, hc_b, s_c, m_c = _mlstm_call(qc_c, kc_c, vc_c, _gateprep_call(gc, gate_b, lc), s_zero, m_zero, bsz)
        hl_f, hl_b, _, _ = _mlstm_call(qc_l, kc_l, vc_l, _gateprep_call(gl, gate_b, n), s_c, m_c, bsz)
        m_l = _mlstm_out_call(hl_f, hl_b, pl_, mlstm_norm_g[i])

        yl = _merge_call(ul, a_l, n_l, m_l, wg, b_gate, wb, i)
        xl = _down_call(yl, w_out, xl, ml[:, 5], 1.0, lead=(i,))
        xl = ffn(xl, ml, 6, norm_g[i, 2], i, 1)
        if not last:
            a_c = _attn_a_call(qa_c, ka_c, va_c, ka_c, va_c, attn_sink[i], bsz, False)
            n_c = _attn_bx_call(qb_c, kb_c, vb_c, bsz)
            mm_c = _mlstm_out_call(hc_f, hc_b, pc, mlstm_norm_g[i])
            yc = _merge_call(uc, a_c, n_c, mm_c, wg, b_gate, wb, i)
            xc = _down_call(yc, w_out, xc, mc[:, 5], 1.0, lead=(i,))
            xc = ffn(xc, mc, 6, norm_g[i, 2], i, 1)
    return xl.reshape(bsz, n, d)
```

```python
import functools

import numpy as np
import jax
import jax.numpy as jnp
from jax import lax
from jax.experimental import pallas as pl
from jax.experimental.pallas import tpu as pltpu

F32 = jnp.float32
BF16 = jnp.bfloat16

HEAD_DIM = 128
GRID_W = 64
ROPE_PAIRS = HEAD_DIM // 4
ROPE_THETA = 10000.0
A_HEADS = 8
A_KV_HEADS = 2
A_GROUP = A_HEADS // A_KV_HEADS
A_WINDOW = 128
A_BLOCK = 128
B_HEADS = 8
NA_ROWS = 8
NA_COLS = 16
C_HEADS = 8
C_CONV = 5
BRANCH_W = 8 * HEAD_DIM
N_MOD = 9
EPS = 1e-6
NEG = -1e30
ATTN_SCALE = HEAD_DIM ** -0.5
LOG2E = 1.4426950408889634
QK_SCALE_LOG2 = ATTN_SCALE * LOG2E

M_CHUNK = 128
NA_TILE_ROWS = 8
NA_TILE = NA_TILE_ROWS * GRID_W
NA_WIN_ROWS = NA_TILE_ROWS + NA_ROWS
NA_HEADS_PER_STEP = 4
A_QBLOCKS = 2

BF16_SUBLANES = 16
V7X_VMEM_BYTES = 64 * 1024 * 1024
VMEM_CAP = V7X_VMEM_BYTES - 6 * 1024 * 1024

COL_AQ, COL_BQ, COL_BK, COL_BV, COL_CQ, COL_CK, COL_CV, COL_CO = range(8)
MAIN_W = 8 * BRANCH_W + 2 * A_KV_HEADS * HEAD_DIM
GATE_W = 128


def _pick(n, cands):
    for c in cands:
        if n % c == 0:
            return c
    raise ValueError(f"no tile in {cands} divides {n}")


def _params(sem, block_bytes, temp_bytes=0):
    limit = 2 * block_bytes + temp_bytes + 4 * 1024 * 1024
    return pltpu.CompilerParams(dimension_semantics=sem,
                                vmem_limit_bytes=int(min(max(limit, 16 * 1024 * 1024), VMEM_CAP)))


def _nbytes(shape, dtype):
    return int(np.prod(shape)) * jnp.dtype(dtype).itemsize


def _dot(a, b):
    return jnp.dot(a, b, preferred_element_type=F32)


def _dot_nt(a, b):
    return lax.dot_general(a, b, (((1,), (1,)), ((), ())), preferred_element_type=F32)


def _dot_tn(a, b):
    return lax.dot_general(a, b, (((0,), (0,)), ((), ())), preferred_element_type=F32)


def _silu(x):
    return x * jax.nn.sigmoid(x)


def _mod_kernel(c_ref, w_ref, b_ref, o_ref):
    a = _silu(c_ref[...]).astype(BF16)
    o_ref[0] = _dot(a, w_ref[0].astype(BF16)) + b_ref[0]


def _mod_call(cs, w_mod, b_mod):
    depth, d, nd = w_mod.shape
    tn = _pick(nd, (1024, 512, 256, 128))
    rows = cs.shape[0]
    blocks = _nbytes((d, tn), F32) + _nbytes((rows, d), F32) + _nbytes((rows, tn), F32)
    return pl.pallas_call(
        _mod_kernel,
        out_shape=jax.ShapeDtypeStruct((depth, rows, nd), F32),
        grid=(depth, nd // tn),
        in_specs=[pl.BlockSpec((rows, d), lambda l, j: (0, 0)),
                  pl.BlockSpec((1, d, tn), lambda l, j: (l, 0, j)),
                  pl.BlockSpec((1, 1, tn), lambda l, j: (l, 0, j))],
        out_specs=pl.BlockSpec((1, rows, tn), lambda l, j: (l, 0, j)),
        compiler_params=_params(("parallel", "parallel"), blocks, _nbytes((d, tn), BF16)),
        name="mod_vectors",
    )(cs, w_mod, b_mod.reshape(depth, 1, nd))


def _normmod_kernel(x_ref, g_ref, shift_ref, scale_ref, o_ref):
    x = x_ref[...]
    ms = jnp.mean(x * x, axis=-1, keepdims=True)
    y = x * lax.rsqrt(ms + EPS) * g_ref[...]
    o_ref[...] = (y * (1.0 + scale_ref[0]) + shift_ref[0]).astype(o_ref.dtype)


def _normmod_call(x, g, shift, scale):
    m, d = x.shape
    groups = shift.shape[0]
    tr = _pick(m // groups, (512, 256, 128, 64, 8))
    per = (m // groups) // tr
    blocks = _nbytes((tr, d), F32) + _nbytes((tr, d), BF16) + 3 * _nbytes((1, d), F32)
    return pl.pallas_call(
        _normmod_kernel,
        out_shape=jax.ShapeDtypeStruct((m, d), BF16),
        grid=(m // tr,),
        in_specs=[pl.BlockSpec((tr, d), lambda i: (i, 0)),
                  pl.BlockSpec((1, d), lambda i: (0, 0)),
                  pl.BlockSpec((1, 1, d), lambda i: (i // per, 0, 0)),
                  pl.BlockSpec((1, 1, d), lambda i: (i // per, 0, 0))],
        out_specs=pl.BlockSpec((tr, d), lambda i: (i, 0)),
        compiler_params=_params(("parallel",), blocks, 2 * _nbytes((tr, d), F32)),
        name="norm_modulate",
    )(x, g.reshape(1, d), shift.reshape(groups, 1, d), scale.reshape(groups, 1, d))


def _prefetch_chunks(tm, nj):
    n = 1
    while 2 * n <= nj and tm % (2 * n) == 0 and (tm // (2 * n)) % BF16_SUBLANES == 0:
        n *= 2
    return n


def _rowblock(a_hbm, abuf, sem, *, ni, nj, tm):
    i = pl.program_id(0)
    j = pl.program_id(1)
    slot = i % 2
    nch = _prefetch_chunks(tm, nj)
    ch = tm // nch

    def chunk_copy(blk, c, slot_):
        return pltpu.make_async_copy(a_hbm.at[pl.ds(blk * tm + c * ch, ch), :],
                                     abuf.at[slot_, pl.ds(c * ch, ch), :], sem.at[slot_])

    @pl.when((i == 0) & (j == 0))
    def _():
        for c in range(nch):
            chunk_copy(0, c, 0).start()

    @pl.when(j == 0)
    def _():
        for c in range(nch):
            chunk_copy(i, c, slot).wait()

    @pl.when((i + 1 < ni) & (j < nch))
    def _():
        chunk_copy(i + 1, j, 1 - slot).start()

    return abuf[slot]


def _rowblock_scratch(tm, k):
    return [pltpu.VMEM((2, tm, k), BF16), pltpu.SemaphoreType.DMA((2,))]


def _up_kernel(u_hbm, w1_ref, w3_ref, o_ref, ubuf, sem, *, ni, nj, tm):
    u = _rowblock(u_hbm, ubuf, sem, ni=ni, nj=nj, tm=tm)
    h1 = _dot(u, w1_ref[...].astype(BF16))
    h3 = _dot(u, w3_ref[...].astype(BF16))
    o_ref[...] = (_silu(h1) * h3).astype(o_ref.dtype)


def _up_call(u, w1, w3, layer, sub):
    m, d = u.shape
    f = w1.shape[-1]
    tm = _pick(m, (2048, 1024, 512, 256))
    tf = _pick(f, (256, 128))
    wspec = pl.BlockSpec((None, None, d, tf), lambda i, j: (layer, sub, 0, j))
    ni, nj = m // tm, f // tf
    blocks = 2 * _nbytes((d, tf), F32) + _nbytes((tm, tf), BF16)
    return pl.pallas_call(
        functools.partial(_up_kernel, ni=ni, nj=nj, tm=tm),
        out_shape=jax.ShapeDtypeStruct((m, f), BF16),
        grid=(ni, nj),
        in_specs=[pl.BlockSpec(memory_space=pl.ANY), wspec, wspec],
        out_specs=pl.BlockSpec((tm, tf), lambda i, j: (i, j)),
        scratch_shapes=_rowblock_scratch(tm, d),
        compiler_params=_params(("arbitrary", "arbitrary"), blocks,
                                2 * _nbytes((tm, d), BF16) + 2 * _nbytes((d, tf), BF16)
                                + 4 * _nbytes((tm, tf), F32)),
        name="ffn_up",
    )(u, w1, w3)


def _down_kernel(a_hbm, w_ref, x_ref, gate_ref, o_ref, abuf, sem, *, coef, ni, nj, tm):
    a = _rowblock(a_hbm, abuf, sem, ni=ni, nj=nj, tm=tm)
    acc = _dot(a, w_ref[...].astype(BF16))
    o_ref[...] = x_ref[...] + (coef * gate_ref[0]) * acc


def _down_call(a, w, x, gate, coef, lead=()):
    m, k = a.shape
    d = w.shape[-1]
    groups = gate.shape[0]
    tm = _pick(m // groups, (1024, 512, 256))
    tn = _pick(d, (512, 256, 128) if k <= 4096 else (256, 128))
    per = (m // groups) // tm
    ni, nj = m // tm, d // tn
    blocks = _nbytes((k, tn), w.dtype) + 2 * _nbytes((tm, tn), F32) + _nbytes((1, tn), F32)
    temps = (2 * _nbytes((tm, k), BF16) + 2 * _nbytes((tm, tn), F32)
             + (_nbytes((k, tn), BF16) if w.dtype != BF16 else 0))
    wspec = pl.BlockSpec((None,) * len(lead) + (k, tn), lambda i, j: tuple(lead) + (0, j))
    return pl.pallas_call(
        functools.partial(_down_kernel, coef=coef, ni=ni, nj=nj, tm=tm),
        out_shape=jax.ShapeDtypeStruct((m, d), F32),
        grid=(ni, nj),
        in_specs=[pl.BlockSpec(memory_space=pl.ANY),
                  wspec,
                  pl.BlockSpec((tm, tn), lambda i, j: (i, j)),
                  pl.BlockSpec((1, 1, tn), lambda i, j: (i // per, 0, j))],
        out_specs=pl.BlockSpec((tm, tn), lambda i, j: (i, j)),
        scratch_shapes=_rowblock_scratch(tm, k),
        compiler_params=_params(("arbitrary", "arbitrary"), blocks, temps),
        name="proj_residual",
    )(a, w, x, gate.reshape(groups, 1, d))


def _matmul_kernel(a_ref, w_ref, o_ref):
    o_ref[...] = _dot(a_ref[...], w_ref[...]).astype(o_ref.dtype)


def _inproj_kernel(u_hbm, w_ref, o_ref, ubuf, sem, *, ni, nj, tm):
    u = _rowblock(u_hbm, ubuf, sem, ni=ni, nj=nj, tm=tm)
    o_ref[...] = _dot(u, w_ref[...])


IN_TN = 2 * A_KV_HEADS * HEAD_DIM


def _inproj_call(u, w_main, layer):
    m, d = u.shape
    tm = _pick(m, (2048, 1024, 512, 256))
    tn = IN_TN
    ni, nj = m // tm, MAIN_W // tn
    blocks = _nbytes((d, tn), BF16) + _nbytes((tm, tn), F32)
    return pl.pallas_call(
        functools.partial(_inproj_kernel, ni=ni, nj=nj, tm=tm),
        out_shape=jax.ShapeDtypeStruct((m, MAIN_W), F32),
        grid=(ni, nj),
        in_specs=[pl.BlockSpec(memory_space=pl.ANY),
                  pl.BlockSpec((None, d, tn), lambda i, j: (layer, 0, j))],
        out_specs=pl.BlockSpec((tm, tn), lambda i, j: (i, j)),
        scratch_shapes=_rowblock_scratch(tm, d),
        compiler_params=_params(("arbitrary", "arbitrary"), blocks,
                                2 * _nbytes((tm, d), BF16) + 2 * _nbytes((tm, tn), F32)),
        name="in_proj",
    )(u, w_main)


def _matmul_call(a, w, out_dtype=F32):
    m, k = a.shape
    n = w.shape[1]
    tm = _pick(m, (1024, 512, 256))
    tn = _pick(n, (512, 256, 128))
    blocks = _nbytes((tm, k), BF16) + _nbytes((k, tn), BF16) + _nbytes((tm, tn), out_dtype)
    return pl.pallas_call(
        _matmul_kernel,
        out_shape=jax.ShapeDtypeStruct((m, n), out_dtype),
        grid=(m // tm, n // tn),
        in_specs=[pl.BlockSpec((tm, k), lambda i, j: (i, 0)),
                  pl.BlockSpec((k, tn), lambda i, j: (0, j))],
        out_specs=pl.BlockSpec((tm, tn), lambda i, j: (i, j)),
        compiler_params=_params(("parallel", "parallel"), blocks, _nbytes((tm, tn), F32)),
        name="in_proj",
    )(a, w)


def _merge_kernel(u_ref, oa_ref, ob_ref, oc_ref, wg_ref, bg_ref, wb_ref, y_ref):
    u = u_ref[...]
    acc = None
    for j, br_ref in enumerate((oa_ref, ob_ref, oc_ref)):
        gate = jax.nn.sigmoid(_dot(u, wg_ref[j]) + bg_ref[j])
        term = gate * _dot(br_ref[...], wb_ref[j])
        acc = term if acc is None else acc + term
    y_ref[...] = acc.astype(y_ref.dtype)


def _merge_call(u, oa, ob, oc, wg, bg, wb, layer):
    m, d = u.shape
    bw = oa.shape[1]
    tm = _pick(m, (1024, 512, 256))
    tn = _pick(d, (256, 128))
    blocks = (_nbytes((tm, d), BF16) + 3 * _nbytes((tm, bw), BF16) + 3 * _nbytes((d, tn), BF16)
              + 3 * _nbytes((bw, tn), BF16) + _nbytes((tm, tn), BF16))
    return pl.pallas_call(
        _merge_kernel,
        out_shape=jax.ShapeDtypeStruct((m, d), BF16),
        grid=(m // tm, d // tn),
        in_specs=[pl.BlockSpec((tm, d), lambda i, j: (i, 0)),
                  pl.BlockSpec((tm, bw), lambda i, j: (i, 0)),
                  pl.BlockSpec((tm, bw), lambda i, j: (i, 0)),
                  pl.BlockSpec((tm, bw), lambda i, j: (i, 0)),
                  pl.BlockSpec((None, 3, d, tn), lambda i, j: (layer, 0, 0, j)),
                  pl.BlockSpec((None, 3, 1, tn), lambda i, j: (layer, 0, 0, j)),
                  pl.BlockSpec((None, 3, bw, tn), lambda i, j: (layer, 0, 0, j))],
        out_specs=pl.BlockSpec((tm, tn), lambda i, j: (i, j)),
        compiler_params=_params(("parallel", "parallel"), blocks, 10 * _nbytes((tm, tn), F32)),
        name="gated_merge",
    )(u, oa, ob, oc, wg, bg.reshape(bg.shape[0], 3, 1, d), wb)


def _head_norm(x, g):
    ms = jnp.mean(x * x, axis=-1, keepdims=True)
    return x * lax.rsqrt(ms + EPS) * g


def _rope(y, cos, sin_signed):
    lane = lax.broadcasted_iota(jnp.int32, y.shape, 1)
    partner = jnp.where((lane % 64) < ROPE_PAIRS, pltpu.roll(y, HEAD_DIM - ROPE_PAIRS, axis=1),
                        pltpu.roll(y, ROPE_PAIRS, axis=1))
    return y * cos + partner * sin_signed


def _qkprep_kernel(p4_ref, pa_ref, g_ref, cos_ref, sin_ref,
                   qa_ref, qb_ref, kb_ref, vb_ref, ka_ref, va_ref, *, rope):
    hd = HEAD_DIM
    if rope:
        cos = cos_ref[...]
        sin = sin_ref[...]
    for h in range(A_HEADS):
        y = _head_norm(p4_ref[:, h * hd:(h + 1) * hd], g_ref[0:1, :])
        if rope:
            y = _rope(y, cos, sin)
        qa_ref[:, h * hd:(h + 1) * hd] = y.astype(BF16)
    for h in range(B_HEADS):
        c0 = BRANCH_W + h * hd
        qb_ref[:, h * hd:(h + 1) * hd] = _head_norm(p4_ref[:, c0:c0 + hd], g_ref[2:3, :]).astype(BF16)
        c0 = 2 * BRANCH_W + h * hd
        kb_ref[:, h * hd:(h + 1) * hd] = _head_norm(p4_ref[:, c0:c0 + hd], g_ref[3:4, :]).astype(BF16)
    vb_ref[...] = p4_ref[:, 3 * BRANCH_W:4 * BRANCH_W].astype(BF16)
    for h in range(A_KV_HEADS):
        y = _head_norm(pa_ref[:, h * hd:(h + 1) * hd], g_ref[1:2, :])
        if rope:
            y = _rope(y, cos, sin)
        ka_ref[:, h * hd:(h + 1) * hd] = y.astype(BF16)
    kvw = A_KV_HEADS * hd
    va_ref[...] = pa_ref[:, kvw:2 * kvw].astype(BF16)


def _qkprep_call(p_main, qk_g, cos_t, sin_t, rope, seq):
    m = p_main.shape[0]
    tr = _pick(seq, (256, 128))
    per = seq // tr
    kvw = A_KV_HEADS * HEAD_DIM
    w4 = 4 * BRANCH_W
    blocks = (_nbytes((tr, w4), F32) + _nbytes((tr, 2 * kvw), F32) + 2 * _nbytes((tr, HEAD_DIM), F32)
              + _nbytes((tr, w4), BF16) + _nbytes((tr, 2 * kvw), BF16))
    outs = [jax.ShapeDtypeStruct((m, BRANCH_W), BF16)] * 4 + [jax.ShapeDtypeStruct((m, kvw), BF16)] * 2
    return pl.pallas_call(
        functools.partial(_qkprep_kernel, rope=rope),
        out_shape=outs,
        grid=(m // tr,),
        in_specs=[pl.BlockSpec((tr, w4), lambda i: (i, 0)),
                  pl.BlockSpec((tr, 2 * kvw), lambda i: (i, (8 * BRANCH_W) // (2 * kvw))),
                  pl.BlockSpec((4, HEAD_DIM), lambda i: (0, 0)),
                  pl.BlockSpec((tr, HEAD_DIM), lambda i: (i % per, 0)),
                  pl.BlockSpec((tr, HEAD_DIM), lambda i: (i % per, 0))],
        out_specs=[pl.BlockSpec((tr, BRANCH_W), lambda i: (i, 0))] * 4
                  + [pl.BlockSpec((tr, kvw), lambda i: (i, 0))] * 2,
        compiler_params=_params(("parallel",), blocks, 4 * _nbytes((tr, HEAD_DIM), F32)),
        name="qk_prep",
    )(p_main, p_main, qk_g, cos_t, sin_t)


def _attn_a_kernel(sink_ref, q_ref, *rest, lc, window):
    nwin = A_QBLOCKS + 2 if window else 0
    nmask = A_QBLOCKS if window else 0
    k_refs = rest[:nwin]
    v_refs = rest[nwin:2 * nwin]
    mask_refs = rest[2 * nwin:2 * nwin + nmask]
    kx_ref, vx_ref, o_ref = rest[2 * nwin + nmask:]
    hd = HEAD_DIM
    blk = A_BLOCK
    gw = A_GROUP * hd
    keys = lc + 3 * blk if window else lc
    ones = jnp.ones((keys, hd), BF16)
    for sb in range(A_QBLOCKS):
        rows = slice(sb * blk, (sb + 1) * blk)
        for hk in range(A_KV_HEADS):
            cs = slice(hk * hd, (hk + 1) * hd)
            if window:
                kall = jnp.concatenate([kx_ref[:, cs]] + [r[:, cs] for r in k_refs[sb:sb + 3]], axis=0)
                vall = jnp.concatenate([vx_ref[:, cs]] + [r[:, cs] for r in v_refs[sb:sb + 3]], axis=0)
            else:
                kall = kx_ref[:, cs]
                vall = vx_ref[:, cs]
            q4 = jnp.concatenate([q_ref[rows, hk * gw + g * hd:hk * gw + (g + 1) * hd] for g in range(A_GROUP)],
                                 axis=0)
            s = _dot_nt(q4, kall) * QK_SCALE_LOG2
            if window:
                s = s + mask_refs[sb][0]
            snk = jnp.concatenate(
                [jnp.full((blk, 1), sink_ref[hk * A_GROUP + g] * LOG2E, F32) for g in range(A_GROUP)], axis=0)
            mx = jnp.maximum(jnp.max(s, axis=-1, keepdims=True), snk)
            e = jnp.exp2(s - mx).astype(BF16)
            both = _dot(e, jnp.concatenate([vall, ones], axis=1))
            o = both[:, 0:hd] * (1.0 / (both[:, hd:2 * hd] + jnp.exp2(snk - mx)))
            for g in range(A_GROUP):
                o_ref[rows, hk * gw + g * hd:hk * gw + (g + 1) * hd] = (
                    o[g * blk:(g + 1) * blk].astype(o_ref.dtype))


def _gqa_mask_table(lc, nb):
    blk = A_BLOCK
    r = np.arange(blk)[:, None]
    kc = np.arange(3 * blk)[None, :]
    band = np.abs(r + blk - kc) <= A_WINDOW
    tabs = []
    for first, last in ((True, False), (False, False), (False, True)):
        ok = band & ~(first & (kc < blk)) & ~(last & (kc >= 2 * blk))
        win = np.where(ok, 0.0, NEG).astype(np.float32)
        tabs.append(np.tile(np.concatenate([np.zeros((blk, lc), np.float32), win], axis=1), (A_GROUP, 1)))
    assert nb >= 2
    return np.stack(tabs)


def _attn_a_call(qa, ka, va, kx, vx, sink, bsz, window):
    n = qa.shape[0] // bsz
    lc = kx.shape[0] // bsz
    blk = A_BLOCK
    nb = n // blk
    qb = A_QBLOCKS
    assert nb % qb == 0
    nt = nb // qb
    kvw = A_KV_HEADS * HEAD_DIM
    qw = A_HEADS * HEAD_DIM
    qspec = pl.BlockSpec((qb * blk, qw), lambda b, t: (b * nt + t, 0))
    xspec = pl.BlockSpec((lc, kvw), lambda b, t: (b, 0))
    sspec = pl.BlockSpec(memory_space=pltpu.SMEM)
    if window:
        def kspec(off):
            return pl.BlockSpec((blk, kvw), lambda b, t: (b * nb + jnp.clip(t * qb + off, 0, nb - 1), 0))

        def mspec(sb):
            def variant(b, t):
                i = t * qb + sb
                return (jnp.where(i == 0, 0, jnp.where(i == nb - 1, 2, 1)), 0, 0)
            return pl.BlockSpec((1, A_GROUP * blk, lc + 3 * blk), variant)

        kspecs = [kspec(off) for off in range(-1, qb + 1)]
        mspecs = [mspec(sb) for sb in range(qb)]
        mask = jnp.asarray(_gqa_mask_table(lc, nb))
        in_specs = [sspec, qspec] + kspecs + kspecs + mspecs + [xspec, xspec]
        args = (sink, qa) + (ka,) * len(kspecs) + (va,) * len(kspecs) + (mask,) * qb + (kx, vx)
        keys = lc + 3 * blk
    else:
        in_specs = [sspec, qspec, xspec, xspec]
        args = (sink, qa, kx, vx)
        keys = lc
    blocks = (2 * _nbytes((qb * blk, qw), BF16) + 2 * _nbytes((keys + qb * blk, kvw), BF16)
              + (qb * _nbytes((A_GROUP * blk, keys), F32) if window else 0))
    return pl.pallas_call(
        functools.partial(_attn_a_kernel, lc=lc, window=window),
        out_shape=jax.ShapeDtypeStruct(qa.shape, BF16),
        grid=(bsz, nt),
        in_specs=in_specs,
        out_specs=qspec,
        compiler_params=_params(("parallel", "parallel"), blocks,
                                10 * qb * _nbytes((A_GROUP * blk, keys), F32)),
        name="windowed_gqa" if window else "context_gqa",
    )(*args)


def _softmax2_pv(s1, v1, s2, v2):
    hd = v1.shape[1]
    mx = jnp.maximum(jnp.max(s1, axis=-1, keepdims=True), jnp.max(s2, axis=-1, keepdims=True))
    e1 = jnp.exp2(s1 - mx).astype(BF16)
    e2 = jnp.exp2(s2 - mx).astype(BF16)
    v1e = jnp.concatenate([v1, jnp.ones(v1.shape, BF16)], axis=1)
    v2e = jnp.concatenate([v2, jnp.ones(v2.shape, BF16)], axis=1)
    both = _dot(e1, v1e) + _dot(e2, v2e)
    return both[:, 0:hd] * (1.0 / both[:, hd:2 * hd])


def _attn_b_kernel(q_ref, k0_ref, k1_ref, k2_ref, k3_ref, v0_ref, v1_ref, v2_ref, v3_ref, kx_ref, vx_ref,
                   bias_ref, o_ref):
    hd = HEAD_DIM
    for h in range(NA_HEADS_PER_STEP):
        cs = slice(h * hd, (h + 1) * hd)
        kwin = jnp.concatenate([k0_ref[:, cs], k1_ref[:, cs], k2_ref[:, cs], k3_ref[:, cs]], axis=0)
        vwin = jnp.concatenate([v0_ref[:, cs], v1_ref[:, cs], v2_ref[:, cs], v3_ref[:, cs]], axis=0)
        q = q_ref[:, cs]
        s_nb = _dot_nt(q, kwin) * QK_SCALE_LOG2 + bias_ref[h, 0]
        s_cx = _dot_nt(q, kx_ref[:, cs]) * QK_SCALE_LOG2
        o_ref[:, cs] = _softmax2_pv(s_nb, vwin, s_cx, vx_ref[:, cs]).astype(o_ref.dtype)


def _attn_bx_kernel(q_ref, kx_ref, vx_ref, o_ref):
    hd = HEAD_DIM
    for h in range(B_HEADS):
        cs = slice(h * hd, (h + 1) * hd)
        s = _dot_nt(q_ref[:, cs], kx_ref[:, cs]) * QK_SCALE_LOG2
        mx = jnp.max(s, axis=-1, keepdims=True)
        e = jnp.exp2(s - mx)
        inv = 1.0 / jnp.sum(e, axis=-1, keepdims=True)
        o_ref[:, cs] = (_dot(e.astype(BF16), vx_ref[:, cs]) * inv).astype(o_ref.dtype)


def _na_bias_kernel(rel_ref, o_ref):
    h = pl.program_id(0)
    pos = pl.program_id(1)
    nrel_r = 2 * NA_ROWS - 1
    nrel_c = 2 * NA_COLS - 1
    qi = lax.broadcasted_iota(jnp.int32, (GRID_W, GRID_W), 0)
    ki = lax.broadcasted_iota(jnp.int32, (GRID_W, GRID_W), 1)
    dc = jnp.clip(ki - qi, -(NA_COLS - 1), NA_COLS - 1) + NA_COLS - 1
    start = jnp.clip(qi - NA_COLS // 2, 0, GRID_W - NA_COLS)
    col_in = (ki >= start) & (ki < start + NA_COLS)
    planes = []
    for dr in range(nrel_r):
        acc = jnp.zeros((GRID_W, GRID_W), F32)
        for dcv in range(nrel_c):
            acc = jnp.where(dc == dcv, rel_ref[(h * nrel_r + dr) * nrel_c + dcv], acc)
        planes.append(jnp.where(col_in, acc * LOG2E, NEG))
    masked = jnp.full((GRID_W, GRID_W), NEG, F32)
    half = NA_ROWS // 2
    first_key = {0: lambda j: max(j, half), 1: lambda j: j, 2: lambda j: min(j, half)}
    for p, lo_of in first_key.items():
        @pl.when(pos == p)
        def _(lo_of=lo_of):
            for j in range(NA_TILE_ROWS):
                lo = lo_of(j)
                for jj in range(NA_WIN_ROWS):
                    blk = planes[jj - j + half - 1] if lo <= jj < lo + NA_ROWS else masked
                    o_ref[0, 0, j * GRID_W:(j + 1) * GRID_W, jj * GRID_W:(jj + 1) * GRID_W] = blk


def _na_bias_table(relpos):
    nh = relpos.shape[0]
    shape = (nh, 3, NA_TILE, NA_WIN_ROWS * GRID_W)
    return pl.pallas_call(
        _na_bias_kernel,
        out_shape=jax.ShapeDtypeStruct(shape, F32),
        grid=(nh, 3),
        in_specs=[pl.BlockSpec(memory_space=pltpu.SMEM)],
        out_specs=pl.BlockSpec((1, 1) + shape[2:], lambda h, p: (h, p, 0, 0)),
        compiler_params=_params(("parallel", "parallel"), _nbytes(shape[2:], F32)),
        name="na_bias_table",
    )(relpos.reshape(-1))


def _attn_b_call(qb, kb, vb, kx, vx, bias, bsz):
    n = qb.shape[0] // bsz
    lc = kx.shape[0] // bsz
    rows = n // GRID_W
    assert rows % NA_TILE_ROWS == 0 and rows >= 2 * NA_TILE_ROWS
    nt = rows // NA_TILE_ROWS
    hp = NA_HEADS_PER_STEP
    hd = hp * HEAD_DIM
    sub = NA_TILE // 2
    qspec = pl.BlockSpec((NA_TILE, hd), lambda b, h, t: (b * nt + t, h))

    def kspec(s):
        return pl.BlockSpec((sub, hd),
                            lambda b, h, t: (jnp.clip(2 * t + s, 0, 2 * nt - 1) + 2 * b * nt, h))

    kspecs = [kspec(s) for s in (-1, 0, 1, 2)]
    xspec = pl.BlockSpec((lc, hd), lambda b, h, t: (b, h))
    bspec = pl.BlockSpec((hp, 1) + bias.shape[2:],
                         lambda b, h, t: (h, jnp.where(t == 0, 0, jnp.where(t == nt - 1, 2, 1)), 0, 0))
    blocks = (2 * _nbytes((NA_TILE, hd), BF16) + 8 * _nbytes((sub, hd), BF16) + 2 * _nbytes((lc, hd), BF16)
              + hp * _nbytes(bias.shape[2:], F32))
    return pl.pallas_call(
        _attn_b_kernel,
        out_shape=jax.ShapeDtypeStruct(qb.shape, BF16),
        grid=(bsz, B_HEADS // hp, nt),
        in_specs=[qspec] + kspecs + kspecs + [xspec, xspec, bspec],
        out_specs=qspec,
        compiler_params=_params(("parallel", "parallel", "parallel"), blocks,
                                5 * hp * _nbytes((NA_TILE, NA_WIN_ROWS * GRID_W + lc), F32)),
        name="neighbourhood_attn",
    )(qb, kb, kb, kb, kb, vb, vb, vb, vb, kx, vx, bias)


def _attn_bx_call(qx, kx, vx, bsz):
    lc = qx.shape[0] // bsz
    w = qx.shape[1]
    spec = pl.BlockSpec((lc, w), lambda b: (b, 0))
    return pl.pallas_call(
        _attn_bx_kernel,
        out_shape=jax.ShapeDtypeStruct(qx.shape, BF16),
        grid=(bsz,),
        in_specs=[spec, spec, spec],
        out_specs=spec,
        compiler_params=_params(("parallel",), 4 * _nbytes((lc, w), BF16), 4 * 1024 * 1024),
        name="context_full_attn",
    )(qx, kx, vx)


def _log_sigmoid(x):
    return -(jnp.maximum(-x, 0.0) + jnp.log1p(jnp.exp(-jnp.abs(x))))


CONV_HALO = 8


def _cprep_kernel(q_ref, k_ref, v_ref, qlo_ref, qhi_ref, klo_ref, khi_ref, cw_ref, cb_ref,
                  qs_ref, kt_ref, vb_ref, ext, *, per):
    pos = pl.program_id(0) % per
    tr = q_ref.shape[0]
    hw = q_ref.shape[1]
    halo = CONV_HALO
    pad = C_CONV // 2

    def conv_silu(x_ref, lo_ref, hi_ref, col0):
        ext[0:halo, :] = jnp.where(pos > 0, lo_ref[...], 0.0)
        ext[halo:halo + tr, :] = x_ref[...]
        ext[halo + tr:2 * halo + tr, :] = jnp.where(pos < per - 1, hi_ref[...], 0.0)
        out = cb_ref[:, col0:col0 + hw]
        for j in range(C_CONV):
            out = out + cw_ref[j:j + 1, col0:col0 + hw] * ext[halo - pad + j:halo - pad + j + tr, :]
        return _silu(out)

    qs_ref[...] = (conv_silu(q_ref, qlo_ref, qhi_ref, 0) * ATTN_SCALE).astype(BF16)
    kc = conv_silu(k_ref, klo_ref, khi_ref, hw)
    for ck in range(tr // M_CHUNK):
        for h in range(C_HEADS):
            blk = kc[ck * M_CHUNK:(ck + 1) * M_CHUNK, h * HEAD_DIM:(h + 1) * HEAD_DIM]
            kt_ref[ck, h * HEAD_DIM:(h + 1) * HEAD_DIM, :] = blk.T.astype(BF16)
    vb_ref[...] = v_ref[...].astype(BF16)


def _cprep_call(p_main, conv_w, conv_b, seq):
    m = p_main.shape[0]
    hw = C_HEADS * HEAD_DIM
    assert M_CHUNK == HEAD_DIM
    tr = _pick(seq, (256, 128))
    per = seq // tr
    halo = CONV_HALO
    hb = tr // halo
    nblk = m // halo
    cpb = tr // M_CHUNK

    def main(col):
        return pl.BlockSpec((tr, hw), lambda i: (i, col))

    def lo(col):
        return pl.BlockSpec((halo, hw), lambda i: (jnp.maximum(i * hb - 1, 0), col))

    def hi(col):
        return pl.BlockSpec((halo, hw), lambda i: (jnp.minimum((i + 1) * hb, nblk - 1), col))

    blocks = (3 * _nbytes((tr, hw), F32) + 4 * _nbytes((halo, hw), F32) + _nbytes((C_CONV + 1, 2 * hw), F32)
              + 3 * _nbytes((tr, hw), BF16))
    scratch = _nbytes((tr + 2 * halo, hw), F32)
    out = jax.ShapeDtypeStruct((m, hw), BF16)
    out_t = jax.ShapeDtypeStruct((m // M_CHUNK, hw, M_CHUNK), BF16)
    ospec = pl.BlockSpec((tr, hw), lambda i: (i, 0))
    tspec = pl.BlockSpec((cpb, hw, M_CHUNK), lambda i: (i, 0, 0))
    return pl.pallas_call(
        functools.partial(_cprep_kernel, per=per),
        out_shape=[out, out_t, out],
        grid=(m // tr,),
        in_specs=[main(COL_CQ), main(COL_CK), main(COL_CV), lo(COL_CQ), hi(COL_CQ), lo(COL_CK), hi(COL_CK),
                  pl.BlockSpec((C_CONV, 2 * hw), lambda i: (0, 0)),
                  pl.BlockSpec((1, 2 * hw), lambda i: (0, 0))],
        out_specs=[ospec, tspec, ospec],
        scratch_shapes=[pltpu.VMEM((tr + 2 * halo, hw), F32)],
        compiler_params=_params(("parallel",), blocks, scratch + 4 * _nbytes((tr, hw), F32)),
        name="mlstm_prep",
    )(p_main, p_main, p_main, p_main, p_main, p_main, p_main, conv_w, conv_b)


GP_PLANES = 6


def _gateprep_kernel(g_ref, gb_ref, o_ref):
    nh = C_HEADS
    L = M_CHUNK
    lane = lax.broadcasted_iota(jnp.int32, (nh, L), 1)
    for ck in range(g_ref.shape[0] // L):
        gt = (g_ref[ck * L:(ck + 1) * L, :] + gb_ref[...]).T
        for d in range(2):
            ig = gt[2 * d * nh:(2 * d + 1) * nh]
            logf = _log_sigmoid(gt[(2 * d + 1) * nh:(2 * d + 2) * nh])
            pre = logf
            sft = 1
            while sft < L:
                pre = pre + jnp.where(lane >= sft, pltpu.roll(pre, sft, axis=1), 0.0)
                sft *= 2
            total = jnp.broadcast_to(pre[:, L - 1:L], (nh, L))
            bcum = pre if d == 0 else total - pre + logf
            r_row = ig - bcum
            rmax = r_row
            sft = 1
            while sft < L:
                if d == 0:
                    rmax = jnp.maximum(rmax, jnp.where(lane >= sft, pltpu.roll(rmax, sft, axis=1), NEG))
                else:
                    rmax = jnp.maximum(rmax, jnp.where(lane < L - sft, pltpu.roll(rmax, L - sft, axis=1), NEG))
                sft *= 2
            r_top = jnp.broadcast_to(rmax[:, L - 1:L] if d == 0 else rmax[:, 0:1], (nh, L))
            w_row = jnp.exp(r_row - r_top)
            for k, plane in enumerate((r_row, rmax, bcum, total, w_row, r_top)):
                o_ref[ck, d, k * nh:(k + 1) * nh, :] = plane


def _gateprep_call(gates, gate_b, seq):
    m = gates.shape[0]
    L = M_CHUNK
    tr = _pick(seq, (1024, 512, 256, 128))
    shape = (m // L, 2, GP_PLANES * C_HEADS, L)
    blocks = _nbytes((tr, GATE_W), F32) + _nbytes((tr // L,) + shape[1:], F32)
    return pl.pallas_call(
        _gateprep_kernel,
        out_shape=jax.ShapeDtypeStruct(shape, F32),
        grid=(m // tr,),
        in_specs=[pl.BlockSpec((tr, GATE_W), lambda i: (i, 0)),
                  pl.BlockSpec((1, GATE_W), lambda i: (0, 0))],
        out_specs=pl.BlockSpec((tr // L,) + shape[1:], lambda i: (i, 0, 0, 0)),
        compiler_params=_params(("parallel",), blocks, 2 * 1024 * 1024),
        name="mlstm_gate_prep",
    )(gates, gate_b)


def _mlstm_kernel(qf_ref, ktf_ref, vf_ref, gpf_ref, qb_ref, ktb_ref, vb_ref, gpb_ref, s0_ref, m0_ref,
                  hf_ref, hb_ref, sf_ref, mf_ref, cst, mst, *, nc):
    c = pl.program_id(1)
    L = M_CHUNK
    hd = HEAD_DIM
    nh = C_HEADS

    @pl.when(c == 0)
    def _():
        cst[...] = s0_ref[...]
        mst[...] = m0_ref[...]

    ii = lax.broadcasted_iota(jnp.int32, (L, L), 0)
    jj = lax.broadcasted_iota(jnp.int32, (L, L), 1)
    ones = jnp.ones((L, hd), BF16)
    dirs = ((qf_ref, ktf_ref, vf_ref, gpf_ref, hf_ref, ii >= jj), (qb_ref, ktb_ref, vb_ref, gpb_ref, hb_ref, ii <= jj))
    for d, (q_ref, kt_ref, v_ref, gp_ref, h_ref, causal) in enumerate(dirs):
        r_row, rmax, bcum, total, w_row, r_top = (gp_ref[0, 0, k * nh:(k + 1) * nh, :] for k in range(GP_PLANES))
        m0 = mst[d]
        big_m = jnp.maximum(rmax, m0)
        si_row = jnp.exp(m0 - big_m)
        nrm_row = jnp.exp(-(bcum + big_m))
        m_loc = total + r_top
        m_new = jnp.maximum(total + m0, m_loc)
        s_prev = jnp.exp(total + m0 - m_new)
        s_loc = jnp.exp(m_loc - m_new)
        mst[d] = m_new
        xt = jnp.concatenate([big_m, si_row, nrm_row, jnp.zeros((L - 3 * nh, L), F32)], axis=0).T
        for h in range(nh):
            cs = slice(h * hd, (h + 1) * hd)
            qs = q_ref[:, cs]
            kt = kt_ref[0, cs, :]
            vext = jnp.concatenate([v_ref[:, cs], ones], axis=1)
            cext = cst[d * nh + h]
            expo = jnp.where(causal, r_row[h:h + 1, :] - xt[:, h:h + 1], NEG)
            sw = jnp.exp(expo) * _dot(qs, kt)
            si = jnp.broadcast_to(xt[:, nh + h:nh + h + 1], (L, 2 * hd))
            both = _dot(sw.astype(BF16), vext) + si * _dot(qs, cext.astype(BF16))
            nrm = jnp.broadcast_to(xt[:, 2 * nh + h:2 * nh + h + 1], (L, hd))
            h_ref[:, cs] = both[:, 0:hd] / jnp.maximum(jnp.abs(both[:, hd:2 * hd]), nrm)
            kwt = (kt.astype(F32) * w_row[h:h + 1, :]).astype(BF16)
            sp = jnp.concatenate([s_prev[h:h + 1, :], s_prev[h:h + 1, :]], axis=1)
            sl = jnp.concatenate([s_loc[h:h + 1, :], s_loc[h:h + 1, :]], axis=1)
            cst[d * nh + h] = sp * cext + sl * _dot(kwt, vext)

    @pl.when(c == nc - 1)
    def _():
        sf_ref[...] = cst[...]
        mf_ref[...] = mst[...]


def _mlstm_call(qs, kt, vb, gplanes, s0, m0, bsz):
    t = qs.shape[0] // bsz
    L = M_CHUNK
    nc = t // L
    nh = C_HEADS
    hw = nh * HEAD_DIM

    def cidx(d, b, c):
        return b * nc + (c if d == 0 else nc - 1 - c)

    def dir_specs(d):
        return [pl.BlockSpec((L, hw), lambda b, c: (cidx(d, b, c), 0)),
                pl.BlockSpec((1, hw, L), lambda b, c: (cidx(d, b, c), 0, 0)),
                pl.BlockSpec((L, hw), lambda b, c: (cidx(d, b, c), 0)),
                pl.BlockSpec((1, 1, GP_PLANES * nh, L), lambda b, c: (cidx(d, b, c), d, 0, 0))]

    sspec = pl.BlockSpec((2 * nh, HEAD_DIM, 2 * HEAD_DIM), lambda b, c: (b, 0, 0))
    mspec = pl.BlockSpec((2, nh, 128), lambda b, c: (b, 0, 0))
    blocks = (6 * _nbytes((L, hw), BF16) + 2 * _nbytes((GP_PLANES * nh, L), F32)
              + 4 * _nbytes((nh, HEAD_DIM, 2 * HEAD_DIM), F32) + 2 * _nbytes((L, hw), F32))
    scratch = 2 * _nbytes((nh, HEAD_DIM, 2 * HEAD_DIM), F32)
    hshape = jax.ShapeDtypeStruct((bsz * t, hw), F32)
    return pl.pallas_call(
        functools.partial(_mlstm_kernel, nc=nc),
        out_shape=[hshape, hshape, jax.ShapeDtypeStruct(s0.shape, F32), jax.ShapeDtypeStruct(m0.shape, F32)],
        grid=(bsz, nc),
        in_specs=dir_specs(0) + dir_specs(1) + [sspec, mspec],
        out_specs=[pl.BlockSpec((L, hw), lambda b, c: (cidx(0, b, c), 0)),
                   pl.BlockSpec((L, hw), lambda b, c: (cidx(1, b, c), 0)), sspec, mspec],
        scratch_shapes=[pltpu.VMEM((2 * nh, HEAD_DIM, 2 * HEAD_DIM), F32), pltpu.VMEM((2, nh, 128), F32)],
        compiler_params=_params(("parallel", "arbitrary"), blocks, scratch + 16 * 1024 * 1024),
        name="mlstm_scan",
    )(qs, kt, vb, gplanes, qs, kt, vb, gplanes, s0, m0)


def _mlstm_out_kernel(hf_ref, hb_ref, o_ref, g_ref, y_ref):
    hd = HEAD_DIM
    for h in range(C_HEADS):
        cs = slice(h * hd, (h + 1) * hd)
        hn = _head_norm(hf_ref[:, cs] + hb_ref[:, cs], g_ref[:, cs])
        y_ref[:, cs] = (hn * jax.nn.sigmoid(o_ref[:, cs])).astype(y_ref.dtype)


def _mlstm_out_call(hf, hb, p_main, norm_g):
    m, hw = hf.shape
    tr = _pick(m, (256, 128))
    blocks = 3 * _nbytes((tr, hw), F32) + _nbytes((tr, hw), BF16)
    hspec = pl.BlockSpec((tr, hw), lambda i: (i, 0))
    return pl.pallas_call(
        _mlstm_out_kernel,
        out_shape=jax.ShapeDtypeStruct((m, hw), BF16),
        grid=(m // tr,),
        in_specs=[hspec, hspec,
                  pl.BlockSpec((tr, hw), lambda i: (i, COL_CO)),
                  pl.BlockSpec((1, hw), lambda i: (0, 0))],
        out_specs=hspec,
        compiler_params=_params(("parallel",), blocks, 2 * 1024 * 1024),
        name="mlstm_out",
    )(hf, hb, p_main, norm_g.reshape(1, hw))


def _rope_lane_tables(n):
    t = jnp.arange(n)
    inv = ROPE_THETA ** (-jnp.arange(ROPE_PAIRS, dtype=F32) / ROPE_PAIRS)
    row = (t // GRID_W).astype(F32)[:, None] * inv
    col = (t % GRID_W).astype(F32)[:, None] * inv
    cos = jnp.concatenate([jnp.cos(row), jnp.cos(row), jnp.cos(col), jnp.cos(col)], axis=1)
    sin = jnp.concatenate([-jnp.sin(row), jnp.sin(row), -jnp.sin(col), jnp.sin(col)], axis=1)
    return cos, sin


def _split_w_in(w_in):
    aq_end = BRANCH_W
    akv_end = aq_end + 2 * A_KV_HEADS * HEAD_DIM
    main = jnp.concatenate([w_in[..., :aq_end], w_in[..., akv_end:MAIN_W], w_in[..., aq_end:akv_end]], axis=-1)
    gates = jnp.pad(w_in[..., MAIN_W:], ((0, 0), (0, 0), (0, GATE_W - 4 * C_HEADS)))
    return main.astype(BF16), gates.astype(BF16)


def kernel(x, c, ctx, c_ctx, w_mod, b_mod, norm_g, ffn_w1, ffn_w3, ffn_w2, w_in, qk_g, attn_sink, na_relpos,
           mlstm_conv_w, mlstm_conv_b, mlstm_gate_b, mlstm_norm_g, w_gate, b_gate, w_branch, w_out):
    bsz, n, d = x.shape
    lc = ctx.shape[1]
    depth = w_mod.shape[0]
    assert n % GRID_W == 0 and n % M_CHUNK == 0 and lc % M_CHUNK == 0 and n % A_BLOCK == 0

    cs = jnp.zeros((8, d), F32).at[:bsz].set(c).at[bsz].set(c_ctx)
    mods = _mod_call(cs, w_mod, b_mod).reshape(depth, 8, N_MOD, d)
    cos_t, sin_t = _rope_lane_tables(n)

    xl = x.reshape(bsz * n, d)
    xc = ctx.reshape(bsz * lc, d)
    nh = C_HEADS
    wg = w_gate.astype(BF16)
    wb = w_branch.astype(BF16)
    w_main, w_gates = _split_w_in(w_in)
    s_zero = jnp.zeros((bsz * 2 * nh, HEAD_DIM, 2 * HEAD_DIM), F32)
    m_zero = jnp.zeros((bsz * 2, nh, 128), F32)

    def ffn(xs, md, k0, g, layer, sub):
        u = _normmod_call(xs, g, md[:, k0], md[:, k0 + 1])
        h = _up_call(u, ffn_w1, ffn_w3, layer, sub)
        return _down_call(h, ffn_w2, xs, md[:, k0 + 2], 0.5, lead=(layer, sub))

    for i in range(depth):
        last = i == depth - 1
        ml = mods[i, :bsz]
        mc = mods[i, bsz:bsz + 1]
        gate_b = jnp.pad(mlstm_gate_b[i].reshape(1, 4 * nh), ((0, 0), (0, GATE_W - 4 * nh)))
        conv_b = mlstm_conv_b[i].reshape(1, -1)
        bias_tab = _na_bias_table(na_relpos[i])

        xc = ffn(xc, mc, 0, norm_g[i, 0], i, 0)
        xl = ffn(xl, ml, 0, norm_g[i, 0], i, 0)

        uc = _normmod_call(xc, norm_g[i, 1], mc[:, 3], mc[:, 4])
        ul = _normmod_call(xl, norm_g[i, 1], ml[:, 3], ml[:, 4])
        pc = _inproj_call(uc, w_main, i)
        pl_ = _inproj_call(ul, w_main, i)
        gc = _matmul_call(uc, w_gates[i])
        gl = _matmul_call(ul, w_gates[i])

        qa_c, qb_c, kb_c, vb_c, ka_c, va_c = _qkprep_call(pc, qk_g[i], cos_t, sin_t, False, lc)
        qa_l, qb_l, kb_l, vb_l, ka_l, va_l = _qkprep_call(pl_, qk_g[i], cos_t, sin_t, True, n)

        a_l = _attn_a_call(qa_l, ka_l, va_l, ka_c, va_c, attn_sink[i], bsz, True)
        n_l = _attn_b_call(qb_l, kb_l, vb_l, kb_c, vb_c, bias_tab, bsz)
        qc_c, kc_c, vc_c = _cprep_call(pc, mlstm_conv_w[i], conv_b, lc)
        qc_l, kc_l, vc_l = _cprep_call(pl_, mlstm_conv_w[i], conv_b, n)
        hc_f, hc_b, s_c, m_c = _mlstm_call(qc_c, kc_c, vc_c, _gateprep_call(gc, gate_b, lc), s_zero, m_zero, bsz)
        hl_f, hl_b, _, _ = _mlstm_call(qc_l, kc_l, vc_l, _gateprep_call(gl, gate_b, n), s_c, m_c, bsz)
        m_l = _mlstm_out_call(hl_f, hl_b, pl_, mlstm_norm_g[i])

        yl = _merge_call(ul, a_l, n_l, m_l, wg, b_gate, wb, i)
        xl = _down_call(yl, w_out, xl, ml[:, 5], 1.0, lead=(i,))
        xl = ffn(xl, ml, 6, norm_g[i, 2], i, 1)
        if not last:
            a_c = _attn_a_call(qa_c, ka_c, va_c, ka_c, va_c, attn_sink[i], bsz, False)
            n_c = _attn_bx_call(qb_c, kb_c, vb_c, bsz)
            mm_c = _mlstm_out_call(hc_f, hc_b, pc, mlstm_norm_g[i])
            yc = _merge_call(uc, a_c, n_c, mm_c, wg, b_gate, wb, i)
            xc = _down_call(yc, w_out, xc, mc[:, 5], 1.0, lead=(i,))
            xc = ffn(xc, mc, 6, norm_g[i, 2], i, 1)
    return xl.reshape(bsz, n, d)
```

```python
import functools

import numpy as np
import jax
import jax.numpy as jnp
from jax import lax
from jax.experimental import pallas as pl
from jax.experimental.pallas import tpu as pltpu

F32 = jnp.float32
BF16 = jnp.bfloat16

HEAD_DIM = 128
GRID_W = 64
ROPE_PAIRS = HEAD_DIM // 4
ROPE_THETA = 10000.0
A_HEADS = 8
A_KV_HEADS = 2
A_GROUP = A_HEADS // A_KV_HEADS
A_WINDOW = 128
A_BLOCK = 128
B_HEADS = 8
NA_ROWS = 8
NA_COLS = 16
C_HEADS = 8
C_CONV = 5
BRANCH_W = 8 * HEAD_DIM
N_MOD = 9
EPS = 1e-6
NEG = -1e30
ATTN_SCALE = HEAD_DIM ** -0.5
LOG2E = 1.4426950408889634
QK_SCALE_LOG2 = ATTN_SCALE * LOG2E

M_CHUNK = 128
NA_TILE_ROWS = 8
NA_TILE = NA_TILE_ROWS * GRID_W
NA_WIN_ROWS = NA_TILE_ROWS + NA_ROWS
NA_HEADS_PER_STEP = 8
A_QBLOCKS = 2

BF16_SUBLANES = 16
V7X_VMEM_BYTES = 64 * 1024 * 1024
VMEM_CAP = V7X_VMEM_BYTES - 6 * 1024 * 1024

COL_AQ, COL_BQ, COL_BK, COL_BV, COL_CQ, COL_CK, COL_CV, COL_CO = range(8)
MAIN_W = 8 * BRANCH_W + 2 * A_KV_HEADS * HEAD_DIM
GATE_W = 128


def _pick(n, cands):
    for c in cands:
        if n % c == 0:
            return c
    raise ValueError(f"no tile in {cands} divides {n}")


def _params(sem, block_bytes, temp_bytes=0):
    limit = 2 * block_bytes + temp_bytes + 4 * 1024 * 1024
    return pltpu.CompilerParams(dimension_semantics=sem,
                                vmem_limit_bytes=int(min(max(limit, 16 * 1024 * 1024), VMEM_CAP)))


def _nbytes(shape, dtype):
    return int(np.prod(shape)) * jnp.dtype(dtype).itemsize


def _dot(a, b):
    return jnp.dot(a, b, preferred_element_type=F32)


def _dot_nt(a, b):
    return lax.dot_general(a, b, (((1,), (1,)), ((), ())), preferred_element_type=F32)


def _dot_tn(a, b):
    return lax.dot_general(a, b, (((0,), (0,)), ((), ())), preferred_element_type=F32)


def _silu(x):
    return x * jax.nn.sigmoid(x)


def _mod_kernel(c_ref, w_ref, b_ref, o_ref):
    a = _silu(c_ref[...]).astype(BF16)
    o_ref[0] = _dot(a, w_ref[0].astype(BF16)) + b_ref[0]


def _mod_call(cs, w_mod, b_mod):
    depth, d, nd = w_mod.shape
    tn = _pick(nd, (1024, 512, 256, 128))
    rows = cs.shape[0]
    blocks = _nbytes((d, tn), F32) + _nbytes((rows, d), F32) + _nbytes((rows, tn), F32)
    return pl.pallas_call(
        _mod_kernel,
        out_shape=jax.ShapeDtypeStruct((depth, rows, nd), F32),
        grid=(depth, nd // tn),
        in_specs=[pl.BlockSpec((rows, d), lambda l, j: (0, 0)),
                  pl.BlockSpec((1, d, tn), lambda l, j: (l, 0, j)),
                  pl.BlockSpec((1, 1, tn), lambda l, j: (l, 0, j))],
        out_specs=pl.BlockSpec((1, rows, tn), lambda l, j: (l, 0, j)),
        compiler_params=_params(("parallel", "parallel"), blocks, _nbytes((d, tn), BF16)),
        name="mod_vectors",
    )(cs, w_mod, b_mod.reshape(depth, 1, nd))


def _normmod_kernel(x_ref, g_ref, shift_ref, scale_ref, o_ref):
    x = x_ref[...]
    ms = jnp.mean(x * x, axis=-1, keepdims=True)
    y = x * lax.rsqrt(ms + EPS) * g_ref[...]
    o_ref[...] = (y * (1.0 + scale_ref[0]) + shift_ref[0]).astype(o_ref.dtype)


def _normmod_call(x, g, shift, scale):
    m, d = x.shape
    groups = shift.shape[0]
    tr = _pick(m // groups, (512, 256, 128, 64, 8))
    per = (m // groups) // tr
    blocks = _nbytes((tr, d), F32) + _nbytes((tr, d), BF16) + 3 * _nbytes((1, d), F32)
    return pl.pallas_call(
        _normmod_kernel,
        out_shape=jax.ShapeDtypeStruct((m, d), BF16),
        grid=(m // tr,),
        in_specs=[pl.BlockSpec((tr, d), lambda i: (i, 0)),
                  pl.BlockSpec((1, d), lambda i: (0, 0)),
                  pl.BlockSpec((1, 1, d), lambda i: (i // per, 0, 0)),
                  pl.BlockSpec((1, 1, d), lambda i: (i // per, 0, 0))],
        out_specs=pl.BlockSpec((tr, d), lambda i: (i, 0)),
        compiler_params=_params(("parallel",), blocks, 2 * _nbytes((tr, d), F32)),
        name="norm_modulate",
    )(x, g.reshape(1, d), shift.reshape(groups, 1, d), scale.reshape(groups, 1, d))


def _prefetch_chunks(tm, nj):
    n = 1
    while 2 * n <= nj and tm % (2 * n) == 0 and (tm // (2 * n)) % BF16_SUBLANES == 0:
        n *= 2
    return n


def _rowblock(a_hbm, abuf, sem, *, ni, nj, tm):
    i = pl.program_id(0)
    j = pl.program_id(1)
    slot = i % 2
    nch = _prefetch_chunks(tm, nj)
    ch = tm // nch

    def chunk_copy(blk, c, slot_):
        return pltpu.make_async_copy(a_hbm.at[pl.ds(blk * tm + c * ch, ch), :],
                                     abuf.at[slot_, pl.ds(c * ch, ch), :], sem.at[slot_])

    @pl.when((i == 0) & (j == 0))
    def _():
        for c in range(nch):
            chunk_copy(0, c, 0).start()

    @pl.when(j == 0)
    def _():
        for c in range(nch):
            chunk_copy(i, c, slot).wait()

    @pl.when((i + 1 < ni) & (j < nch))
    def _():
        chunk_copy(i + 1, j, 1 - slot).start()

    return abuf[slot]


def _rowblock_scratch(tm, k):
    return [pltpu.VMEM((2, tm, k), BF16), pltpu.SemaphoreType.DMA((2,))]


def _up_kernel(u_hbm, w1_ref, w3_ref, o_ref, ubuf, sem, *, ni, nj, tm):
    u = _rowblock(u_hbm, ubuf, sem, ni=ni, nj=nj, tm=tm)
    h1 = _dot(u, w1_ref[...].astype(BF16))
    h3 = _dot(u, w3_ref[...].astype(BF16))
    o_ref[...] = (_silu(h1) * h3).astype(o_ref.dtype)


def _up_call(u, w1, w3, layer, sub):
    m, d = u.shape
    f = w1.shape[-1]
    tm = _pick(m, (2048, 1024, 512, 256))
    tf = _pick(f, (256, 128))
    wspec = pl.BlockSpec((None, None, d, tf), lambda i, j: (layer, sub, 0, j))
    ni, nj = m // tm, f // tf
    blocks = 2 * _nbytes((d, tf), F32) + _nbytes((tm, tf), BF16)
    return pl.pallas_call(
        functools.partial(_up_kernel, ni=ni, nj=nj, tm=tm),
        out_shape=jax.ShapeDtypeStruct((m, f), BF16),
        grid=(ni, nj),
        in_specs=[pl.BlockSpec(memory_space=pl.ANY), wspec, wspec],
        out_specs=pl.BlockSpec((tm, tf), lambda i, j: (i, j)),
        scratch_shapes=_rowblock_scratch(tm, d),
        compiler_params=_params(("arbitrary", "arbitrary"), blocks,
                                2 * _nbytes((tm, d), BF16) + 2 * _nbytes((d, tf), BF16)
                                + 4 * _nbytes((tm, tf), F32)),
        name="ffn_up",
    )(u, w1, w3)


def _down_kernel(a_hbm, w_ref, x_ref, gate_ref, o_ref, abuf, sem, *, coef, ni, nj, tm):
    a = _rowblock(a_hbm, abuf, sem, ni=ni, nj=nj, tm=tm)
    acc = _dot(a, w_ref[...].astype(BF16))
    o_ref[...] = x_ref[...] + (coef * gate_ref[0]) * acc


def _down_call(a, w, x, gate, coef, lead=()):
    m, k = a.shape
    d = w.shape[-1]
    groups = gate.shape[0]
    tm = _pick(m // groups, (1024, 512, 256))
    tn = _pick(d, (512, 256, 128) if k <= 4096 else (256, 128))
    per = (m // groups) // tm
    ni, nj = m // tm, d // tn
    blocks = _nbytes((k, tn), w.dtype) + 2 * _nbytes((tm, tn), F32) + _nbytes((1, tn), F32)
    temps = (2 * _nbytes((tm, k), BF16) + 2 * _nbytes((tm, tn), F32)
             + (_nbytes((k, tn), BF16) if w.dtype != BF16 else 0))
    wspec = pl.BlockSpec((None,) * len(lead) + (k, tn), lambda i, j: tuple(lead) + (0, j))
    return pl.pallas_call(
        functools.partial(_down_kernel, coef=coef, ni=ni, nj=nj, tm=tm),
        out_shape=jax.ShapeDtypeStruct((m, d), F32),
        grid=(ni, nj),
        in_specs=[pl.BlockSpec(memory_space=pl.ANY),
                  wspec,
                  pl.BlockSpec((tm, tn), lambda i, j: (i, j)),
                  pl.BlockSpec((1, 1, tn), lambda i, j: (i // per, 0, j))],
        out_specs=pl.BlockSpec((tm, tn), lambda i, j: (i, j)),
        scratch_shapes=_rowblock_scratch(tm, k),
        compiler_params=_params(("arbitrary", "arbitrary"), blocks, temps),
        name="proj_residual",
    )(a, w, x, gate.reshape(groups, 1, d))


def _matmul_kernel(a_ref, w_ref, o_ref):
    o_ref[...] = _dot(a_ref[...], w_ref[...]).astype(o_ref.dtype)


def _inproj_kernel(u_hbm, w_ref, o_ref, ubuf, sem, *, ni, nj, tm):
    u = _rowblock(u_hbm, ubuf, sem, ni=ni, nj=nj, tm=tm)
    o_ref[...] = _dot(u, w_ref[...])


IN_TN = 2 * A_KV_HEADS * HEAD_DIM


def _inproj_call(u, w_main, layer):
    m, d = u.shape
    tm = _pick(m, (2048, 1024, 512, 256))
    tn = IN_TN
    ni, nj = m // tm, MAIN_W // tn
    blocks = _nbytes((d, tn), BF16) + _nbytes((tm, tn), F32)
    return pl.pallas_call(
        functools.partial(_inproj_kernel, ni=ni, nj=nj, tm=tm),
        out_shape=jax.ShapeDtypeStruct((m, MAIN_W), F32),
        grid=(ni, nj),
        in_specs=[pl.BlockSpec(memory_space=pl.ANY),
                  pl.BlockSpec((None, d, tn), lambda i, j: (layer, 0, j))],
        out_specs=pl.BlockSpec((tm, tn), lambda i, j: (i, j)),
        scratch_shapes=_rowblock_scratch(tm, d),
        compiler_params=_params(("arbitrary", "arbitrary"), blocks,
                                2 * _nbytes((tm, d), BF16) + 2 * _nbytes((tm, tn), F32)),
        name="in_proj",
    )(u, w_main)


def _matmul_call(a, w, out_dtype=F32):
    m, k = a.shape
    n = w.shape[1]
    tm = _pick(m, (1024, 512, 256))
    tn = _pick(n, (512, 256, 128))
    blocks = _nbytes((tm, k), BF16) + _nbytes((k, tn), BF16) + _nbytes((tm, tn), out_dtype)
    return pl.pallas_call(
        _matmul_kernel,
        out_shape=jax.ShapeDtypeStruct((m, n), out_dtype),
        grid=(m // tm, n // tn),
        in_specs=[pl.BlockSpec((tm, k), lambda i, j: (i, 0)),
                  pl.BlockSpec((k, tn), lambda i, j: (0, j))],
        out_specs=pl.BlockSpec((tm, tn), lambda i, j: (i, j)),
        compiler_params=_params(("parallel", "parallel"), blocks, _nbytes((tm, tn), F32)),
        name="in_proj",
    )(a, w)


def _merge_kernel(u_ref, oa_ref, ob_ref, oc_ref, wg_ref, bg_ref, wb_ref, y_ref):
    u = u_ref[...]
    acc = None
    for j, br_ref in enumerate((oa_ref, ob_ref, oc_ref)):
        gate = jax.nn.sigmoid(_dot(u, wg_ref[j]) + bg_ref[j])
        term = gate * _dot(br_ref[...], wb_ref[j])
        acc = term if acc is None else acc + term
    y_ref[...] = acc.astype(y_ref.dtype)


def _merge_call(u, oa, ob, oc, wg, bg, wb, layer):
    m, d = u.shape
    bw = oa.shape[1]
    tm = _pick(m, (1024, 512, 256))
    tn = _pick(d, (256, 128))
    blocks = (_nbytes((tm, d), BF16) + 3 * _nbytes((tm, bw), BF16) + 3 * _nbytes((d, tn), BF16)
              + 3 * _nbytes((bw, tn), BF16) + _nbytes((tm, tn), BF16))
    return pl.pallas_call(
        _merge_kernel,
        out_shape=jax.ShapeDtypeStruct((m, d), BF16),
        grid=(m // tm, d // tn),
        in_specs=[pl.BlockSpec((tm, d), lambda i, j: (i, 0)),
                  pl.BlockSpec((tm, bw), lambda i, j: (i, 0)),
                  pl.BlockSpec((tm, bw), lambda i, j: (i, 0)),
                  pl.BlockSpec((tm, bw), lambda i, j: (i, 0)),
                  pl.BlockSpec((None, 3, d, tn), lambda i, j: (layer, 0, 0, j)),
                  pl.BlockSpec((None, 3, 1, tn), lambda i, j: (layer, 0, 0, j)),
                  pl.BlockSpec((None, 3, bw, tn), lambda i, j: (layer, 0, 0, j))],
        out_specs=pl.BlockSpec((tm, tn), lambda i, j: (i, j)),
        compiler_params=_params(("parallel", "parallel"), blocks, 10 * _nbytes((tm, tn), F32)),
        name="gated_merge",
    )(u, oa, ob, oc, wg, bg.reshape(bg.shape[0], 3, 1, d), wb)


def _head_norm(x, g):
    ms = jnp.mean(x * x, axis=-1, keepdims=True)
    return x * lax.rsqrt(ms + EPS) * g


def _rope(y, cos, sin_signed):
    lane = lax.broadcasted_iota(jnp.int32, y.shape, 1)
    partner = jnp.where((lane % 64) < ROPE_PAIRS, pltpu.roll(y, HEAD_DIM - ROPE_PAIRS, axis=1),
                        pltpu.roll(y, ROPE_PAIRS, axis=1))
    return y * cos + partner * sin_signed


def _qkprep_kernel(p4_ref, pa_ref, g_ref, cos_ref, sin_ref,
                   qa_ref, qb_ref, kb_ref, vb_ref, ka_ref, va_ref, *, rope):
    hd = HEAD_DIM
    if rope:
        cos = cos_ref[...]
        sin = sin_ref[...]
    for h in range(A_HEADS):
        y = _head_norm(p4_ref[:, h * hd:(h + 1) * hd], g_ref[0:1, :])
        if rope:
            y = _rope(y, cos, sin)
        qa_ref[:, h * hd:(h + 1) * hd] = y.astype(BF16)
    for h in range(B_HEADS):
        c0 = BRANCH_W + h * hd
        qb_ref[:, h * hd:(h + 1) * hd] = _head_norm(p4_ref[:, c0:c0 + hd], g_ref[2:3, :]).astype(BF16)
        c0 = 2 * BRANCH_W + h * hd
        kb_ref[:, h * hd:(h + 1) * hd] = _head_norm(p4_ref[:, c0:c0 + hd], g_ref[3:4, :]).astype(BF16)
    vb_ref[...] = p4_ref[:, 3 * BRANCH_W:4 * BRANCH_W].astype(BF16)
    for h in range(A_KV_HEADS):
        y = _head_norm(pa_ref[:, h * hd:(h + 1) * hd], g_ref[1:2, :])
        if rope:
            y = _rope(y, cos, sin)
        ka_ref[:, h * hd:(h + 1) * hd] = y.astype(BF16)
    kvw = A_KV_HEADS * hd
    va_ref[...] = pa_ref[:, kvw:2 * kvw].astype(BF16)


def _qkprep_call(p_main, qk_g, cos_t, sin_t, rope, seq):
    m = p_main.shape[0]
    tr = _pick(seq, (512, 256, 128))
    per = seq // tr
    kvw = A_KV_HEADS * HEAD_DIM
    w4 = 4 * BRANCH_W
    blocks = (_nbytes((tr, w4), F32) + _nbytes((tr, 2 * kvw), F32) + 2 * _nbytes((tr, HEAD_DIM), F32)
              + _nbytes((tr, w4), BF16) + _nbytes((tr, 2 * kvw), BF16))
    outs = [jax.ShapeDtypeStruct((m, BRANCH_W), BF16)] * 4 + [jax.ShapeDtypeStruct((m, kvw), BF16)] * 2
    return pl.pallas_call(
        functools.partial(_qkprep_kernel, rope=rope),
        out_shape=outs,
        grid=(m // tr,),
        in_specs=[pl.BlockSpec((tr, w4), lambda i: (i, 0)),
                  pl.BlockSpec((tr, 2 * kvw), lambda i: (i, (8 * BRANCH_W) // (2 * kvw))),
                  pl.BlockSpec((4, HEAD_DIM), lambda i: (0, 0)),
                  pl.BlockSpec((tr, HEAD_DIM), lambda i: (i % per, 0)),
                  pl.BlockSpec((tr, HEAD_DIM), lambda i: (i % per, 0))],
        out_specs=[pl.BlockSpec((tr, BRANCH_W), lambda i: (i, 0))] * 4
                  + [pl.BlockSpec((tr, kvw), lambda i: (i, 0))] * 2,
        compiler_params=_params(("parallel",), blocks, 4 * _nbytes((tr, HEAD_DIM), F32)),
        name="qk_prep",
    )(p_main, p_main, qk_g, cos_t, sin_t)


def _attn_a_kernel(sink_ref, q_ref, *rest, lc, window):
    nwin = A_QBLOCKS + 2 if window else 0
    nmask = A_QBLOCKS if window else 0
    k_refs = rest[:nwin]
    v_refs = rest[nwin:2 * nwin]
    mask_refs = rest[2 * nwin:2 * nwin + nmask]
    kx_ref, vx_ref, o_ref = rest[2 * nwin + nmask:]
    hd = HEAD_DIM
    blk = A_BLOCK
    gw = A_GROUP * hd
    keys = lc + 3 * blk if window else lc
    ones = jnp.ones((keys, hd), BF16)
    for sb in range(A_QBLOCKS):
        rows = slice(sb * blk, (sb + 1) * blk)
        for hk in range(A_KV_HEADS):
            cs = slice(hk * hd, (hk + 1) * hd)
            if window:
                kall = jnp.concatenate([kx_ref[:, cs]] + [r[:, cs] for r in k_refs[sb:sb + 3]], axis=0)
                vall = jnp.concatenate([vx_ref[:, cs]] + [r[:, cs] for r in v_refs[sb:sb + 3]], axis=0)
            else:
                kall = kx_ref[:, cs]
                vall = vx_ref[:, cs]
            q4 = jnp.concatenate([q_ref[rows, hk * gw + g * hd:hk * gw + (g + 1) * hd] for g in range(A_GROUP)],
                                 axis=0)
            s = _dot_nt(q4, kall) * QK_SCALE_LOG2
            if window:
                s = s + mask_refs[sb][0]
            snk = jnp.concatenate(
                [jnp.full((blk, 1), sink_ref[hk * A_GROUP + g] * LOG2E, F32) for g in range(A_GROUP)], axis=0)
            mx = jnp.maximum(jnp.max(s, axis=-1, keepdims=True), snk)
            e = jnp.exp2(s - mx).astype(BF16)
            both = _dot(e, jnp.concatenate([vall, ones], axis=1))
            o = both[:, 0:hd] * (1.0 / (both[:, hd:2 * hd] + jnp.exp2(snk - mx)))
            for g in range(A_GROUP):
                o_ref[rows, hk * gw + g * hd:hk * gw + (g + 1) * hd] = (
                    o[g * blk:(g + 1) * blk].astype(o_ref.dtype))


def _gqa_mask_table(lc, nb):
    blk = A_BLOCK
    r = np.arange(blk)[:, None]
    kc = np.arange(3 * blk)[None, :]
    band = np.abs(r + blk - kc) <= A_WINDOW
    tabs = []
    for first, last in ((True, False), (False, False), (False, True)):
        ok = band & ~(first & (kc < blk)) & ~(last & (kc >= 2 * blk))
        win = np.where(ok, 0.0, NEG).astype(np.float32)
        tabs.append(np.tile(np.concatenate([np.zeros((blk, lc), np.float32), win], axis=1), (A_GROUP, 1)))
    assert nb >= 2
    return np.stack(tabs)


def _attn_a_call(qa, ka, va, kx, vx, sink, bsz, window):
    n = qa.shape[0] // bsz
    lc = kx.shape[0] // bsz
    blk = A_BLOCK
    nb = n // blk
    qb = A_QBLOCKS
    assert nb % qb == 0
    nt = nb // qb
    kvw = A_KV_HEADS * HEAD_DIM
    qw = A_HEADS * HEAD_DIM
    qspec = pl.BlockSpec((qb * blk, qw), lambda b, t: (b * nt + t, 0))
    xspec = pl.BlockSpec((lc, kvw), lambda b, t: (b, 0))
    sspec = pl.BlockSpec(memory_space=pltpu.SMEM)
    if window:
        def kspec(off):
            return pl.BlockSpec((blk, kvw), lambda b, t: (b * nb + jnp.clip(t * qb + off, 0, nb - 1), 0))

        def mspec(sb):
            def variant(b, t):
                i = t * qb + sb
                return (jnp.where(i == 0, 0, jnp.where(i == nb - 1, 2, 1)), 0, 0)
            return pl.BlockSpec((1, A_GROUP * blk, lc + 3 * blk), variant)

        kspecs = [kspec(off) for off in range(-1, qb + 1)]
        mspecs = [mspec(sb) for sb in range(qb)]
        mask = jnp.asarray(_gqa_mask_table(lc, nb))
        in_specs = [sspec, qspec] + kspecs + kspecs + mspecs + [xspec, xspec]
        args = (sink, qa) + (ka,) * len(kspecs) + (va,) * len(kspecs) + (mask,) * qb + (kx, vx)
        keys = lc + 3 * blk
    else:
        in_specs = [sspec, qspec, xspec, xspec]
        args = (sink, qa, kx, vx)
        keys = lc
    blocks = (2 * _nbytes((qb * blk, qw), BF16) + 2 * _nbytes((keys + qb * blk, kvw), BF16)
              + (qb * _nbytes((A_GROUP * blk, keys), F32) if window else 0))
    return pl.pallas_call(
        functools.partial(_attn_a_kernel, lc=lc, window=window),
        out_shape=jax.ShapeDtypeStruct(qa.shape, BF16),
        grid=(bsz, nt),
        in_specs=in_specs,
        out_specs=qspec,
        compiler_params=_params(("parallel", "parallel"), blocks,
                                10 * qb * _nbytes((A_GROUP * blk, keys), F32)),
        name="windowed_gqa" if window else "context_gqa",
    )(*args)


def _softmax2_pv(s1, v1, s2, v2):
    hd = v1.shape[1]
    mx = jnp.maximum(jnp.max(s1, axis=-1, keepdims=True), jnp.max(s2, axis=-1, keepdims=True))
    e1 = jnp.exp2(s1 - mx).astype(BF16)
    e2 = jnp.exp2(s2 - mx).astype(BF16)
    v1e = jnp.concatenate([v1, jnp.ones(v1.shape, BF16)], axis=1)
    v2e = jnp.concatenate([v2, jnp.ones(v2.shape, BF16)], axis=1)
    both = _dot(e1, v1e) + _dot(e2, v2e)
    return both[:, 0:hd] * (1.0 / both[:, hd:2 * hd])


def _attn_b_kernel(q_ref, k0_ref, k1_ref, k2_ref, k3_ref, v0_ref, v1_ref, v2_ref, v3_ref, kx_ref, vx_ref,
                   bias_ref, o_ref):
    hd = HEAD_DIM
    for h in range(NA_HEADS_PER_STEP):
        cs = slice(h * hd, (h + 1) * hd)
        kwin = jnp.concatenate([k0_ref[:, cs], k1_ref[:, cs], k2_ref[:, cs], k3_ref[:, cs]], axis=0)
        vwin = jnp.concatenate([v0_ref[:, cs], v1_ref[:, cs], v2_ref[:, cs], v3_ref[:, cs]], axis=0)
        q = q_ref[:, cs]
        s_nb = _dot_nt(q, kwin) * QK_SCALE_LOG2 + bias_ref[h, 0]
        s_cx = _dot_nt(q, kx_ref[:, cs]) * QK_SCALE_LOG2
        o_ref[:, cs] = _softmax2_pv(s_nb, vwin, s_cx, vx_ref[:, cs]).astype(o_ref.dtype)


def _attn_bx_kernel(q_ref, kx_ref, vx_ref, o_ref):
    hd = HEAD_DIM
    for h in range(B_HEADS):
        cs = slice(h * hd, (h + 1) * hd)
        s = _dot_nt(q_ref[:, cs], kx_ref[:, cs]) * QK_SCALE_LOG2
        mx = jnp.max(s, axis=-1, keepdims=True)
        e = jnp.exp2(s - mx)
        inv = 1.0 / jnp.sum(e, axis=-1, keepdims=True)
        o_ref[:, cs] = (_dot(e.astype(BF16), vx_ref[:, cs]) * inv).astype(o_ref.dtype)


def _na_bias_kernel(rel_ref, o_ref):
    h = pl.program_id(0)
    pos = pl.program_id(1)
    nrel_r = 2 * NA_ROWS - 1
    nrel_c = 2 * NA_COLS - 1
    qi = lax.broadcasted_iota(jnp.int32, (GRID_W, GRID_W), 0)
    ki = lax.broadcasted_iota(jnp.int32, (GRID_W, GRID_W), 1)
    dc = jnp.clip(ki - qi, -(NA_COLS - 1), NA_COLS - 1) + NA_COLS - 1
    start = jnp.clip(qi - NA_COLS // 2, 0, GRID_W - NA_COLS)
    col_in = (ki >= start) & (ki < start + NA_COLS)
    planes = []
    for dr in range(nrel_r):
        acc = jnp.zeros((GRID_W, GRID_W), F32)
        for dcv in range(nrel_c):
            acc = jnp.where(dc == dcv, rel_ref[(h * nrel_r + dr) * nrel_c + dcv], acc)
        planes.append(jnp.where(col_in, acc * LOG2E, NEG))
    masked = jnp.full((GRID_W, GRID_W), NEG, F32)
    half = NA_ROWS // 2
    first_key = {0: lambda j: max(j, half), 1: lambda j: j, 2: lambda j: min(j, half)}
    for p, lo_of in first_key.items():
        @pl.when(pos == p)
        def _(lo_of=lo_of):
            for j in range(NA_TILE_ROWS):
                lo = lo_of(j)
                for jj in range(NA_WIN_ROWS):
                    blk = planes[jj - j + half - 1] if lo <= jj < lo + NA_ROWS else masked
                    o_ref[0, 0, j * GRID_W:(j + 1) * GRID_W, jj * GRID_W:(jj + 1) * GRID_W] = blk


def _na_bias_table(relpos):
    nh = relpos.shape[0]
    shape = (nh, 3, NA_TILE, NA_WIN_ROWS * GRID_W)
    return pl.pallas_call(
        _na_bias_kernel,
        out_shape=jax.ShapeDtypeStruct(shape, F32),
        grid=(nh, 3),
        in_specs=[pl.BlockSpec(memory_space=pltpu.SMEM)],
        out_specs=pl.BlockSpec((1, 1) + shape[2:], lambda h, p: (h, p, 0, 0)),
        compiler_params=_params(("parallel", "parallel"), _nbytes(shape[2:], F32)),
        name="na_bias_table",
    )(relpos.reshape(-1))


def _attn_b_call(qb, kb, vb, kx, vx, bias, bsz):
    n = qb.shape[0] // bsz
    lc = kx.shape[0] // bsz
    rows = n // GRID_W
    assert rows % NA_TILE_ROWS == 0 and rows >= 2 * NA_TILE_ROWS
    nt = rows // NA_TILE_ROWS
    hp = NA_HEADS_PER_STEP
    hd = hp * HEAD_DIM
    sub = NA_TILE // 2
    qspec = pl.BlockSpec((NA_TILE, hd), lambda b, h, t: (b * nt + t, h))

    def kspec(s):
        return pl.BlockSpec((sub, hd),
                            lambda b, h, t: (jnp.clip(2 * t + s, 0, 2 * nt - 1) + 2 * b * nt, h))

    kspecs = [kspec(s) for s in (-1, 0, 1, 2)]
    xspec = pl.BlockSpec((lc, hd), lambda b, h, t: (b, h))
    bspec = pl.BlockSpec((hp, 1) + bias.shape[2:],
                         lambda b, h, t: (h, jnp.where(t == 0, 0, jnp.where(t == nt - 1, 2, 1)), 0, 0))
    blocks = (2 * _nbytes((NA_TILE, hd), BF16) + 8 * _nbytes((sub, hd), BF16) + 2 * _nbytes((lc, hd), BF16)
              + hp * _nbytes(bias.shape[2:], F32))
    return pl.pallas_call(
        _attn_b_kernel,
        out_shape=jax.ShapeDtypeStruct(qb.shape, BF16),
        grid=(bsz, B_HEADS // hp, nt),
        in_specs=[qspec] + kspecs + kspecs + [xspec, xspec, bspec],
        out_specs=qspec,
        compiler_params=_params(("parallel", "parallel", "parallel"), blocks,
                                5 * hp * _nbytes((NA_TILE, NA_WIN_ROWS * GRID_W + lc), F32)),
        name="neighbourhood_attn",
    )(qb, kb, kb, kb, kb, vb, vb, vb, vb, kx, vx, bias)


def _attn_bx_call(qx, kx, vx, bsz):
    lc = qx.shape[0] // bsz
    w = qx.shape[1]
    spec = pl.BlockSpec((lc, w), lambda b: (b, 0))
    return pl.pallas_call(
        _attn_bx_kernel,
        out_shape=jax.ShapeDtypeStruct(qx.shape, BF16),
        grid=(bsz,),
        in_specs=[spec, spec, spec],
        out_specs=spec,
        compiler_params=_params(("parallel",), 4 * _nbytes((lc, w), BF16), 4 * 1024 * 1024),
        name="context_full_attn",
    )(qx, kx, vx)


def _log_sigmoid(x):
    return -(jnp.maximum(-x, 0.0) + jnp.log1p(jnp.exp(-jnp.abs(x))))


CONV_HALO = 8


def _cprep_kernel(q_ref, k_ref, v_ref, qlo_ref, qhi_ref, klo_ref, khi_ref, cw_ref, cb_ref,
                  qs_ref, kt_ref, vb_ref, ext, *, per):
    pos = pl.program_id(0) % per
    tr = q_ref.shape[0]
    hw = q_ref.shape[1]
    halo = CONV_HALO
    pad = C_CONV // 2

    def conv_silu(x_ref, lo_ref, hi_ref, col0):
        ext[0:halo, :] = jnp.where(pos > 0, lo_ref[...], 0.0)
        ext[halo:halo + tr, :] = x_ref[...]
        ext[halo + tr:2 * halo + tr, :] = jnp.where(pos < per - 1, hi_ref[...], 0.0)
        out = cb_ref[:, col0:col0 + hw]
        for j in range(C_CONV):
            out = out + cw_ref[j:j + 1, col0:col0 + hw] * ext[halo - pad + j:halo - pad + j + tr, :]
        return _silu(out)

    qs_ref[...] = (conv_silu(q_ref, qlo_ref, qhi_ref, 0) * ATTN_SCALE).astype(BF16)
    kc = conv_silu(k_ref, klo_ref, khi_ref, hw)
    for ck in range(tr // M_CHUNK):
        for h in range(C_HEADS):
            blk = kc[ck * M_CHUNK:(ck + 1) * M_CHUNK, h * HEAD_DIM:(h + 1) * HEAD_DIM]
            kt_ref[ck, h * HEAD_DIM:(h + 1) * HEAD_DIM, :] = blk.T.astype(BF16)
    vb_ref[...] = v_ref[...].astype(BF16)


def _cprep_call(p_main, conv_w, conv_b, seq):
    m = p_main.shape[0]
    hw = C_HEADS * HEAD_DIM
    assert M_CHUNK == HEAD_DIM
    tr = _pick(seq, (512, 256, 128))
    per = seq // tr
    halo = CONV_HALO
    hb = tr // halo
    nblk = m // halo
    cpb = tr // M_CHUNK

    def main(col):
        return pl.BlockSpec((tr, hw), lambda i: (i, col))

    def lo(col):
        return pl.BlockSpec((halo, hw), lambda i: (jnp.maximum(i * hb - 1, 0), col))

    def hi(col):
        return pl.BlockSpec((halo, hw), lambda i: (jnp.minimum((i + 1) * hb, nblk - 1), col))

    blocks = (3 * _nbytes((tr, hw), F32) + 4 * _nbytes((halo, hw), F32) + _nbytes((C_CONV + 1, 2 * hw), F32)
              + 3 * _nbytes((tr, hw), BF16))
    scratch = _nbytes((tr + 2 * halo, hw), F32)
    out = jax.ShapeDtypeStruct((m, hw), BF16)
    out_t = jax.ShapeDtypeStruct((m // M_CHUNK, hw, M_CHUNK), BF16)
    ospec = pl.BlockSpec((tr, hw), lambda i: (i, 0))
    tspec = pl.BlockSpec((cpb, hw, M_CHUNK), lambda i: (i, 0, 0))
    return pl.pallas_call(
        functools.partial(_cprep_kernel, per=per),
        out_shape=[out, out_t, out],
        grid=(m // tr,),
        in_specs=[main(COL_CQ), main(COL_CK), main(COL_CV), lo(COL_CQ), hi(COL_CQ), lo(COL_CK), hi(COL_CK),
                  pl.BlockSpec((C_CONV, 2 * hw), lambda i: (0, 0)),
                  pl.BlockSpec((1, 2 * hw), lambda i: (0, 0))],
        out_specs=[ospec, tspec, ospec],
        scratch_shapes=[pltpu.VMEM((tr + 2 * halo, hw), F32)],
        compiler_params=_params(("parallel",), blocks, scratch + 4 * _nbytes((tr, hw), F32)),
        name="mlstm_prep",
    )(p_main, p_main, p_main, p_main, p_main, p_main, p_main, conv_w, conv_b)


GP_PLANES = 6


def _gateprep_kernel(g_ref, gb_ref, o_ref):
    nh = C_HEADS
    L = M_CHUNK
    lane = lax.broadcasted_iota(jnp.int32, (nh, L), 1)
    for ck in range(g_ref.shape[0] // L):
        gt = (g_ref[ck * L:(ck + 1) * L, :] + gb_ref[...]).T
        for d in range(2):
            ig = gt[2 * d * nh:(2 * d + 1) * nh]
            logf = _log_sigmoid(gt[(2 * d + 1) * nh:(2 * d + 2) * nh])
            pre = logf
            sft = 1
            while sft < L:
                pre = pre + jnp.where(lane >= sft, pltpu.roll(pre, sft, axis=1), 0.0)
                sft *= 2
            total = jnp.broadcast_to(pre[:, L - 1:L], (nh, L))
            bcum = pre if d == 0 else total - pre + logf
            r_row = ig - bcum
            rmax = r_row
            sft = 1
            while sft < L:
                if d == 0:
                    rmax = jnp.maximum(rmax, jnp.where(lane >= sft, pltpu.roll(rmax, sft, axis=1), NEG))
                else:
                    rmax = jnp.maximum(rmax, jnp.where(lane < L - sft, pltpu.roll(rmax, L - sft, axis=1), NEG))
                sft *= 2
            r_top = jnp.broadcast_to(rmax[:, L - 1:L] if d == 0 else rmax[:, 0:1], (nh, L))
            w_row = jnp.exp(r_row - r_top)
            for k, plane in enumerate((r_row, rmax, bcum, total, w_row, r_top)):
                o_ref[ck, d, k * nh:(k + 1) * nh, :] = plane


def _gateprep_call(gates, gate_b, seq):
    m = gates.shape[0]
    L = M_CHUNK
    tr = _pick(seq, (1024, 512, 256, 128))
    shape = (m // L, 2, GP_PLANES * C_HEADS, L)
    blocks = _nbytes((tr, GATE_W), F32) + _nbytes((tr // L,) + shape[1:], F32)
    return pl.pallas_call(
        _gateprep_kernel,
        out_shape=jax.ShapeDtypeStruct(shape, F32),
        grid=(m // tr,),
        in_specs=[pl.BlockSpec((tr, GATE_W), lambda i: (i, 0)),
                  pl.BlockSpec((1, GATE_W), lambda i: (0, 0))],
        out_specs=pl.BlockSpec((tr // L,) + shape[1:], lambda i: (i, 0, 0, 0)),
        compiler_params=_params(("parallel",), blocks, 2 * 1024 * 1024),
        name="mlstm_gate_prep",
    )(gates, gate_b)


def _mlstm_kernel(qf_ref, ktf_ref, vf_ref, gpf_ref, qb_ref, ktb_ref, vb_ref, gpb_ref, s0_ref, m0_ref,
                  hf_ref, hb_ref, sf_ref, mf_ref, cst, mst, *, nc):
    c = pl.program_id(1)
    L = M_CHUNK
    hd = HEAD_DIM
    nh = C_HEADS

    @pl.when(c == 0)
    def _():
        cst[...] = s0_ref[...]
        mst[...] = m0_ref[...]

    ii = lax.broadcasted_iota(jnp.int32, (L, L), 0)
    jj = lax.broadcasted_iota(jnp.int32, (L, L), 1)
    ones = jnp.ones((L, hd), BF16)
    dirs = ((qf_ref, ktf_ref, vf_ref, gpf_ref, hf_ref, ii >= jj), (qb_ref, ktb_ref, vb_ref, gpb_ref, hb_ref, ii <= jj))
    for d, (q_ref, kt_ref, v_ref, gp_ref, h_ref, causal) in enumerate(dirs):
        r_row, rmax, bcum, total, w_row, r_top = (gp_ref[0, 0, k * nh:(k + 1) * nh, :] for k in range(GP_PLANES))
        m0 = mst[d]
        big_m = jnp.maximum(rmax, m0)
        si_row = jnp.exp(m0 - big_m)
        nrm_row = jnp.exp(-(bcum + big_m))
        m_loc = total + r_top
        m_new = jnp.maximum(total + m0, m_loc)
        s_prev = jnp.exp(total + m0 - m_new)
        s_loc = jnp.exp(m_loc - m_new)
        mst[d] = m_new
        xt = jnp.concatenate([big_m, si_row, nrm_row, jnp.zeros((L - 3 * nh, L), F32)], axis=0).T
        for h in range(nh):
            cs = slice(h * hd, (h + 1) * hd)
            qs = q_ref[:, cs]
            kt = kt_ref[0, cs, :]
            vext = jnp.concatenate([v_ref[:, cs], ones], axis=1)
            cext = cst[d * nh + h]
            expo = jnp.where(causal, r_row[h:h + 1, :] - xt[:, h:h + 1], NEG)
            sw = jnp.exp(expo) * _dot(qs, kt)
            si = jnp.broadcast_to(xt[:, nh + h:nh + h + 1], (L, 2 * hd))
            both = _dot(sw.astype(BF16), vext) + si * _dot(qs, cext.astype(BF16))
            nrm = jnp.broadcast_to(xt[:, 2 * nh + h:2 * nh + h + 1], (L, hd))
            h_ref[:, cs] = both[:, 0:hd] / jnp.maximum(jnp.abs(both[:, hd:2 * hd]), nrm)
            kwt = (kt.astype(F32) * w_row[h:h + 1, :]).astype(BF16)
            sp = jnp.concatenate([s_prev[h:h + 1, :], s_prev[h:h + 1, :]], axis=1)
            sl = jnp.concatenate([s_loc[h:h + 1, :], s_loc[h:h + 1, :]], axis=1)
            cst[d * nh + h] = sp * cext + sl * _dot(kwt, vext)

    @pl.when(c == nc - 1)
    def _():
        sf_ref[...] = cst[...]
        mf_ref[...] = mst[...]


def _mlstm_call(qs, kt, vb, gplanes, s0, m0, bsz):
    t = qs.shape[0] // bsz
    L = M_CHUNK
    nc = t // L
    nh = C_HEADS
    hw = nh * HEAD_DIM

    def cidx(d, b, c):
        return b * nc + (c if d == 0 else nc - 1 - c)

    def dir_specs(d):
        return [pl.BlockSpec((L, hw), lambda b, c: (cidx(d, b, c), 0)),
                pl.BlockSpec((1, hw, L), lambda b, c: (cidx(d, b, c), 0, 0)),
                pl.BlockSpec((L, hw), lambda b, c: (cidx(d, b, c), 0)),
                pl.BlockSpec((1, 1, GP_PLANES * nh, L), lambda b, c: (cidx(d, b, c), d, 0, 0))]

    sspec = pl.BlockSpec((2 * nh, HEAD_DIM, 2 * HEAD_DIM), lambda b, c: (b, 0, 0))
    mspec = pl.BlockSpec((2, nh, 128), lambda b, c: (b, 0, 0))
    blocks = (6 * _nbytes((L, hw), BF16) + 2 * _nbytes((GP_PLANES * nh, L), F32)
              + 4 * _nbytes((nh, HEAD_DIM, 2 * HEAD_DIM), F32) + 2 * _nbytes((L, hw), F32))
    scratch = 2 * _nbytes((nh, HEAD_DIM, 2 * HEAD_DIM), F32)
    hshape = jax.ShapeDtypeStruct((bsz * t, hw), F32)
    return pl.pallas_call(
        functools.partial(_mlstm_kernel, nc=nc),
        out_shape=[hshape, hshape, jax.ShapeDtypeStruct(s0.shape, F32), jax.ShapeDtypeStruct(m0.shape, F32)],
        grid=(bsz, nc),
        in_specs=dir_specs(0) + dir_specs(1) + [sspec, mspec],
        out_specs=[pl.BlockSpec((L, hw), lambda b, c: (cidx(0, b, c), 0)),
                   pl.BlockSpec((L, hw), lambda b, c: (cidx(1, b, c), 0)), sspec, mspec],
        scratch_shapes=[pltpu.VMEM((2 * nh, HEAD_DIM, 2 * HEAD_DIM), F32), pltpu.VMEM((2, nh, 128), F32)],
        compiler_params=_params(("parallel", "arbitrary"), blocks, scratch + 16 * 1024 * 1024),
        name="mlstm_scan",
    )(qs, kt, vb, gplanes, qs, kt, vb, gplanes, s0, m0)


def _mlstm_out_kernel(hf_ref, hb_ref, o_ref, g_ref, y_ref):
    hd = HEAD_DIM
    for h in range(C_HEADS):
        cs = slice(h * hd, (h + 1) * hd)
        hn = _head_norm(hf_ref[:, cs] + hb_ref[:, cs], g_ref[:, cs])
        y_ref[:, cs] = (hn * jax.nn.sigmoid(o_ref[:, cs])).astype(y_ref.dtype)


def _mlstm_out_call(hf, hb, p_main, norm_g):
    m, hw = hf.shape
    tr = _pick(m, (256, 128))
    blocks = 3 * _nbytes((tr, hw), F32) + _nbytes((tr, hw), BF16)
    hspec = pl.BlockSpec((tr, hw), lambda i: (i, 0))
    return pl.pallas_call(
        _mlstm_out_kernel,
        out_shape=jax.ShapeDtypeStruct((m, hw), BF16),
        grid=(m // tr,),
        in_specs=[hspec, hspec,
                  pl.BlockSpec((tr, hw), lambda i: (i, COL_CO)),
                  pl.BlockSpec((1, hw), lambda i: (0, 0))],
        out_specs=hspec,
        compiler_params=_params(("parallel",), blocks, 2 * 1024 * 1024),
        name="mlstm_out",
    )(hf, hb, p_main, norm_g.reshape(1, hw))


def _rope_lane_tables(n):
    t = jnp.arange(n)
    inv = ROPE_THETA ** (-jnp.arange(ROPE_PAIRS, dtype=F32) / ROPE_PAIRS)
    row = (t // GRID_W).astype(F32)[:, None] * inv
    col = (t % GRID_W).astype(F32)[:, None] * inv
    cos = jnp.concatenate([jnp.cos(row), jnp.cos(row), jnp.cos(col), jnp.cos(col)], axis=1)
    sin = jnp.concatenate([-jnp.sin(row), jnp.sin(row), -jnp.sin(col), jnp.sin(col)], axis=1)
    return cos, sin


def _split_w_in(w_in):
    aq_end = BRANCH_W
    akv_end = aq_end + 2 * A_KV_HEADS * HEAD_DIM
    main = jnp.concatenate([w_in[..., :aq_end], w_in[..., akv_end:MAIN_W], w_in[..., aq_end:akv_end]], axis=-1)
    gates = jnp.pad(w_in[..., MAIN_W:], ((0, 0), (0, 0), (0, GATE_W - 4 * C_HEADS)))
    return main.astype(BF16), gates.astype(BF16)


def kernel(x, c, ctx, c_ctx, w_mod, b_mod, norm_g, ffn_w1, ffn_w3, ffn_w2, w_in, qk_g, attn_sink, na_relpos,
           mlstm_conv_w, mlstm_conv_b, mlstm_gate_b, mlstm_norm_g, w_gate, b_gate, w_branch, w_out):
    bsz, n, d = x.shape
    lc = ctx.shape[1]
    depth = w_mod.shape[0]
    assert n % GRID_W == 0 and n % M_CHUNK == 0 and lc % M_CHUNK == 0 and n % A_BLOCK == 0

    cs = jnp.zeros((8, d), F32).at[:bsz].set(c).at[bsz].set(c_ctx)
    mods = _mod_call(cs, w_mod, b_mod).reshape(depth, 8, N_MOD, d)
    cos_t, sin_t = _rope_lane_tables(n)

    xl = x.reshape(bsz * n, d)
    xc = ctx.reshape(bsz * lc, d)
    nh = C_HEADS
    wg = w_gate.astype(BF16)
    wb = w_branch.astype(BF16)
    w_main, w_gates = _split_w_in(w_in)
    s_zero = jnp.zeros((bsz * 2 * nh, HEAD_DIM, 2 * HEAD_DIM), F32)
    m_zero = jnp.zeros((bsz * 2, nh, 128), F32)

    def ffn(xs, md, k0, g, layer, sub):
        u = _normmod_call(xs, g, md[:, k0], md[:, k0 + 1])
        h = _up_call(u, ffn_w1, ffn_w3, layer, sub)
        return _down_call(h, ffn_w2, xs, md[:, k0 + 2], 0.5, lead=(layer, sub))

    for i in range(depth):
        last = i == depth - 1
        ml = mods[i, :bsz]
        mc = mods[i, bsz:bsz + 1]
        gate_b = jnp.pad(mlstm_gate_b[i].reshape(1, 4 * nh), ((0, 0), (0, GATE_W - 4 * nh)))
        conv_b = mlstm_conv_b[i].reshape(1, -1)
        bias_tab = _na_bias_table(na_relpos[i])

        xc = ffn(xc, mc, 0, norm_g[i, 0], i, 0)
        xl = ffn(xl, ml, 0, norm_g[i, 0], i, 0)

        uc = _normmod_call(xc, norm_g[i, 1], mc[:, 3], mc[:, 4])
        ul = _normmod_call(xl, norm_g[i, 1], ml[:, 3], ml[:, 4])
        pc = _inproj_call(uc, w_main, i)
        pl_ = _inproj_call(ul, w_main, i)
        gc = _matmul_call(uc, w_gates[i])
        gl = _matmul_call(ul, w_gates[i])

        qa_c, qb_c, kb_c, vb_c, ka_c, va_c = _qkprep_call(pc, qk_g[i], cos_t, sin_t, False, lc)
        qa_l, qb_l, kb_l, vb_l, ka_l, va_l = _qkprep_call(pl_, qk_g[i], cos_t, sin_t, True, n)

        a_l = _attn_a_call(qa_l, ka_l, va_l, ka_c, va_c, attn_sink[i], bsz, True)
        n_l = _attn_b_call(qb_l, kb_l, vb_l, kb_c, vb_c, bias_tab, bsz)
        qc_c, kc_c, vc_c = _cprep_call(pc, mlstm_conv_w[i], conv_b, lc)
        qc_l, kc_l, vc_l = _cprep_call(pl_, mlstm_conv_w[i], conv_b, n)
        hc_f, hc_b, s_c, m_c = _mlstm_call(qc_c, kc_c, vc_c, _gateprep_call(gc, gate_b, lc), s_zero, m_zero, bsz)
        hl_f, hl_b, _, _ = _mlstm_call(qc_l, kc_l, vc_l, _gateprep_call(gl, gate_b, n), s_c, m_c, bsz)
        m_l = _mlstm_out_call(hl_f, hl_b, pl_, mlstm_norm_g[i])

        yl = _merge_call(ul, a_l, n_l, m_l, wg, b_gate, wb, i)
        xl = _down_call(yl, w_out, xl, ml[:, 5], 1.0, lead=(i,))
        xl = ffn(xl, ml, 6, norm_g[i, 2], i, 1)
        if not last:
            a_c = _attn_a_call(qa_c, ka_c, va_c, ka_c, va_c, attn_sink[i], bsz, False)
            n_c = _attn_bx_call(qb_c, kb_c, vb_c, bsz)
            mm_c = _mlstm_out_call(hc_f, hc_b, pc, mlstm_norm_g[i])
            yc = _merge_call(uc, a_c, n_c, mm_c, wg, b_gate, wb, i)
            xc = _down_call(yc, w_out, xc, mc[:, 5], 1.0, lead=(i,))
            xc = ffn(xc, mc, 6, norm_g[i, 2], i, 1)
    return xl.reshape(bsz, n, d)
```

```python
import functools

import numpy as np
import jax
import jax.numpy as jnp
from jax import lax
from jax.experimental import pallas as pl
from jax.experimental.pallas import tpu as pltpu

F32 = jnp.float32
BF16 = jnp.bfloat16

HEAD_DIM = 128
GRID_W = 64
ROPE_PAIRS = HEAD_DIM // 4
ROPE_THETA = 10000.0
A_HEADS = 8
A_KV_HEADS = 2
A_GROUP = A_HEADS // A_KV_HEADS
A_WINDOW = 128
A_BLOCK = 128
B_HEADS = 8
NA_ROWS = 8
NA_COLS = 16
C_HEADS = 8
C_CONV = 5
BRANCH_W = 8 * HEAD_DIM
N_MOD = 9
EPS = 1e-6
NEG = -1e30
ATTN_SCALE = HEAD_DIM ** -0.5
LOG2E = 1.4426950408889634
QK_SCALE_LOG2 = ATTN_SCALE * LOG2E

M_CHUNK = 128
NA_TILE_ROWS = 8
NA_TILE = NA_TILE_ROWS * GRID_W
NA_WIN_ROWS = NA_TILE_ROWS + NA_ROWS
NA_HEADS_PER_STEP = 8
A_QBLOCKS = 2

BF16_SUBLANES = 16
V7X_VMEM_BYTES = 64 * 1024 * 1024
VMEM_CAP = V7X_VMEM_BYTES - 6 * 1024 * 1024

COL_AQ, COL_BQ, COL_BK, COL_BV, COL_CQ, COL_CK, COL_CV, COL_CO = range(8)
MAIN_W = 8 * BRANCH_W + 2 * A_KV_HEADS * HEAD_DIM
GATE_W = 128
PROJ_W = MAIN_W + 2 * A_KV_HEADS * HEAD_DIM


def _pick(n, cands):
    for c in cands:
        if n % c == 0:
            return c
    raise ValueError(f"no tile in {cands} divides {n}")


def _params(sem, block_bytes, temp_bytes=0):
    limit = 2 * block_bytes + temp_bytes + 4 * 1024 * 1024
    return pltpu.CompilerParams(dimension_semantics=sem,
                                vmem_limit_bytes=int(min(max(limit, 16 * 1024 * 1024), VMEM_CAP)))


def _nbytes(shape, dtype):
    return int(np.prod(shape)) * jnp.dtype(dtype).itemsize


def _dot(a, b):
    return jnp.dot(a, b, preferred_element_type=F32)


def _dot_nt(a, b):
    return lax.dot_general(a, b, (((1,), (1,)), ((), ())), preferred_element_type=F32)


def _dot_tn(a, b):
    return lax.dot_general(a, b, (((0,), (0,)), ((), ())), preferred_element_type=F32)


def _silu(x):
    return x * jax.nn.sigmoid(x)


def _mod_kernel(c_ref, w_ref, b_ref, o_ref):
    a = _silu(c_ref[...]).astype(BF16)
    o_ref[0] = _dot(a, w_ref[0].astype(BF16)) + b_ref[0]


def _mod_call(cs, w_mod, b_mod):
    depth, d, nd = w_mod.shape
    tn = _pick(nd, (1024, 512, 256, 128))
    rows = cs.shape[0]
    blocks = _nbytes((d, tn), F32) + _nbytes((rows, d), F32) + _nbytes((rows, tn), F32)
    return pl.pallas_call(
        _mod_kernel,
        out_shape=jax.ShapeDtypeStruct((depth, rows, nd), F32),
        grid=(depth, nd // tn),
        in_specs=[pl.BlockSpec((rows, d), lambda l, j: (0, 0)),
                  pl.BlockSpec((1, d, tn), lambda l, j: (l, 0, j)),
                  pl.BlockSpec((1, 1, tn), lambda l, j: (l, 0, j))],
        out_specs=pl.BlockSpec((1, rows, tn), lambda l, j: (l, 0, j)),
        compiler_params=_params(("parallel", "parallel"), blocks, _nbytes((d, tn), BF16)),
        name="mod_vectors",
    )(cs, w_mod, b_mod.reshape(depth, 1, nd))


def _normmod_kernel(x_ref, g_ref, shift_ref, scale_ref, o_ref):
    x = x_ref[...]
    ms = jnp.mean(x * x, axis=-1, keepdims=True)
    y = x * lax.rsqrt(ms + EPS) * g_ref[...]
    o_ref[...] = (y * (1.0 + scale_ref[0]) + shift_ref[0]).astype(o_ref.dtype)


def _normmod_call(x, g, shift, scale):
    m, d = x.shape
    groups = shift.shape[0]
    tr = _pick(m // groups, (512, 256, 128, 64, 8))
    per = (m // groups) // tr
    blocks = _nbytes((tr, d), F32) + _nbytes((tr, d), BF16) + 3 * _nbytes((1, d), F32)
    return pl.pallas_call(
        _normmod_kernel,
        out_shape=jax.ShapeDtypeStruct((m, d), BF16),
        grid=(m // tr,),
        in_specs=[pl.BlockSpec((tr, d), lambda i: (i, 0)),
                  pl.BlockSpec((1, d), lambda i: (0, 0)),
                  pl.BlockSpec((1, 1, d), lambda i: (i // per, 0, 0)),
                  pl.BlockSpec((1, 1, d), lambda i: (i // per, 0, 0))],
        out_specs=pl.BlockSpec((tr, d), lambda i: (i, 0)),
        compiler_params=_params(("parallel",), blocks, 2 * _nbytes((tr, d), F32)),
        name="norm_modulate",
    )(x, g.reshape(1, d), shift.reshape(groups, 1, d), scale.reshape(groups, 1, d))


def _prefetch_chunks(tm, nj):
    n = 1
    while 2 * n <= nj and tm % (2 * n) == 0 and (tm // (2 * n)) % BF16_SUBLANES == 0:
        n *= 2
    return n


def _rowblock(a_hbm, abuf, sem, *, ni, nj, tm):
    i = pl.program_id(0)
    j = pl.program_id(1)
    slot = i % 2
    nch = _prefetch_chunks(tm, nj)
    ch = tm // nch

    def chunk_copy(blk, c, slot_):
        return pltpu.make_async_copy(a_hbm.at[pl.ds(blk * tm + c * ch, ch), :],
                                     abuf.at[slot_, pl.ds(c * ch, ch), :], sem.at[slot_])

    @pl.when((i == 0) & (j == 0))
    def _():
        for c in range(nch):
            chunk_copy(0, c, 0).start()

    @pl.when(j == 0)
    def _():
        for c in range(nch):
            chunk_copy(i, c, slot).wait()

    @pl.when((i + 1 < ni) & (j < nch))
    def _():
        chunk_copy(i + 1, j, 1 - slot).start()

    return abuf[slot]


def _rowblock_scratch(tm, k):
    return [pltpu.VMEM((2, tm, k), BF16), pltpu.SemaphoreType.DMA((2,))]


def _up_kernel(u_hbm, w1_ref, w3_ref, o_ref, ubuf, sem, *, ni, nj, tm):
    u = _rowblock(u_hbm, ubuf, sem, ni=ni, nj=nj, tm=tm)
    h1 = _dot(u, w1_ref[...].astype(BF16))
    h3 = _dot(u, w3_ref[...].astype(BF16))
    o_ref[...] = (_silu(h1) * h3).astype(o_ref.dtype)


def _up_call(u, w1, w3, layer, sub):
    m, d = u.shape
    f = w1.shape[-1]
    tm = _pick(m, (2048, 1024, 512, 256))
    tf = _pick(f, (256, 128))
    wspec = pl.BlockSpec((None, None, d, tf), lambda i, j: (layer, sub, 0, j))
    ni, nj = m // tm, f // tf
    blocks = 2 * _nbytes((d, tf), F32) + _nbytes((tm, tf), BF16)
    return pl.pallas_call(
        functools.partial(_up_kernel, ni=ni, nj=nj, tm=tm),
        out_shape=jax.ShapeDtypeStruct((m, f), BF16),
        grid=(ni, nj),
        in_specs=[pl.BlockSpec(memory_space=pl.ANY), wspec, wspec],
        out_specs=pl.BlockSpec((tm, tf), lambda i, j: (i, j)),
        scratch_shapes=_rowblock_scratch(tm, d),
        compiler_params=_params(("arbitrary", "arbitrary"), blocks,
                                2 * _nbytes((tm, d), BF16) + 2 * _nbytes((d, tf), BF16)
                                + 4 * _nbytes((tm, tf), F32)),
        name="ffn_up",
    )(u, w1, w3)


def _down_kernel(a_hbm, w_ref, x_ref, gate_ref, o_ref, abuf, sem, *, coef, ni, nj, tm):
    a = _rowblock(a_hbm, abuf, sem, ni=ni, nj=nj, tm=tm)
    acc = _dot(a, w_ref[...].astype(BF16))
    o_ref[...] = x_ref[...] + (coef * gate_ref[0]) * acc


def _down_call(a, w, x, gate, coef, lead=()):
    m, k = a.shape
    d = w.shape[-1]
    groups = gate.shape[0]
    tm = _pick(m // groups, (1024, 512, 256))
    tn = _pick(d, (512, 256, 128) if k <= 4096 else (256, 128))
    per = (m // groups) // tm
    ni, nj = m // tm, d // tn
    blocks = _nbytes((k, tn), w.dtype) + 2 * _nbytes((tm, tn), F32) + _nbytes((1, tn), F32)
    temps = (2 * _nbytes((tm, k), BF16) + 2 * _nbytes((tm, tn), F32)
             + (_nbytes((k, tn), BF16) if w.dtype != BF16 else 0))
    wspec = pl.BlockSpec((None,) * len(lead) + (k, tn), lambda i, j: tuple(lead) + (0, j))
    return pl.pallas_call(
        functools.partial(_down_kernel, coef=coef, ni=ni, nj=nj, tm=tm),
        out_shape=jax.ShapeDtypeStruct((m, d), F32),
        grid=(ni, nj),
        in_specs=[pl.BlockSpec(memory_space=pl.ANY),
                  wspec,
                  pl.BlockSpec((tm, tn), lambda i, j: (i, j)),
                  pl.BlockSpec((1, 1, tn), lambda i, j: (i // per, 0, j))],
        out_specs=pl.BlockSpec((tm, tn), lambda i, j: (i, j)),
        scratch_shapes=_rowblock_scratch(tm, k),
        compiler_params=_params(("arbitrary", "arbitrary"), blocks, temps),
        name="proj_residual",
    )(a, w, x, gate.reshape(groups, 1, d))


def _matmul_kernel(a_ref, w_ref, o_ref):
    o_ref[...] = _dot(a_ref[...], w_ref[...]).astype(o_ref.dtype)


def _inproj_kernel(u_hbm, w_ref, o_ref, ubuf, sem, *, ni, nj, tm):
    u = _rowblock(u_hbm, ubuf, sem, ni=ni, nj=nj, tm=tm)
    o_ref[...] = _dot(u, w_ref[...])


IN_TN = 2 * A_KV_HEADS * HEAD_DIM


def _inproj_call(u, w_main, layer):
    m, d = u.shape
    tm = _pick(m, (2048, 1024, 512, 256))
    tn = IN_TN
    ni, nj = m // tm, PROJ_W // tn
    blocks = _nbytes((d, tn), BF16) + _nbytes((tm, tn), F32)
    return pl.pallas_call(
        functools.partial(_inproj_kernel, ni=ni, nj=nj, tm=tm),
        out_shape=jax.ShapeDtypeStruct((m, PROJ_W), F32),
        grid=(ni, nj),
        in_specs=[pl.BlockSpec(memory_space=pl.ANY),
                  pl.BlockSpec((None, d, tn), lambda i, j: (layer, 0, j))],
        out_specs=pl.BlockSpec((tm, tn), lambda i, j: (i, j)),
        scratch_shapes=_rowblock_scratch(tm, d),
        compiler_params=_params(("arbitrary", "arbitrary"), blocks,
                                2 * _nbytes((tm, d), BF16) + 2 * _nbytes((tm, tn), F32)),
        name="in_proj",
    )(u, w_main)


def _matmul_call(a, w, out_dtype=F32):
    m, k = a.shape
    n = w.shape[1]
    tm = _pick(m, (1024, 512, 256))
    tn = _pick(n, (512, 256, 128))
    blocks = _nbytes((tm, k), BF16) + _nbytes((k, tn), BF16) + _nbytes((tm, tn), out_dtype)
    return pl.pallas_call(
        _matmul_kernel,
        out_shape=jax.ShapeDtypeStruct((m, n), out_dtype),
        grid=(m // tm, n // tn),
        in_specs=[pl.BlockSpec((tm, k), lambda i, j: (i, 0)),
                  pl.BlockSpec((k, tn), lambda i, j: (0, j))],
        out_specs=pl.BlockSpec((tm, tn), lambda i, j: (i, j)),
        compiler_params=_params(("parallel", "parallel"), blocks, _nbytes((tm, tn), F32)),
        name="in_proj",
    )(a, w)


def _merge_kernel(u_ref, oa_ref, ob_ref, oc_ref, wg_ref, bg_ref, wb_ref, y_ref):
    u = u_ref[...]
    acc = None
    for j, br_ref in enumerate((oa_ref, ob_ref, oc_ref)):
        gate = jax.nn.sigmoid(_dot(u, wg_ref[j]) + bg_ref[j])
        term = gate * _dot(br_ref[...], wb_ref[j])
        acc = term if acc is None else acc + term
    y_ref[...] = acc.astype(y_ref.dtype)


def _merge_call(u, oa, ob, oc, wg, bg, wb, layer):
    m, d = u.shape
    bw = oa.shape[1]
    tm = _pick(m, (1024, 512, 256))
    tn = _pick(d, (256, 128))
    blocks = (_nbytes((tm, d), BF16) + 3 * _nbytes((tm, bw), BF16) + 3 * _nbytes((d, tn), BF16)
              + 3 * _nbytes((bw, tn), BF16) + _nbytes((tm, tn), BF16))
    return pl.pallas_call(
        _merge_kernel,
        out_shape=jax.ShapeDtypeStruct((m, d), BF16),
        grid=(m // tm, d // tn),
        in_specs=[pl.BlockSpec((tm, d), lambda i, j: (i, 0)),
                  pl.BlockSpec((tm, bw), lambda i, j: (i, 0)),
                  pl.BlockSpec((tm, bw), lambda i, j: (i, 0)),
                  pl.BlockSpec((tm, bw), lambda i, j: (i, 0)),
                  pl.BlockSpec((None, 3, d, tn), lambda i, j: (layer, 0, 0, j)),
                  pl.BlockSpec((None, 3, 1, tn), lambda i, j: (layer, 0, 0, j)),
                  pl.BlockSpec((None, 3, bw, tn), lambda i, j: (layer, 0, 0, j))],
        out_specs=pl.BlockSpec((tm, tn), lambda i, j: (i, j)),
        compiler_params=_params(("parallel", "parallel"), blocks, 10 * _nbytes((tm, tn), F32)),
        name="gated_merge",
    )(u, oa, ob, oc, wg, bg.reshape(bg.shape[0], 3, 1, d), wb)


def _head_norm(x, g):
    ms = jnp.mean(x * x, axis=-1, keepdims=True)
    return x * lax.rsqrt(ms + EPS) * g


def _rope(y, cos, sin_signed):
    lane = lax.broadcasted_iota(jnp.int32, y.shape, 1)
    partner = jnp.where((lane % 64) < ROPE_PAIRS, pltpu.roll(y, HEAD_DIM - ROPE_PAIRS, axis=1),
                        pltpu.roll(y, ROPE_PAIRS, axis=1))
    return y * cos + partner * sin_signed


def _qkprep_kernel(p4_ref, pa_ref, g_ref, cos_ref, sin_ref,
                   qa_ref, qb_ref, kb_ref, vb_ref, ka_ref, va_ref, *, rope):
    hd = HEAD_DIM
    if rope:
        cos = cos_ref[...]
        sin = sin_ref[...]
    for h in range(A_HEADS):
        y = _head_norm(p4_ref[:, h * hd:(h + 1) * hd], g_ref[0:1, :])
        if rope:
            y = _rope(y, cos, sin)
        qa_ref[:, h * hd:(h + 1) * hd] = y.astype(BF16)
    for h in range(B_HEADS):
        c0 = BRANCH_W + h * hd
        qb_ref[:, h * hd:(h + 1) * hd] = _head_norm(p4_ref[:, c0:c0 + hd], g_ref[2:3, :]).astype(BF16)
        c0 = 2 * BRANCH_W + h * hd
        kb_ref[:, h * hd:(h + 1) * hd] = _head_norm(p4_ref[:, c0:c0 + hd], g_ref[3:4, :]).astype(BF16)
    vb_ref[...] = p4_ref[:, 3 * BRANCH_W:4 * BRANCH_W].astype(BF16)
    for h in range(A_KV_HEADS):
        y = _head_norm(pa_ref[:, h * hd:(h + 1) * hd], g_ref[1:2, :])
        if rope:
            y = _rope(y, cos, sin)
        ka_ref[:, h * hd:(h + 1) * hd] = y.astype(BF16)
    kvw = A_KV_HEADS * hd
    va_ref[...] = pa_ref[:, kvw:2 * kvw].astype(BF16)


def _qkprep_call(p_main, qk_g, cos_t, sin_t, rope, seq):
    m = p_main.shape[0]
    tr = _pick(seq, (512, 256, 128))
    per = seq // tr
    kvw = A_KV_HEADS * HEAD_DIM
    w4 = 4 * BRANCH_W
    blocks = (_nbytes((tr, w4), F32) + _nbytes((tr, 2 * kvw), F32) + 2 * _nbytes((tr, HEAD_DIM), F32)
              + _nbytes((tr, w4), BF16) + _nbytes((tr, 2 * kvw), BF16))
    outs = [jax.ShapeDtypeStruct((m, BRANCH_W), BF16)] * 4 + [jax.ShapeDtypeStruct((m, kvw), BF16)] * 2
    return pl.pallas_call(
        functools.partial(_qkprep_kernel, rope=rope),
        out_shape=outs,
        grid=(m // tr,),
        in_specs=[pl.BlockSpec((tr, w4), lambda i: (i, 0)),
                  pl.BlockSpec((tr, 2 * kvw), lambda i: (i, (8 * BRANCH_W) // (2 * kvw))),
                  pl.BlockSpec((4, HEAD_DIM), lambda i: (0, 0)),
                  pl.BlockSpec((tr, HEAD_DIM), lambda i: (i % per, 0)),
                  pl.BlockSpec((tr, HEAD_DIM), lambda i: (i % per, 0))],
        out_specs=[pl.BlockSpec((tr, BRANCH_W), lambda i: (i, 0))] * 4
                  + [pl.BlockSpec((tr, kvw), lambda i: (i, 0))] * 2,
        compiler_params=_params(("parallel",), blocks, 4 * _nbytes((tr, HEAD_DIM), F32)),
        name="qk_prep",
    )(p_main, p_main, qk_g, cos_t, sin_t)


def _attn_a_kernel(sink_ref, q_ref, *rest, lc, window):
    nwin = A_QBLOCKS + 2 if window else 0
    nmask = A_QBLOCKS if window else 0
    k_refs = rest[:nwin]
    v_refs = rest[nwin:2 * nwin]
    mask_refs = rest[2 * nwin:2 * nwin + nmask]
    kx_ref, vx_ref, o_ref = rest[2 * nwin + nmask:]
    hd = HEAD_DIM
    blk = A_BLOCK
    gw = A_GROUP * hd
    keys = lc + 3 * blk if window else lc
    ones = jnp.ones((keys, hd), BF16)
    for sb in range(A_QBLOCKS):
        rows = slice(sb * blk, (sb + 1) * blk)
        for hk in range(A_KV_HEADS):
            cs = slice(hk * hd, (hk + 1) * hd)
            if window:
                kall = jnp.concatenate([kx_ref[:, cs]] + [r[:, cs] for r in k_refs[sb:sb + 3]], axis=0)
                vall = jnp.concatenate([vx_ref[:, cs]] + [r[:, cs] for r in v_refs[sb:sb + 3]], axis=0)
            else:
                kall = kx_ref[:, cs]
                vall = vx_ref[:, cs]
            q4 = jnp.concatenate([q_ref[rows, hk * gw + g * hd:hk * gw + (g + 1) * hd] for g in range(A_GROUP)],
                                 axis=0)
            s = _dot_nt(q4, kall) * QK_SCALE_LOG2
            if window:
                s = s + mask_refs[sb][0]
            snk = jnp.concatenate(
                [jnp.full((blk, 1), sink_ref[hk * A_GROUP + g] * LOG2E, F32) for g in range(A_GROUP)], axis=0)
            mx = jnp.maximum(jnp.max(s, axis=-1, keepdims=True), snk)
            e = jnp.exp2(s - mx).astype(BF16)
            both = _dot(e, jnp.concatenate([vall, ones], axis=1))
            o = both[:, 0:hd] * (1.0 / (both[:, hd:2 * hd] + jnp.exp2(snk - mx)))
            for g in range(A_GROUP):
                o_ref[rows, hk * gw + g * hd:hk * gw + (g + 1) * hd] = (
                    o[g * blk:(g + 1) * blk].astype(o_ref.dtype))


def _gqa_mask_table(lc, nb):
    blk = A_BLOCK
    r = np.arange(blk)[:, None]
    kc = np.arange(3 * blk)[None, :]
    band = np.abs(r + blk - kc) <= A_WINDOW
    tabs = []
    for first, last in ((True, False), (False, False), (False, True)):
        ok = band & ~(first & (kc < blk)) & ~(last & (kc >= 2 * blk))
        win = np.where(ok, 0.0, NEG).astype(np.float32)
        tabs.append(np.tile(np.concatenate([np.zeros((blk, lc), np.float32), win], axis=1), (A_GROUP, 1)))
    assert nb >= 2
    return np.stack(tabs)


def _attn_a_call(qa, ka, va, kx, vx, sink, bsz, window):
    n = qa.shape[0] // bsz
    lc = kx.shape[0] // bsz
    blk = A_BLOCK
    nb = n // blk
    qb = A_QBLOCKS
    assert nb % qb == 0
    nt = nb // qb
    kvw = A_KV_HEADS * HEAD_DIM
    qw = A_HEADS * HEAD_DIM
    qspec = pl.BlockSpec((qb * blk, qw), lambda b, t: (b * nt + t, 0))
    xspec = pl.BlockSpec((lc, kvw), lambda b, t: (b, 0))
    sspec = pl.BlockSpec(memory_space=pltpu.SMEM)
    if window:
        def kspec(off):
            return pl.BlockSpec((blk, kvw), lambda b, t: (b * nb + jnp.clip(t * qb + off, 0, nb - 1), 0))

        def mspec(sb):
            def variant(b, t):
                i = t * qb + sb
                return (jnp.where(i == 0, 0, jnp.where(i == nb - 1, 2, 1)), 0, 0)
            return pl.BlockSpec((1, A_GROUP * blk, lc + 3 * blk), variant)

        kspecs = [kspec(off) for off in range(-1, qb + 1)]
        mspecs = [mspec(sb) for sb in range(qb)]
        mask = jnp.asarray(_gqa_mask_table(lc, nb))
        in_specs = [sspec, qspec] + kspecs + kspecs + mspecs + [xspec, xspec]
        args = (sink, qa) + (ka,) * len(kspecs) + (va,) * len(kspecs) + (mask,) * qb + (kx, vx)
        keys = lc + 3 * blk
    else:
        in_specs = [sspec, qspec, xspec, xspec]
        args = (sink, qa, kx, vx)
        keys = lc
    blocks = (2 * _nbytes((qb * blk, qw), BF16) + 2 * _nbytes((keys + qb * blk, kvw), BF16)
              + (qb * _nbytes((A_GROUP * blk, keys), F32) if window else 0))
    return pl.pallas_call(
        functools.partial(_attn_a_kernel, lc=lc, window=window),
        out_shape=jax.ShapeDtypeStruct(qa.shape, BF16),
        grid=(bsz, nt),
        in_specs=in_specs,
        out_specs=qspec,
        compiler_params=_params(("parallel", "parallel"), blocks,
                                10 * qb * _nbytes((A_GROUP * blk, keys), F32)),
        name="windowed_gqa" if window else "context_gqa",
    )(*args)


def _softmax2_pv(s1, v1, s2, v2):
    hd = v1.shape[1]
    mx = jnp.maximum(jnp.max(s1, axis=-1, keepdims=True), jnp.max(s2, axis=-1, keepdims=True))
    e1 = jnp.exp2(s1 - mx).astype(BF16)
    e2 = jnp.exp2(s2 - mx).astype(BF16)
    v1e = jnp.concatenate([v1, jnp.ones(v1.shape, BF16)], axis=1)
    v2e = jnp.concatenate([v2, jnp.ones(v2.shape, BF16)], axis=1)
    both = _dot(e1, v1e) + _dot(e2, v2e)
    return both[:, 0:hd] * (1.0 / both[:, hd:2 * hd])


def _attn_b_kernel(q_ref, k0_ref, k1_ref, k2_ref, k3_ref, v0_ref, v1_ref, v2_ref, v3_ref, kx_ref, vx_ref,
                   bias_ref, o_ref):
    hd = HEAD_DIM
    for h in range(NA_HEADS_PER_STEP):
        cs = slice(h * hd, (h + 1) * hd)
        kwin = jnp.concatenate([k0_ref[:, cs], k1_ref[:, cs], k2_ref[:, cs], k3_ref[:, cs]], axis=0)
        vwin = jnp.concatenate([v0_ref[:, cs], v1_ref[:, cs], v2_ref[:, cs], v3_ref[:, cs]], axis=0)
        q = q_ref[:, cs]
        s_nb = _dot_nt(q, kwin) * QK_SCALE_LOG2 + bias_ref[h, 0]
        s_cx = _dot_nt(q, kx_ref[:, cs]) * QK_SCALE_LOG2
        o_ref[:, cs] = _softmax2_pv(s_nb, vwin, s_cx, vx_ref[:, cs]).astype(o_ref.dtype)


def _attn_bx_kernel(q_ref, kx_ref, vx_ref, o_ref):
    hd = HEAD_DIM
    for h in range(B_HEADS):
        cs = slice(h * hd, (h + 1) * hd)
        s = _dot_nt(q_ref[:, cs], kx_ref[:, cs]) * QK_SCALE_LOG2
        mx = jnp.max(s, axis=-1, keepdims=True)
        e = jnp.exp2(s - mx)
        inv = 1.0 / jnp.sum(e, axis=-1, keepdims=True)
        o_ref[:, cs] = (_dot(e.astype(BF16), vx_ref[:, cs]) * inv).astype(o_ref.dtype)


def _na_bias_kernel(rel_ref, o_ref):
    h = pl.program_id(0)
    pos = pl.program_id(1)
    nrel_r = 2 * NA_ROWS - 1
    nrel_c = 2 * NA_COLS - 1
    qi = lax.broadcasted_iota(jnp.int32, (GRID_W, GRID_W), 0)
    ki = lax.broadcasted_iota(jnp.int32, (GRID_W, GRID_W), 1)
    dc = jnp.clip(ki - qi, -(NA_COLS - 1), NA_COLS - 1) + NA_COLS - 1
    start = jnp.clip(qi - NA_COLS // 2, 0, GRID_W - NA_COLS)
    col_in = (ki >= start) & (ki < start + NA_COLS)
    planes = []
    for dr in range(nrel_r):
        acc = jnp.zeros((GRID_W, GRID_W), F32)
        for dcv in range(nrel_c):
            acc = jnp.where(dc == dcv, rel_ref[(h * nrel_r + dr) * nrel_c + dcv], acc)
        planes.append(jnp.where(col_in, acc * LOG2E, NEG))
    masked = jnp.full((GRID_W, GRID_W), NEG, F32)
    half = NA_ROWS // 2
    first_key = {0: lambda j: max(j, half), 1: lambda j: j, 2: lambda j: min(j, half)}
    for p, lo_of in first_key.items():
        @pl.when(pos == p)
        def _(lo_of=lo_of):
            for j in range(NA_TILE_ROWS):
                lo = lo_of(j)
                for jj in range(NA_WIN_ROWS):
                    blk = planes[jj - j + half - 1] if lo <= jj < lo + NA_ROWS else masked
                    o_ref[0, 0, j * GRID_W:(j + 1) * GRID_W, jj * GRID_W:(jj + 1) * GRID_W] = blk


def _na_bias_table(relpos):
    nh = relpos.shape[0]
    shape = (nh, 3, NA_TILE, NA_WIN_ROWS * GRID_W)
    return pl.pallas_call(
        _na_bias_kernel,
        out_shape=jax.ShapeDtypeStruct(shape, F32),
        grid=(nh, 3),
        in_specs=[pl.BlockSpec(memory_space=pltpu.SMEM)],
        out_specs=pl.BlockSpec((1, 1) + shape[2:], lambda h, p: (h, p, 0, 0)),
        compiler_params=_params(("parallel", "parallel"), _nbytes(shape[2:], F32)),
        name="na_bias_table",
    )(relpos.reshape(-1))


def _attn_b_call(qb, kb, vb, kx, vx, bias, bsz):
    n = qb.shape[0] // bsz
    lc = kx.shape[0] // bsz
    rows = n // GRID_W
    assert rows % NA_TILE_ROWS == 0 and rows >= 2 * NA_TILE_ROWS
    nt = rows // NA_TILE_ROWS
    hp = NA_HEADS_PER_STEP
    hd = hp * HEAD_DIM
    sub = NA_TILE // 2
    qspec = pl.BlockSpec((NA_TILE, hd), lambda b, h, t: (b * nt + t, h))

    def kspec(s):
        return pl.BlockSpec((sub, hd),
                            lambda b, h, t: (jnp.clip(2 * t + s, 0, 2 * nt - 1) + 2 * b * nt, h))

    kspecs = [kspec(s) for s in (-1, 0, 1, 2)]
    xspec = pl.BlockSpec((lc, hd), lambda b, h, t: (b, h))
    bspec = pl.BlockSpec((hp, 1) + bias.shape[2:],
                         lambda b, h, t: (h, jnp.where(t == 0, 0, jnp.where(t == nt - 1, 2, 1)), 0, 0))
    blocks = (2 * _nbytes((NA_TILE, hd), BF16) + 8 * _nbytes((sub, hd), BF16) + 2 * _nbytes((lc, hd), BF16)
              + hp * _nbytes(bias.shape[2:], F32))
    return pl.pallas_call(
        _attn_b_kernel,
        out_shape=jax.ShapeDtypeStruct(qb.shape, BF16),
        grid=(bsz, B_HEADS // hp, nt),
        in_specs=[qspec] + kspecs + kspecs + [xspec, xspec, bspec],
        out_specs=qspec,
        compiler_params=_params(("parallel", "parallel", "parallel"), blocks,
                                5 * hp * _nbytes((NA_TILE, NA_WIN_ROWS * GRID_W + lc), F32)),
        name="neighbourhood_attn",
    )(qb, kb, kb, kb, kb, vb, vb, vb, vb, kx, vx, bias)


def _attn_bx_call(qx, kx, vx, bsz):
    lc = qx.shape[0] // bsz
    w = qx.shape[1]
    spec = pl.BlockSpec((lc, w), lambda b: (b, 0))
    return pl.pallas_call(
        _attn_bx_kernel,
        out_shape=jax.ShapeDtypeStruct(qx.shape, BF16),
        grid=(bsz,),
        in_specs=[spec, spec, spec],
        out_specs=spec,
        compiler_params=_params(("parallel",), 4 * _nbytes((lc, w), BF16), 4 * 1024 * 1024),
        name="context_full_attn",
    )(qx, kx, vx)


def _log_sigmoid(x):
    return -(jnp.maximum(-x, 0.0) + jnp.log1p(jnp.exp(-jnp.abs(x))))


CONV_HALO = 8


def _cprep_kernel(q_ref, k_ref, v_ref, qlo_ref, qhi_ref, klo_ref, khi_ref, cw_ref, cb_ref,
                  qs_ref, kt_ref, vb_ref, ext, *, per):
    pos = pl.program_id(0) % per
    tr = q_ref.shape[0]
    hw = q_ref.shape[1]
    halo = CONV_HALO
    pad = C_CONV // 2

    def conv_silu(x_ref, lo_ref, hi_ref, col0):
        ext[0:halo, :] = jnp.where(pos > 0, lo_ref[...], 0.0)
        ext[halo:halo + tr, :] = x_ref[...]
        ext[halo + tr:2 * halo + tr, :] = jnp.where(pos < per - 1, hi_ref[...], 0.0)
        out = cb_ref[:, col0:col0 + hw]
        for j in range(C_CONV):
            out = out + cw_ref[j:j + 1, col0:col0 + hw] * ext[halo - pad + j:halo - pad + j + tr, :]
        return _silu(out)

    qs_ref[...] = (conv_silu(q_ref, qlo_ref, qhi_ref, 0) * ATTN_SCALE).astype(BF16)
    kc = conv_silu(k_ref, klo_ref, khi_ref, hw)
    for ck in range(tr // M_CHUNK):
        for h in range(C_HEADS):
            blk = kc[ck * M_CHUNK:(ck + 1) * M_CHUNK, h * HEAD_DIM:(h + 1) * HEAD_DIM]
            kt_ref[ck, h * HEAD_DIM:(h + 1) * HEAD_DIM, :] = blk.T.astype(BF16)
    vb_ref[...] = v_ref[...].astype(BF16)


def _cprep_call(p_main, conv_w, conv_b, seq):
    m = p_main.shape[0]
    hw = C_HEADS * HEAD_DIM
    assert M_CHUNK == HEAD_DIM
    tr = _pick(seq, (512, 256, 128))
    per = seq // tr
    halo = CONV_HALO
    hb = tr // halo
    nblk = m // halo
    cpb = tr // M_CHUNK

    def main(col):
        return pl.BlockSpec((tr, hw), lambda i: (i, col))

    def lo(col):
        return pl.BlockSpec((halo, hw), lambda i: (jnp.maximum(i * hb - 1, 0), col))

    def hi(col):
        return pl.BlockSpec((halo, hw), lambda i: (jnp.minimum((i + 1) * hb, nblk - 1), col))

    blocks = (3 * _nbytes((tr, hw), F32) + 4 * _nbytes((halo, hw), F32) + _nbytes((C_CONV + 1, 2 * hw), F32)
              + 3 * _nbytes((tr, hw), BF16))
    scratch = _nbytes((tr + 2 * halo, hw), F32)
    out = jax.ShapeDtypeStruct((m, hw), BF16)
    out_t = jax.ShapeDtypeStruct((m // M_CHUNK, hw, M_CHUNK), BF16)
    ospec = pl.BlockSpec((tr, hw), lambda i: (i, 0))
    tspec = pl.BlockSpec((cpb, hw, M_CHUNK), lambda i: (i, 0, 0))
    return pl.pallas_call(
        functools.partial(_cprep_kernel, per=per),
        out_shape=[out, out_t, out],
        grid=(m // tr,),
        in_specs=[main(COL_CQ), main(COL_CK), main(COL_CV), lo(COL_CQ), hi(COL_CQ), lo(COL_CK), hi(COL_CK),
                  pl.BlockSpec((C_CONV, 2 * hw), lambda i: (0, 0)),
                  pl.BlockSpec((1, 2 * hw), lambda i: (0, 0))],
        out_specs=[ospec, tspec, ospec],
        scratch_shapes=[pltpu.VMEM((tr + 2 * halo, hw), F32)],
        compiler_params=_params(("parallel",), blocks, scratch + 4 * _nbytes((tr, hw), F32)),
        name="mlstm_prep",
    )(p_main, p_main, p_main, p_main, p_main, p_main, p_main, conv_w, conv_b)


GP_PLANES = 6


def _gateprep_kernel(g_ref, gb_ref, o_ref):
    nh = C_HEADS
    L = M_CHUNK
    lane = lax.broadcasted_iota(jnp.int32, (nh, L), 1)
    for ck in range(g_ref.shape[0] // L):
        gt = (g_ref[ck * L:(ck + 1) * L, :] + gb_ref[...]).T
        for d in range(2):
            ig = gt[2 * d * nh:(2 * d + 1) * nh]
            logf = _log_sigmoid(gt[(2 * d + 1) * nh:(2 * d + 2) * nh])
            pre = logf
            sft = 1
            while sft < L:
                pre = pre + jnp.where(lane >= sft, pltpu.roll(pre, sft, axis=1), 0.0)
                sft *= 2
            total = jnp.broadcast_to(pre[:, L - 1:L], (nh, L))
            bcum = pre if d == 0 else total - pre + logf
            r_row = ig - bcum
            rmax = r_row
            sft = 1
            while sft < L:
                if d == 0:
                    rmax = jnp.maximum(rmax, jnp.where(lane >= sft, pltpu.roll(rmax, sft, axis=1), NEG))
                else:
                    rmax = jnp.maximum(rmax, jnp.where(lane < L - sft, pltpu.roll(rmax, L - sft, axis=1), NEG))
                sft *= 2
            r_top = jnp.broadcast_to(rmax[:, L - 1:L] if d == 0 else rmax[:, 0:1], (nh, L))
            w_row = jnp.exp(r_row - r_top)
            for k, plane in enumerate((r_row, rmax, bcum, total, w_row, r_top)):
                o_ref[ck, d, k * nh:(k + 1) * nh, :] = plane


def _gateprep_call(p_main, gate_b, seq):
    m = p_main.shape[0]
    L = M_CHUNK
    tr = _pick(seq, (1024, 512, 256, 128))
    shape = (m // L, 2, GP_PLANES * C_HEADS, L)
    blocks = _nbytes((tr, GATE_W), F32) + _nbytes((tr // L,) + shape[1:], F32)
    return pl.pallas_call(
        _gateprep_kernel,
        out_shape=jax.ShapeDtypeStruct(shape, F32),
        grid=(m // tr,),
        in_specs=[pl.BlockSpec((tr, GATE_W), lambda i: (i, MAIN_W // GATE_W)),
                  pl.BlockSpec((1, GATE_W), lambda i: (0, 0))],
        out_specs=pl.BlockSpec((tr // L,) + shape[1:], lambda i: (i, 0, 0, 0)),
        compiler_params=_params(("parallel",), blocks, 2 * 1024 * 1024),
        name="mlstm_gate_prep",
    )(p_main, gate_b)


def _mlstm_kernel(qf_ref, ktf_ref, vf_ref, gpf_ref, qb_ref, ktb_ref, vb_ref, gpb_ref, s0_ref, m0_ref,
                  hf_ref, hb_ref, sf_ref, mf_ref, cst, mst, *, nc):
    c = pl.program_id(1)
    L = M_CHUNK
    hd = HEAD_DIM
    nh = C_HEADS

    @pl.when(c == 0)
    def _():
        cst[...] = s0_ref[...]
        mst[...] = m0_ref[...]

    ii = lax.broadcasted_iota(jnp.int32, (L, L), 0)
    jj = lax.broadcasted_iota(jnp.int32, (L, L), 1)
    ones = jnp.ones((L, hd), BF16)
    dirs = ((qf_ref, ktf_ref, vf_ref, gpf_ref, hf_ref, ii >= jj), (qb_ref, ktb_ref, vb_ref, gpb_ref, hb_ref, ii <= jj))
    for d, (q_ref, kt_ref, v_ref, gp_ref, h_ref, causal) in enumerate(dirs):
        r_row, rmax, bcum, total, w_row, r_top = (gp_ref[0, 0, k * nh:(k + 1) * nh, :] for k in range(GP_PLANES))
        m0 = mst[d]
        big_m = jnp.maximum(rmax, m0)
        si_row = jnp.exp(m0 - big_m)
        nrm_row = jnp.exp(-(bcum + big_m))
        m_loc = total + r_top
        m_new = jnp.maximum(total + m0, m_loc)
        s_prev = jnp.exp(total + m0 - m_new)
        s_loc = jnp.exp(m_loc - m_new)
        mst[d] = m_new
        xt = jnp.concatenate([big_m, si_row, nrm_row, jnp.zeros((L - 3 * nh, L), F32)], axis=0).T
        for h in range(nh):
            cs = slice(h * hd, (h + 1) * hd)
            qs = q_ref[:, cs]
            kt = kt_ref[0, cs, :]
            vext = jnp.concatenate([v_ref[:, cs], ones], axis=1)
            cext = cst[d * nh + h]
            expo = jnp.where(causal, r_row[h:h + 1, :] - xt[:, h:h + 1], NEG)
            sw = jnp.exp(expo) * _dot(qs, kt)
            si = jnp.broadcast_to(xt[:, nh + h:nh + h + 1], (L, 2 * hd))
            both = _dot(sw.astype(BF16), vext) + si * _dot(qs, cext.astype(BF16))
            nrm = jnp.broadcast_to(xt[:, 2 * nh + h:2 * nh + h + 1], (L, hd))
            h_ref[:, cs] = both[:, 0:hd] / jnp.maximum(jnp.abs(both[:, hd:2 * hd]), nrm)
            kwt = (kt.astype(F32) * w_row[h:h + 1, :]).astype(BF16)
            sp = jnp.concatenate([s_prev[h:h + 1, :], s_prev[h:h + 1, :]], axis=1)
            sl = jnp.concatenate([s_loc[h:h + 1, :], s_loc[h:h + 1, :]], axis=1)
            cst[d * nh + h] = sp * cext + sl * _dot(kwt, vext)

    @pl.when(c == nc - 1)
    def _():
        sf_ref[...] = cst[...]
        mf_ref[...] = mst[...]


def _mlstm_call(qs, kt, vb, gplanes, s0, m0, bsz):
    t = qs.shape[0] // bsz
    L = M_CHUNK
    nc = t // L
    nh = C_HEADS
    hw = nh * HEAD_DIM

    def cidx(d, b, c):
        return b * nc + (c if d == 0 else nc - 1 - c)

    def dir_specs(d):
        return [pl.BlockSpec((L, hw), lambda b, c: (cidx(d, b, c), 0)),
                pl.BlockSpec((1, hw, L), lambda b, c: (cidx(d, b, c), 0, 0)),
                pl.BlockSpec((L, hw), lambda b, c: (cidx(d, b, c), 0)),
                pl.BlockSpec((1, 1, GP_PLANES * nh, L), lambda b, c: (cidx(d, b, c), d, 0, 0))]

    sspec = pl.BlockSpec((2 * nh, HEAD_DIM, 2 * HEAD_DIM), lambda b, c: (b, 0, 0))
    mspec = pl.BlockSpec((2, nh, 128), lambda b, c: (b, 0, 0))
    blocks = (6 * _nbytes((L, hw), BF16) + 2 * _nbytes((GP_PLANES * nh, L), F32)
              + 4 * _nbytes((nh, HEAD_DIM, 2 * HEAD_DIM), F32) + 2 * _nbytes((L, hw), F32))
    scratch = 2 * _nbytes((nh, HEAD_DIM, 2 * HEAD_DIM), F32)
    hshape = jax.ShapeDtypeStruct((bsz * t, hw), F32)
    return pl.pallas_call(
        functools.partial(_mlstm_kernel, nc=nc),
        out_shape=[hshape, hshape, jax.ShapeDtypeStruct(s0.shape, F32), jax.ShapeDtypeStruct(m0.shape, F32)],
        grid=(bsz, nc),
        in_specs=dir_specs(0) + dir_specs(1) + [sspec, mspec],
        out_specs=[pl.BlockSpec((L, hw), lambda b, c: (cidx(0, b, c), 0)),
                   pl.BlockSpec((L, hw), lambda b, c: (cidx(1, b, c), 0)), sspec, mspec],
        scratch_shapes=[pltpu.VMEM((2 * nh, HEAD_DIM, 2 * HEAD_DIM), F32), pltpu.VMEM((2, nh, 128), F32)],
        compiler_params=_params(("parallel", "arbitrary"), blocks, scratch + 16 * 1024 * 1024),
        name="mlstm_scan",
    )(qs, kt, vb, gplanes, qs, kt, vb, gplanes, s0, m0)


def _mlstm_out_kernel(hf_ref, hb_ref, o_ref, g_ref, y_ref):
    hd = HEAD_DIM
    for h in range(C_HEADS):
        cs = slice(h * hd, (h + 1) * hd)
        hn = _head_norm(hf_ref[:, cs] + hb_ref[:, cs], g_ref[:, cs])
        y_ref[:, cs] = (hn * jax.nn.sigmoid(o_ref[:, cs])).astype(y_ref.dtype)


def _mlstm_out_call(hf, hb, p_main, norm_g):
    m, hw = hf.shape
    tr = _pick(m, (256, 128))
    blocks = 3 * _nbytes((tr, hw), F32) + _nbytes((tr, hw), BF16)
    hspec = pl.BlockSpec((tr, hw), lambda i: (i, 0))
    return pl.pallas_call(
        _mlstm_out_kernel,
        out_shape=jax.ShapeDtypeStruct((m, hw), BF16),
        grid=(m // tr,),
        in_specs=[hspec, hspec,
                  pl.BlockSpec((tr, hw), lambda i: (i, COL_CO)),
                  pl.BlockSpec((1, hw), lambda i: (0, 0))],
        out_specs=hspec,
        compiler_params=_params(("parallel",), blocks, 2 * 1024 * 1024),
        name="mlstm_out",
    )(hf, hb, p_main, norm_g.reshape(1, hw))


def _rope_lane_tables(n):
    t = jnp.arange(n)
    inv = ROPE_THETA ** (-jnp.arange(ROPE_PAIRS, dtype=F32) / ROPE_PAIRS)
    row = (t // GRID_W).astype(F32)[:, None] * inv
    col = (t % GRID_W).astype(F32)[:, None] * inv
    cos = jnp.concatenate([jnp.cos(row), jnp.cos(row), jnp.cos(col), jnp.cos(col)], axis=1)
    sin = jnp.concatenate([-jnp.sin(row), jnp.sin(row), -jnp.sin(col), jnp.sin(col)], axis=1)
    return cos, sin


def _split_w_in(w_in):
    aq_end = BRANCH_W
    akv_end = aq_end + 2 * A_KV_HEADS * HEAD_DIM
    pad = jnp.zeros(w_in.shape[:-1] + (PROJ_W - w_in.shape[-1],), w_in.dtype)
    return jnp.concatenate([w_in[..., :aq_end], w_in[..., akv_end:MAIN_W], w_in[..., aq_end:akv_end],
                            w_in[..., MAIN_W:], pad], axis=-1).astype(BF16)


def kernel(x, c, ctx, c_ctx, w_mod, b_mod, norm_g, ffn_w1, ffn_w3, ffn_w2, w_in, qk_g, attn_sink, na_relpos,
           mlstm_conv_w, mlstm_conv_b, mlstm_gate_b, mlstm_norm_g, w_gate, b_gate, w_branch, w_out):
    bsz, n, d = x.shape
    lc = ctx.shape[1]
    depth = w_mod.shape[0]
    assert n % GRID_W == 0 and n % M_CHUNK == 0 and lc % M_CHUNK == 0 and n % A_BLOCK == 0

    cs = jnp.zeros((8, d), F32).at[:bsz].set(c).at[bsz].set(c_ctx)
    mods = _mod_call(cs, w_mod, b_mod).reshape(depth, 8, N_MOD, d)
    cos_t, sin_t = _rope_lane_tables(n)

    xl = x.reshape(bsz * n, d)
    xc = ctx.reshape(bsz * lc, d)
    nh = C_HEADS
    wg = w_gate.astype(BF16)
    wb = w_branch.astype(BF16)
    w_main = _split_w_in(w_in)
    s_zero = jnp.zeros((bsz * 2 * nh, HEAD_DIM, 2 * HEAD_DIM), F32)
    m_zero = jnp.zeros((bsz * 2, nh, 128), F32)

    def ffn(xs, md, k0, g, layer, sub):
        u = _normmod_call(xs, g, md[:, k0], md[:, k0 + 1])
        h = _up_call(u, ffn_w1, ffn_w3, layer, sub)
        return _down_call(h, ffn_w2, xs, md[:, k0 + 2], 0.5, lead=(layer, sub))

    for i in range(depth):
        last = i == depth - 1
        ml = mods[i, :bsz]
        mc = mods[i, bsz:bsz + 1]
        gate_b = jnp.pad(mlstm_gate_b[i].reshape(1, 4 * nh), ((0, 0), (0, GATE_W - 4 * nh)))
        conv_b = mlstm_conv_b[i].reshape(1, -1)
        bias_tab = _na_bias_table(na_relpos[i])

        xc = ffn(xc, mc, 0, norm_g[i, 0], i, 0)
        xl = ffn(xl, ml, 0, norm_g[i, 0], i, 0)

        uc = _normmod_call(xc, norm_g[i, 1], mc[:, 3], mc[:, 4])
        ul = _normmod_call(xl, norm_g[i, 1], ml[:, 3], ml[:, 4])
        pc = _inproj_call(uc, w_main, i)
        pl_ = _inproj_call(ul, w_main, i)

        qa_c, qb_c, kb_c, vb_c, ka_c, va_c = _qkprep_call(pc, qk_g[i], cos_t, sin_t, False, lc)
        qa_l, qb_l, kb_l, vb_l, ka_l, va_l = _qkprep_call(pl_, qk_g[i], cos_t, sin_t, True, n)

        a_l = _attn_a_call(qa_l, ka_l, va_l, ka_c, va_c, attn_sink[i], bsz, True)
        n_l = _attn_b_call(qb_l, kb_l, vb_l, kb_c, vb_c, bias_tab, bsz)
        qc_c, kc_c, vc_c = _cprep_call(pc, mlstm_conv_w[i], conv_b, lc)
        qc_l, kc_l, vc_l = _cprep_call(pl_, mlstm_conv_w[i], conv_b, n)
        hc_f, hc_b, s_c, m_c = _mlstm_call(qc_c, kc_c, vc_c, _gateprep_call(pc, gate_b, lc), s_zero, m_zero, bsz)
        hl_f, hl_b, _, _ = _mlstm_call(qc_l, kc_l, vc_l, _gateprep_call(pl_, gate_b, n), s_c, m_c, bsz)
        m_l = _mlstm_out_call(hl_f, hl_b, pl_, mlstm_norm_g[i])

        yl = _merge_call(ul, a_l, n_l, m_l, wg, b_gate, wb, i)
        xl = _down_call(yl, w_out, xl, ml[:, 5], 1.0, lead=(i,))
        xl = ffn(xl, ml, 6, norm_g[i, 2], i, 1)
        if not last:
            a_c = _attn_a_call(qa_c, ka_c, va_c, ka_c, va_c, attn_sink[i], bsz, False)
            n_c = _attn_bx_call(qb_c, kb_c, vb_c, bsz)
            mm_c = _mlstm_out_call(hc_f, hc_b, pc, mlstm_norm_g[i])
            yc = _merge_call(uc, a_c, n_c, mm_c, wg, b_gate, wb, i)
            xc = _down_call(yc, w_out, xc, mc[:, 5], 1.0, lead=(i,))
            xc = ffn(xc, mc, 6, norm_g[i, 2], i, 1)
    return xl.reshape(bsz, n, d)
```
